```python
import jax, jax.numpy as jnp
from jax import lax
import numpy as np

D_MODEL = 2048
BATCH = 4
SEQ = 2048
DEPTH = 1
DEC_BATCH = 128
DEC_SEQ = 1
PAST_LEN = 16384
PAGE_SIZE = 128

CHUNK = 128
GMLP_GROUPS = 6
GMLP_GROUP_CH = 128
GMLP_D = GMLP_GROUPS * GMLP_GROUP_CH
RWKV_HEADS = 12
RWKV_HD = 64
RWKV_D = RWKV_HEADS * RWKV_HD
LORA_W = 64
LORA_A = 64
LORA_G = 128
RWKV_COLS = 3 * RWKV_D + LORA_W + LORA_A + LORA_G
XA_HEADS = 4
XA_HD = 128
XA_D = XA_HEADS * XA_HD
N_MEM = 256
N_BRANCH = 3
GMLP_COLS = 2 * GMLP_D
GATE_COLS = N_BRANCH * D_MODEL
D_IN = GMLP_COLS + RWKV_COLS + XA_D + GATE_COLS
PEER_HEADS = 8
N_KEYS = 128
N_EXPERTS = N_KEYS * N_KEYS
PEER_DQ = 256
PEER_DH = PEER_DQ // 2
TOPK_HALF = 16
TOPK = 16
PEER_BLOCK = 256
RMS_EPS = 1e-6
LN_EPS = 1e-5
GN_EPS = 64e-5

kernel_name = 'hybrid_gmlp_rwkv7_memxattn_peer_step'


def rmsnorm(x, g):
    xf = x.astype(jnp.float32)
    y = xf * lax.rsqrt(jnp.mean(xf * xf, axis=-1, keepdims=True) + RMS_EPS)
    return (y * g.astype(jnp.float32)).astype(x.dtype)


def layernorm(x, g, b):
    xf = x.astype(jnp.float32)
    xc = xf - jnp.mean(xf, axis=-1, keepdims=True)
    var = jnp.mean(xc * xc, axis=-1, keepdims=True)
    y = xc * lax.rsqrt(var + LN_EPS) * g.astype(jnp.float32) + b.astype(jnp.float32)
    return y.astype(x.dtype)


def gmlp_spatial_gate(v, ws, bs):
    B, T, _ = v.shape
    n_chunks = -(-T // CHUNK)
    pad = n_chunks * CHUNK - T
    vp = jnp.pad(v, ((0, 0), (0, pad), (0, 0))).reshape(B, n_chunks, CHUNK, GMLP_GROUPS, GMLP_GROUP_CH)
    mask = jnp.tril(jnp.ones((CHUNK, CHUNK), dtype=bool))
    wm = jnp.where(mask, ws, 0.0).astype(v.dtype)
    mixed = jnp.einsum('gts,bcsgd->bctgd', wm, vp) + bs.T.astype(v.dtype)[None, None, :, :, None]
    return mixed.reshape(B, n_chunks * CHUNK, GMLP_D)[:, :T]


def rwkv_recurrence(S0, r, w, k, v, kk, a):
    def step(S, inp):
        r_t, w_t, k_t, v_t, kk_t, a_t = inp
        sa = jnp.einsum('bhij,bhj->bhi', S, -kk_t)
        S = (S * w_t[:, :, None, :] + sa[..., None] * (kk_t * a_t)[:, :, None, :]
             + v_t[..., None] * k_t[:, :, None, :])
        y = jnp.einsum('bhij,bhj->bhi', S, r_t)
        return S, y
    xs = tuple(jnp.moveaxis(t, 1, 0) for t in (r, w, k, v, kk, a))
    S, ys = lax.scan(step, S0, xs)
    return jnp.moveaxis(ys, 0, 1), S


def memory_kv(mem, g, w):
    B, M, _ = mem.shape
    kv = rmsnorm(mem, g) @ w
    k, v = jnp.split(kv, 2, axis=-1)
    return k.reshape(B, M, XA_HEADS, XA_HD), v.reshape(B, M, XA_HEADS, XA_HD)


def mixer_branches(xn, shift0, S0, mem_k, mem_v, p):
    f32 = jnp.float32
    B, T, _ = xn.shape
    z = xn @ p['w_in']
    zg, zr, zq, zgate = jnp.split(
        z, [GMLP_COLS, GMLP_COLS + RWKV_COLS, GMLP_COLS + RWKV_COLS + XA_D], axis=-1)

    u, v = jnp.split(jax.nn.gelu(zg), 2, axis=-1)
    v = layernorm(v, p['gmlp_ln_g'], p['gmlp_ln_b'])
    o_g = u * gmlp_spatial_gate(v, p['gmlp_ws'], p['gmlp_bs'])

    prev = jnp.concatenate([shift0[:, None, :].astype(zr.dtype), zr[:, :-1]], axis=1)
    zs = (zr + (prev - zr) * p['rwkv_mu'].astype(zr.dtype)).astype(f32)
    r, k, vr, lw, la, lg = jnp.split(
        zs, [RWKV_D, 2 * RWKV_D, 3 * RWKV_D, 3 * RWKV_D + LORA_W, 3 * RWKV_D + LORA_W + LORA_A], axis=-1)
    w_log = -jax.nn.softplus(-(p['rwkv_w0'].astype(f32) + jnp.tanh(lw) @ p['rwkv_w2'].astype(f32))) - 0.5
    decay = jnp.exp(-jnp.exp(w_log))
    a = jax.nn.sigmoid(p['rwkv_a0'].astype(f32) + la @ p['rwkv_a2'].astype(f32))
    g = jax.nn.sigmoid(lg) @ p['rwkv_g2'].astype(f32)

    def heads(t):
        return t.reshape(B, T, RWKV_HEADS, RWKV_HD)

    kk = heads(k * p['rwkv_kk'].astype(f32))
    kk = kk / jnp.maximum(jnp.sqrt(jnp.sum(kk * kk, axis=-1, keepdims=True)), 1e-12)
    k = k * (1.0 + (a - 1.0) * p['rwkv_ka'].astype(f32))
    r_h, k_h, v_h = heads(r), heads(k), heads(vr)
    y, S_new = rwkv_recurrence(S0.astype(f32), r_h, heads(decay), k_h, v_h, kk, heads(a))
    yc = y - jnp.mean(y, axis=-1, keepdims=True)
    y = yc * lax.rsqrt(jnp.mean(yc * yc, axis=-1, keepdims=True) + GN_EPS)
    y = y.reshape(B, T, RWKV_D) * p['rwkv_lnx_g'].astype(f32) + p['rwkv_lnx_b'].astype(f32)
    bonus = jnp.sum(r_h * k_h * p['rwkv_rk'].astype(f32), axis=-1, keepdims=True) * v_h
    o_r = ((y + bonus.reshape(B, T, RWKV_D)) * g).astype(xn.dtype)

    q = zq.reshape(B, T, XA_HEADS, XA_HD)
    s = jnp.einsum('bthd,bmhd->bhtm', q, mem_k.astype(q.dtype)).astype(f32) * (XA_HD ** -0.5)
    pr = jax.nn.softmax(s, axis=-1).astype(xn.dtype)
    o_x = jnp.einsum('bhtm,bmhd->bthd', pr, mem_v.astype(xn.dtype)).reshape(B, T, XA_D)

    gates = jax.nn.sigmoid(zgate.reshape(B, T, N_BRANCH, D_MODEL))
    merged = (gates[:, :, 0] * (o_g @ p['w_up_g']) + gates[:, :, 1] * (o_r @ p['w_up_r'])
              + gates[:, :, 2] * (o_x @ p['w_up_x']))
    return merged @ p['w_out'], zr[:, -1], S_new.astype(xn.dtype), v


def peer_ffn(xn, wq, keys, tab_u, tab_v):
    B, T, D = xn.shape
    x = xn.reshape(B * T, D)
    n = x.shape[0]
    nb = -(-n // PEER_BLOCK)
    xp = jnp.pad(x, ((0, nb * PEER_BLOCK - n), (0, 0))).reshape(nb, PEER_BLOCK, D)

    def one_block(xb):
        q = (xb @ wq).reshape(PEER_BLOCK, PEER_HEADS, 2, PEER_DH)
        s = jnp.einsum('thcd,hckd->thck', q, keys.astype(q.dtype)).astype(jnp.float32)
        v_top, i_top = lax.top_k(s, TOPK_HALF)
        cand = (v_top[:, :, 0, :, None] + v_top[:, :, 1, None, :]).reshape(PEER_BLOCK, PEER_HEADS, TOPK_HALF * TOPK_HALF)
        cand_ids = (i_top[:, :, 0, :, None] * N_KEYS + i_top[:, :, 1, None, :]).reshape(PEER_BLOCK, PEER_HEADS, TOPK_HALF * TOPK_HALF)
        sc, pos = lax.top_k(cand, TOPK)
        ids = jnp.take_along_axis(cand_ids, pos, axis=-1).reshape(PEER_BLOCK, PEER_HEADS * TOPK)
        gate = jax.nn.softmax(sc, axis=-1).reshape(PEER_BLOCK, PEER_HEADS * TOPK)
        h = jax.nn.gelu(jnp.einsum('td,ted->te', xb, tab_u[ids]).astype(jnp.float32))
        coef = (gate * h).astype(xb.dtype)
        return jnp.einsum('te,ted->td', coef, tab_v[ids])

    y = lax.map(one_block, xp)
    return y.reshape(nb * PEER_BLOCK, D)[:n].reshape(B, T, D)


def setup_inputs(seed: int = 0) -> dict:
    key = jax.random.key(seed)
    ks = iter(jax.random.split(key, 48))
    f32 = jnp.float32
    L = DEPTH

    def nrm(shape, scale):
        return jax.random.normal(next(ks), shape, f32) * scale

    def gain(shape):
        return 1.0 + nrm(shape, 0.02)

    return {
        'x_prompt': nrm((BATCH, SEQ, D_MODEL), 1.0),
        'x_sample': nrm((DEC_BATCH, DEC_SEQ, D_MODEL), 1.0),
        'mem_prompt': nrm((BATCH, N_MEM, D_MODEL), 1.0),
        'state_shift': nrm((L, DEC_BATCH, RWKV_COLS), 1.0),
        'state_wkv': nrm((L, DEC_BATCH, RWKV_HEADS, RWKV_HD, RWKV_HD), 0.1),
        'cache_mem_k': nrm((L, DEC_BATCH, N_MEM, XA_HEADS, XA_HD), 1.0),
        'cache_mem_v': nrm((L, DEC_BATCH, N_MEM, XA_HEADS, XA_HD), 1.0),
        'ln1_g': gain((L, D_MODEL)),
        'w_in': nrm((L, D_MODEL, D_IN), D_MODEL ** -0.5),
        'gmlp_ln_g': gain((L, GMLP_D)),
        'gmlp_ln_b': nrm((L, GMLP_D), 0.02),
        'gmlp_ws': nrm((L, GMLP_GROUPS, CHUNK, CHUNK), 0.5 * CHUNK ** -0.5),
        'gmlp_bs': gain((L, GMLP_GROUPS, CHUNK)),
        'rwkv_mu': jax.random.uniform(next(ks), (L, RWKV_COLS), f32),
        'rwkv_w0': nrm((L, RWKV_D), 0.5) - 0.5,
        'rwkv_w2': nrm((L, LORA_W, RWKV_D), 0.3 * LORA_W ** -0.5),
        'rwkv_a0': nrm((L, RWKV_D), 0.1),
        'rwkv_a2': nrm((L, LORA_A, RWKV_D), 0.3 * LORA_A ** -0.5),
        'rwkv_g2': nrm((L, LORA_G, RWKV_D), LORA_G ** -0.5),
        'rwkv_kk': 0.85 + nrm((L, RWKV_D), 0.05),
        'rwkv_ka': 1.0 + nrm((L, RWKV_D), 0.05),
        'rwkv_rk': nrm((L, RWKV_HEADS, RWKV_HD), 0.1),
        'rwkv_lnx_g': gain((L, RWKV_D)),
        'rwkv_lnx_b': nrm((L, RWKV_D), 0.02),
        'mem_norm_g': gain((L, D_MODEL)),
        'w_mem_kv': nrm((L, D_MODEL, 2 * XA_D), D_MODEL ** -0.5),
        'w_up_g': nrm((L, GMLP_D, D_MODEL), GMLP_D ** -0.5),
        'w_up_r': nrm((L, RWKV_D, D_MODEL), RWKV_D ** -0.5),
        'w_up_x': nrm((L, XA_D, D_MODEL), XA_D ** -0.5),
        'w_out': nrm((L, D_MODEL, D_MODEL), D_MODEL ** -0.5),
        'ln2_g': gain((L, D_MODEL)),
        'peer_wq': nrm((L, D_MODEL, PEER_HEADS * PEER_DQ), D_MODEL ** -0.5),
        'peer_keys': nrm((L, PEER_HEADS, 2, N_KEYS, PEER_DH), PEER_DH ** -0.5),
        'peer_u': nrm((L, N_EXPERTS, D_MODEL), D_MODEL ** -0.5),
        'peer_v': nrm((L, N_EXPERTS, D_MODEL), 0.25),
        'final_g': gain((D_MODEL,)),
    }


def reference(x_prompt, x_sample, mem_prompt, state_shift, state_wkv, cache_mem_k, cache_mem_v,
              ln1_g, w_in, gmlp_ln_g, gmlp_ln_b, gmlp_ws, gmlp_bs, rwkv_mu, rwkv_w0, rwkv_w2,
              rwkv_a0, rwkv_a2, rwkv_g2, rwkv_kk, rwkv_ka, rwkv_rk, rwkv_lnx_g, rwkv_lnx_b,
              mem_norm_g, w_mem_kv, w_up_g, w_up_r, w_up_x, w_out, ln2_g, peer_wq, peer_keys,
              peer_u, peer_v, final_g):
    xp, xs = x_prompt, x_sample
    Bp = xp.shape[0]
    p_mk, p_mv, p_sh, p_S, s_sh, s_S, s_gv = [], [], [], [], [], [], []
    for l in range(DEPTH):
        pl = {
            'w_in': w_in[l], 'gmlp_ln_g': gmlp_ln_g[l], 'gmlp_ln_b': gmlp_ln_b[l],
            'gmlp_ws': gmlp_ws[l], 'gmlp_bs': gmlp_bs[l], 'rwkv_mu': rwkv_mu[l],
            'rwkv_w0': rwkv_w0[l], 'rwkv_w2': rwkv_w2[l], 'rwkv_a0': rwkv_a0[l],
            'rwkv_a2': rwkv_a2[l], 'rwkv_g2': rwkv_g2[l], 'rwkv_kk': rwkv_kk[l],
            'rwkv_ka': rwkv_ka[l], 'rwkv_rk': rwkv_rk[l], 'rwkv_lnx_g': rwkv_lnx_g[l],
            'rwkv_lnx_b': rwkv_lnx_b[l], 'w_up_g': w_up_g[l], 'w_up_r': w_up_r[l],
            'w_up_x': w_up_x[l], 'w_out': w_out[l],
        }
        mk, mv = memory_kv(mem_prompt, mem_norm_g[l], w_mem_kv[l])
        shift0 = jnp.zeros((Bp, RWKV_COLS), xp.dtype)
        S0 = jnp.zeros((Bp, RWKV_HEADS, RWKV_HD, RWKV_HD), jnp.float32)
        mix, sh, S, _ = mixer_branches(rmsnorm(xp, ln1_g[l]), shift0, S0, mk, mv, pl)
        h = xp + mix
        xp = h + peer_ffn(rmsnorm(h, ln2_g[l]), peer_wq[l], peer_keys[l], peer_u[l], peer_v[l])
        p_mk.append(mk)
        p_mv.append(mv)
        p_sh.append(sh)
        p_S.append(S)
        mix, sh, S, gv = mixer_branches(rmsnorm(xs, ln1_g[l]), state_shift[l], state_wkv[l],
                                        cache_mem_k[l], cache_mem_v[l], pl)
        h = xs + mix
        xs = h + peer_ffn(rmsnorm(h, ln2_g[l]), peer_wq[l], peer_keys[l], peer_u[l], peer_v[l])
        s_sh.append(sh)
        s_S.append(S)
        s_gv.append(gv)
    y_prompt = rmsnorm(xp, final_g)
    y_sample = rmsnorm(xs, final_g)
    return (y_prompt, y_sample, jnp.stack(p_mk), jnp.stack(p_mv), jnp.stack(p_sh), jnp.stack(p_S),
            jnp.stack(s_sh), jnp.stack(s_S), jnp.stack(s_gv))
```

```python
import functools

import jax
import jax.numpy as jnp
from jax import lax
from jax.experimental import pallas as pl
from jax.experimental.pallas import tpu as pltpu

F32 = jnp.float32
BF16 = jnp.bfloat16
HIGHEST = lax.Precision.HIGHEST

D_MODEL = 2048
CHUNK = 128
GMLP_GROUPS = 6
GMLP_D = 768
RWKV_HEADS = 12
RWKV_HD = 64
RWKV_D = 768
LORA_W = 64
LORA_A = 64
LORA_G = 128
RWKV_COLS = 2560
XA_HEADS = 4
XA_HD = 128
XA_D = 512
N_MEM = 256
GMLP_COLS = 2 * GMLP_D
GATE_COLS = 3 * D_MODEL
PEER_HEADS = 8
N_KEYS = 128
PEER_DH = 128
TOPK = 16
RMS_EPS = 1e-6
LN_EPS = 1e-5
GN_EPS = 64e-5

V7X_LANES = 128
V7X_VMEM_LIMIT_BYTES = 56 * 1024 * 1024

HEAD_PAIRS = RWKV_HEADS // 2
RWKV_CHUNK = 64
RWKV_CHUNKS_PER_STEP = 8
NEG_INF = float("-inf")


def _params(semantics):
    return pltpu.CompilerParams(dimension_semantics=semantics, vmem_limit_bytes=V7X_VMEM_LIMIT_BYTES)


def _largest_divisor(m, candidates):
    return next(c for c in candidates if m % c == 0)


def _gelu(x):
    return 0.5 * x * (1.0 + jnp.tanh(0.7978845608028654 * (x + 0.044715 * (x * x * x))))


def _sigmoid(x):
    return 1.0 / (1.0 + jnp.exp(-x))


def _rmsnorm_kernel(x_ref, g_ref, *o_refs, transposed):
    x = x_ref[...]
    y = x * lax.rsqrt(jnp.mean(x * x, axis=-1, keepdims=True) + RMS_EPS) * g_ref[...]
    o_refs[0][...] = y.astype(o_refs[0].dtype)
    if transposed:
        o_refs[1][...] = y.T.astype(o_refs[1].dtype)


def rmsnorm(x, g, *, tb, out_dtype, transposed=False):
    m, d = x.shape
    out_shape = [jax.ShapeDtypeStruct((m, d), out_dtype)]
    out_specs = [pl.BlockSpec((tb, d), lambda i: (i, 0))]
    if transposed:
        out_shape.append(jax.ShapeDtypeStruct((d, m), out_dtype))
        out_specs.append(pl.BlockSpec((d, tb), lambda i: (0, i)))
    res = pl.pallas_call(
        functools.partial(_rmsnorm_kernel, transposed=transposed),
        grid=(m // tb,),
        in_specs=[pl.BlockSpec((tb, d), lambda i: (i, 0)), pl.BlockSpec((1, d), lambda i: (0, 0))],
        out_specs=out_specs,
        out_shape=out_shape,
        compiler_params=_params(("parallel",)),
        name="rmsnorm_t" if transposed else "rmsnorm",
    )(x, g.reshape(1, d))
    return res if transposed else res[0]


def _matmul_kernel(a_ref, w_ref, *rest, has_residual):
    if has_residual:
        r_ref, o_ref, wbf_ref = rest
    else:
        o_ref, wbf_ref = rest

    @pl.when(pl.program_id(1) == 0)
    def _():
        wbf_ref[...] = w_ref[...].astype(BF16)

    acc = jnp.dot(a_ref[...], wbf_ref[...], preferred_element_type=F32)
    if has_residual:
        acc = acc + r_ref[...]
    o_ref[...] = acc.astype(o_ref.dtype)


def matmul(a, w, *, tb, nb, col_off=0, n_cols=None, residual=None, out_dtype=F32, name="matmul"):
    m, k = a.shape
    n = w.shape[1] if n_cols is None else n_cols
    off = col_off // nb
    in_specs = [pl.BlockSpec((tb, k), lambda j, i: (i, 0)),
                pl.BlockSpec((k, nb), lambda j, i: (0, j + off))]
    args = [a, w]
    if residual is not None:
        in_specs.append(pl.BlockSpec((tb, nb), lambda j, i: (i, j)))
        args.append(residual)
    return pl.pallas_call(
        functools.partial(_matmul_kernel, has_residual=residual is not None),
        grid=(n // nb, m // tb),
        in_specs=in_specs,
        out_specs=pl.BlockSpec((tb, nb), lambda j, i: (i, j)),
        out_shape=jax.ShapeDtypeStruct((m, n), out_dtype),
        scratch_shapes=[pltpu.VMEM((k, nb), BF16)],
        compiler_params=_params(("arbitrary", "arbitrary")),
        name=name,
    )(*args)


def _layernorm(v, g, b):
    vc = v - jnp.mean(v, axis=-1, keepdims=True)
    var = jnp.mean(vc * vc, axis=-1, keepdims=True)
    return vc * lax.rsqrt(var + LN_EPS) * g + b


def _gmlp_kernel(z_ref, lng_ref, lnb_ref, ws_ref, bst_ref, o_ref):
    ge = _gelu(z_ref[...])
    u = ge[:, :GMLP_D]
    v = _layernorm(ge[:, GMLP_D:], lng_ref[...], lnb_ref[...])
    row = lax.broadcasted_iota(jnp.int32, (CHUNK, CHUNK), 0)
    col = lax.broadcasted_iota(jnp.int32, (CHUNK, CHUNK), 1)
    causal = col <= row
    for g in range(GMLP_GROUPS):
        sl = slice(g * CHUNK, (g + 1) * CHUNK)
        wm = jnp.where(causal, ws_ref[g], 0.0).astype(BF16)
        mixed = jnp.dot(wm, v[:, sl].astype(BF16), preferred_element_type=F32) + bst_ref[:, g:g + 1]
        o_ref[:, sl] = (u[:, sl] * mixed).astype(o_ref.dtype)


def gmlp_prompt(z_g, n_tokens, ln_g, ln_b, ws, bs):
    return pl.pallas_call(
        _gmlp_kernel,
        grid=(n_tokens // CHUNK,),
        in_specs=[pl.BlockSpec((CHUNK, GMLP_COLS), lambda i: (i, 0)),
                  pl.BlockSpec((1, GMLP_D), lambda i: (0, 0)),
                  pl.BlockSpec((1, GMLP_D), lambda i: (0, 0)),
                  pl.BlockSpec((GMLP_GROUPS, CHUNK, CHUNK), lambda i: (0, 0, 0)),
                  pl.BlockSpec((CHUNK, GMLP_GROUPS), lambda i: (0, 0))],
        out_specs=pl.BlockSpec((CHUNK, GMLP_D), lambda i: (i, 0)),
        out_shape=jax.ShapeDtypeStruct((n_tokens, GMLP_D), BF16),
        compiler_params=_params(("parallel",)),
        name="gmlp_prompt",
    )(z_g, ln_g.reshape(1, GMLP_D), ln_b.reshape(1, GMLP_D), ws, bs.T)


def _head_block_ones():
    r = lax.broadcasted_iota(jnp.int32, (V7X_LANES, V7X_LANES), 0) // RWKV_HD
    c = lax.broadcasted_iota(jnp.int32, (V7X_LANES, V7X_LANES), 1) // RWKV_HD
    return (r == c).astype(F32)


def _head_sum(x, ones_bd):
    return jnp.dot(x, ones_bd, preferred_element_type=F32, precision=HIGHEST)


def _rwkv_mix(zs, w0, w2p, a0, a2p, g2, kkp, ka, rk, ones_bd):
    r = zs[:, 0:RWKV_D]
    k = zs[:, RWKV_D:2 * RWKV_D]
    v = zs[:, 2 * RWKV_D:3 * RWKV_D]
    lwa = zs[:, 3 * RWKV_D:3 * RWKV_D + LORA_W + LORA_A]
    lg = zs[:, 3 * RWKV_D + LORA_W + LORA_A:]
    lora_w = jnp.dot(jnp.tanh(lwa).astype(BF16), w2p, preferred_element_type=F32)
    lora_a = jnp.dot(lwa.astype(BF16), a2p, preferred_element_type=F32)
    x = -(w0 + lora_w)
    softplus = jnp.maximum(x, 0.0) + jnp.log(1.0 + jnp.exp(-jnp.abs(x)))
    logw = -jnp.exp(-softplus - 0.5)
    a = _sigmoid(a0 + lora_a)
    g = jnp.dot(_sigmoid(lg).astype(BF16), g2, preferred_element_type=F32)
    kk = k * kkp
    k2 = k * (1.0 + (a - 1.0) * ka)
    rkk = r * k2 * rk
    out = []
    for p in range(HEAD_PAIRS):
        sl = slice(p * V7X_LANES, (p + 1) * V7X_LANES)
        kk_p = kk[:, sl]
        norm = jnp.sqrt(_head_sum(kk_p * kk_p, ones_bd))
        kkn = kk_p / jnp.maximum(norm, 1e-12)
        bonus = _head_sum(rkk[:, sl], ones_bd) * v[:, sl]
        out.append((r[:, sl], k2[:, sl], v[:, sl], logw[:, sl], kkn, a[:, sl], g[:, sl], bonus))
    return out


def _rwkv_prep_kernel(z_ref, zp_ref, mu_ref, w0_ref, w2_ref, a0_ref, a2_ref, g2_ref, kkp_ref, ka_ref, rk_ref,
                      at_ref, btp_ref, ktp_ref, rt_ref, v_ref, g_ref, bonus_ref, w_ref, pc_ref, cs_scr,
                      *, tb, blocks_per_seq):
    i = pl.program_id(0)
    z = z_ref[...]
    first = (i % blocks_per_seq) == 0
    prev_row = jnp.where(first, 0.0, zp_ref[7:8, :])
    row = lax.broadcasted_iota(jnp.int32, (tb, 1), 0)
    zprev = jnp.where(row == 0, prev_row, pltpu.roll(z, 1, axis=0))
    zs = z + (zprev - z) * mu_ref[...]
    ones_bd = _head_block_ones()
    mixed = _rwkv_mix(zs, w0_ref[...], w2_ref[...], a0_ref[...], a2_ref[...], g2_ref[...], kkp_ref[...],
                      ka_ref[...], rk_ref[...], ones_bd)
    rowc = row % RWKV_CHUNK
    n_chunks = tb // RWKV_CHUNK
    for p, (r, k2, v, logw, kkn, a, g, bonus) in enumerate(mixed):
        cs = logw
        shift = 1
        while shift < RWKV_CHUNK:
            cs = cs + jnp.where(rowc >= shift, pltpu.roll(cs, shift, axis=0), 0.0)
            shift *= 2
        cs_scr[...] = cs
        cs_end = cs_scr[pl.ds(RWKV_CHUNK - 1, n_chunks, stride=RWKV_CHUNK), :]
        rs = logw
        shift = 1
        while shift < RWKV_CHUNK:
            rs = rs + jnp.where(rowc + shift < RWKV_CHUNK, pltpu.roll(rs, tb - shift, axis=0), 0.0)
            shift *= 2
        tail = jnp.exp(rs - logw)
        inv_p = jnp.exp(-cs)
        at_ref[p] = jnp.exp(cs - logw) * kkn
        btp_ref[p] = kkn * a * tail
        ktp_ref[p] = k2 * tail
        rt_ref[p] = jnp.exp(cs) * r
        v_ref[p] = v
        g_ref[p] = g
        bonus_ref[p] = bonus
        w_ref[p] = jnp.concatenate([kkn * a * inv_p, k2 * inv_p], axis=-1)
        pc_ref[p] = jnp.exp(cs_end)


def rwkv_prep(z_r, n_tokens, seq_len, mu, w0, w2p, a0, a2p, g2, kkp, ka, rk, *, tb=512):
    n_chunks = n_tokens // RWKV_CHUNK
    row_spec = lambda w: pl.BlockSpec((1, w), lambda i: (0, 0))
    pair_out = lambda w: pl.BlockSpec((HEAD_PAIRS, tb, w), lambda i: (0, i, 0))
    pair_shape = lambda w: jax.ShapeDtypeStruct((HEAD_PAIRS, n_tokens, w), F32)
    return pl.pallas_call(
        functools.partial(_rwkv_prep_kernel, tb=tb, blocks_per_seq=seq_len // tb),
        grid=(n_tokens // tb,),
        in_specs=[pl.BlockSpec((tb, RWKV_COLS), lambda i: (i, 0)),
                  pl.BlockSpec((8, RWKV_COLS), lambda i: (jnp.maximum(i * (tb // 8) - 1, 0), 0)),
                  row_spec(RWKV_COLS), row_spec(RWKV_D),
                  pl.BlockSpec((LORA_W + LORA_A, RWKV_D), lambda i: (0, 0)),
                  row_spec(RWKV_D),
                  pl.BlockSpec((LORA_W + LORA_A, RWKV_D), lambda i: (0, 0)),
                  pl.BlockSpec((LORA_G, RWKV_D), lambda i: (0, 0)),
                  row_spec(RWKV_D), row_spec(RWKV_D), row_spec(RWKV_D)],
        out_specs=[pair_out(V7X_LANES)] * 7 + [pair_out(2 * V7X_LANES),
                   pl.BlockSpec((HEAD_PAIRS, tb // RWKV_CHUNK, V7X_LANES), lambda i: (0, i, 0))],
        out_shape=[pair_shape(V7X_LANES)] * 7 + [pair_shape(2 * V7X_LANES),
                   jax.ShapeDtypeStruct((HEAD_PAIRS, n_chunks, V7X_LANES), F32)],
        scratch_shapes=[pltpu.VMEM((tb, V7X_LANES), F32)],
        compiler_params=_params(("parallel",)),
        name="rwkv_prep",
    )(z_r, z_r, mu, w0, w2p, a0, a2p, g2, kkp, ka, rk)


def _bmm(x, y):
    return lax.dot_general(x, y, (((2,), (1,)), ((0,), (0,))), preferred_element_type=F32, precision=HIGHEST)


def _bmm_nt(x, y):
    return lax.dot_general(x, y, (((2,), (2,)), ((0,), (0,))), preferred_element_type=F32, precision=HIGHEST)


def _mm_tn(x, y):
    return lax.dot_general(x, y, (((0,), (0,)), ((), ())), preferred_element_type=F32, precision=HIGHEST)


def _group_norm_gate(y, bonus, g, lng, lnb, ones_bd):
    mean = _head_sum(y, ones_bd) * (1.0 / RWKV_HD)
    yc = y - mean
    var = _head_sum(yc * yc, ones_bd) * (1.0 / RWKV_HD)
    return (yc * lax.rsqrt(var + GN_EPS) * lng + lnb + bonus) * g


def _rwkv_chunk_kernel(at_ref, btp_ref, ktp_ref, rt_ref, v_ref, g_ref, bonus_ref, w_ref, pc_ref, lng_ref,
                       lnb_ref, o_ref, st_ref, st_scr):
    nc, c, ln = RWKV_CHUNKS_PER_STEP, RWKV_CHUNK, V7X_LANES
    t = pl.program_id(1)

    @pl.when(t == 0)
    def _():
        st_scr[...] = jnp.zeros_like(st_scr)

    shape3 = (nc, c, ln)
    at = at_ref[0].reshape(shape3)
    rt = rt_ref[0].reshape(shape3)
    v = v_ref[0].reshape(shape3)
    btk = w_ref[0].reshape(nc, c, 2 * ln)
    bt, kt = btk[..., :ln], btk[..., ln:]
    lane = lax.broadcasted_iota(jnp.int32, (1, 1, ln), 2)
    head0 = lane < RWKV_HD
    head0_2 = jnp.concatenate([head0, head0], axis=-1)
    ri = lax.broadcasted_iota(jnp.int32, (1, c, c), 1)
    ci = lax.broadcasted_iota(jnp.int32, (1, c, c), 2)
    strict = ri > ci
    incl = ri >= ci

    xs, mvs, mrbs = [], [], []
    for h in range(2):
        hm = head0 if h == 0 else jnp.logical_not(head0)
        a_h = jnp.where(hm, at, 0.0)
        r_h = jnp.where(hm, rt, 0.0)
        lab = jnp.where(strict, _bmm_nt(a_h, bt), 0.0)
        lak = jnp.where(strict, _bmm_nt(a_h, kt), 0.0)
        mrb = jnp.where(incl, _bmm_nt(r_h, bt), 0.0)
        mrk = jnp.where(incl, _bmm_nt(r_h, kt), 0.0)
        x = jnp.concatenate([at, _bmm(lak, v)], axis=-1)
        lp = -lab
        x = x + _bmm(lp, x)
        n = 2
        while n < c:
            lp = _bmm(lp, lp)
            x = x + _bmm(lp, x)
            n *= 2
        xs.append(x)
        mvs.append(_bmm(mrk, v))
        mrbs.append(mrb)
    x = jnp.where(head0_2, xs[0], xs[1])
    mv = jnp.where(head0, mvs[0], mvs[1])
    mwu = jnp.where(head0_2, _bmm(mrbs[0], x), _bmm(mrbs[1], x))
    qe = rt - mwu[..., :ln]
    yl = mv - mwu[..., ln:]

    rr = lax.broadcasted_iota(jnp.int32, (ln, ln), 0)
    cc = lax.broadcasted_iota(jnp.int32, (ln, ln), 1)
    same_head = (rr // RWKV_HD) == (cc // RWKV_HD)
    eye = rr == cc
    btp = btp_ref[0].reshape(shape3)
    ktp = ktp_ref[0].reshape(shape3)
    pc = pc_ref[0]
    st = st_scr[...]
    ys = []
    for j in range(nc):
        bwu = _mm_tn(btp[j], x[j])
        kv = _mm_tn(ktp[j], v[j])
        tr = jnp.where(same_head, jnp.where(eye, pc[j:j + 1, :], 0.0) - bwu[:, :ln], 0.0)
        ad = jnp.where(same_head, kv - bwu[:, ln:], 0.0)
        ys.append(jnp.dot(qe[j], st, preferred_element_type=F32, precision=HIGHEST) + yl[j])
        st = jnp.dot(tr, st, preferred_element_type=F32, precision=HIGHEST) + ad
    st_scr[...] = st
    st_ref[0] = st
    y = jnp.concatenate(ys, axis=0)
    o = _group_norm_gate(y, bonus_ref[0], g_ref[0], lng_ref[0], lnb_ref[0], _head_block_ones())
    o_ref[0] = o.astype(o_ref.dtype)


def rwkv_chunks(prep, n_batch, seq_len, lnx_g, lnx_b):
    at, btp, ktp, rt, v, g, bonus, wbk, pc = prep
    rows = RWKV_CHUNK * RWKV_CHUNKS_PER_STEP
    steps = seq_len // rows
    n_tokens = n_batch * seq_len
    tok = lambda w: pl.BlockSpec((1, rows, w), lambda bp, t: (bp % HEAD_PAIRS, (bp // HEAD_PAIRS) * steps + t, 0))
    pair_row = pl.BlockSpec((1, 1, V7X_LANES), lambda bp, t: (bp % HEAD_PAIRS, 0, 0))
    return pl.pallas_call(
        _rwkv_chunk_kernel,
        grid=(n_batch * HEAD_PAIRS, steps),
        in_specs=[tok(V7X_LANES)] * 7 + [tok(2 * V7X_LANES),
                  pl.BlockSpec((1, RWKV_CHUNKS_PER_STEP, V7X_LANES),
                               lambda bp, t: (bp % HEAD_PAIRS, (bp // HEAD_PAIRS) * steps + t, 0)),
                  pair_row, pair_row],
        out_specs=[tok(V7X_LANES),
                   pl.BlockSpec((1, V7X_LANES, V7X_LANES), lambda bp, t: (bp, 0, 0))],
        out_shape=[jax.ShapeDtypeStruct((HEAD_PAIRS, n_tokens, V7X_LANES), BF16),
                   jax.ShapeDtypeStruct((n_batch * HEAD_PAIRS, V7X_LANES, V7X_LANES), F32)],
        scratch_shapes=[pltpu.VMEM((V7X_LANES, V7X_LANES), F32)],
        compiler_params=_params(("parallel", "arbitrary")),
        name="rwkv_chunks",
    )(at, btp, ktp, rt, v, g, bonus, wbk, pc,
      lnx_g.reshape(HEAD_PAIRS, 1, V7X_LANES), lnx_b.reshape(HEAD_PAIRS, 1, V7X_LANES))


def _softmax_rows(s):
    e = jnp.exp(s - jnp.max(s, axis=-1, keepdims=True))
    return e / jnp.sum(e, axis=-1, keepdims=True)


def _xattn_kernel(q_ref, k_ref, v_ref, o_ref):
    for h in range(XA_HEADS):
        sl = slice(h * XA_HD, (h + 1) * XA_HD)
        s = lax.dot_general(q_ref[:, sl].astype(BF16), k_ref[:, sl].astype(BF16), (((1,), (1,)), ((), ())),
                            preferred_element_type=F32) * (XA_HD ** -0.5)
        pr = _softmax_rows(s)
        o_ref[:, sl] = jnp.dot(pr.astype(BF16), v_ref[:, sl].astype(BF16),
                               preferred_element_type=F32).astype(o_ref.dtype)


def xattn_prompt(z_q, kv, n_batch, seq_len, *, tq=512):
    steps = seq_len // tq
    return pl.pallas_call(
        _xattn_kernel,
        grid=(n_batch, steps),
        in_specs=[pl.BlockSpec((tq, XA_D), lambda b, t: (b * steps + t, 0)),
                  pl.BlockSpec((N_MEM, XA_D), lambda b, t: (b, 0)),
                  pl.BlockSpec((N_MEM, XA_D), lambda b, t: (b, 1))],
        out_specs=pl.BlockSpec((tq, XA_D), lambda b, t: (b * steps + t, 0)),
        out_shape=jax.ShapeDtypeStruct((n_batch * seq_len, XA_D), BF16),
        compiler_params=_params(("parallel", "parallel")),
        name="xattn_prompt",
    )(z_q, kv, kv)


def _sample_kernel(zg_ref, zr_ref, zq_ref, shift_ref, s_ref, ck_ref, cv_ref,
                   lng_ref, lnb_ref, gw_ref, gb_ref,
                   mu_ref, w0_ref, w2_ref, a0_ref, a2_ref, g2_ref, kkp_ref, ka_ref, rk_ref, lnxg_ref, lnxb_ref,
                   og_ref, or_ref, ox_ref, snew_ref, gv_ref, *, sb):
    ge = _gelu(zg_ref[...])
    v = _layernorm(ge[:, GMLP_D:], lng_ref[...], lnb_ref[...])
    gv_ref[...] = v
    og_ref[...] = (ge[:, :GMLP_D] * (v * gw_ref[...] + gb_ref[...])).astype(og_ref.dtype)

    z = zr_ref[...]
    zs = z + (shift_ref[...] - z) * mu_ref[...]
    ones_bd = _head_block_ones()
    mixed = _rwkv_mix(zs, w0_ref[...], w2_ref[...], a0_ref[...], a2_ref[...], g2_ref[...], kkp_ref[...],
                      ka_ref[...], rk_ref[...], ones_bd)
    row8 = lax.broadcasted_iota(jnp.int32, (8, RWKV_HD), 0)
    nt = (((1,), (1,)), ((), ()))
    for p, (r, k2, vv, logw, kkn, a, g, bonus) in enumerate(mixed):
        w = jnp.exp(logw)
        b = kkn * a
        y_rows = []
        for s in range(sb):
            y_heads = []
            for hh in range(2):
                h = 2 * p + hh
                sl = slice(hh * RWKV_HD, (hh + 1) * RWKV_HD)
                st = s_ref[s, h]
                sa = lax.dot_general(-kkn[:, sl], st, nt, preferred_element_type=F32, precision=HIGHEST)
                lhs = jnp.where(row8 == 0, sa[s:s + 1, :], jnp.where(row8 == 1, vv[s:s + 1, sl], 0.0))
                rhs = jnp.where(row8 == 0, b[s:s + 1, sl], jnp.where(row8 == 1, k2[s:s + 1, sl], 0.0))
                st_new = st * w[s:s + 1, sl] + _mm_tn(lhs, rhs)
                snew_ref[s, h] = st_new
                yy = lax.dot_general(r[:, sl], st_new, nt, preferred_element_type=F32, precision=HIGHEST)
                y_heads.append(yy[s:s + 1, :])
            y_rows.append(jnp.concatenate(y_heads, axis=-1))
        y = jnp.concatenate(y_rows, axis=0)
        o = _group_norm_gate(y, bonus, g, lnxg_ref[p], lnxb_ref[p], ones_bd)
        or_ref[p] = o.astype(or_ref.dtype)

    q = zq_ref[...]
    hrow = lax.broadcasted_iota(jnp.int32, (8, XA_D), 0)
    hlane = lax.broadcasted_iota(jnp.int32, (8, XA_D), 1) // XA_HD
    own = hrow == hlane
    outs = []
    for s in range(sb):
        qm = jnp.where(own, q[s:s + 1, :], 0.0).astype(BF16)
        sc = lax.dot_general(qm, ck_ref[s].astype(BF16), nt, preferred_element_type=F32) * (XA_HD ** -0.5)
        pr = _softmax_rows(sc)
        res = jnp.dot(pr.astype(BF16), cv_ref[s].astype(BF16), preferred_element_type=F32)
        outs.append(jnp.sum(jnp.where(own, res, 0.0), axis=0, keepdims=True))
    ox_ref[...] = jnp.concatenate(outs, axis=0).astype(ox_ref.dtype)


def sample_branches(z_g, z_r, z_q, row0, shift, s0, ck, cv, ln_g, ln_b, gw, gb, mu, w0, w2p, a0, a2p, g2, kkp, ka,
                    rk, lnx_g, lnx_b, *, sb=8):
    n = shift.shape[0]
    off = row0 // sb
    tok = lambda w: pl.BlockSpec((sb, w), lambda i: (i + off, 0))
    loc = lambda w: pl.BlockSpec((sb, w), lambda i: (i, 0))
    row = lambda w: pl.BlockSpec((1, w), lambda i: (0, 0))
    full2 = lambda a, b: pl.BlockSpec((a, b), lambda i: (0, 0))
    pair_row = pl.BlockSpec((HEAD_PAIRS, 1, V7X_LANES), lambda i: (0, 0, 0))
    state = pl.BlockSpec((sb, RWKV_HEADS, RWKV_HD, RWKV_HD), lambda i: (i, 0, 0, 0))
    cache = pl.BlockSpec((sb, N_MEM, XA_D), lambda i: (i, 0, 0))
    return pl.pallas_call(
        functools.partial(_sample_kernel, sb=sb),
        grid=(n // sb,),
        in_specs=[tok(GMLP_COLS), tok(RWKV_COLS), tok(XA_D), loc(RWKV_COLS), state, cache, cache,
                  row(GMLP_D), row(GMLP_D), row(GMLP_D), row(GMLP_D),
                  row(RWKV_COLS), row(RWKV_D), full2(LORA_W + LORA_A, RWKV_D), row(RWKV_D),
                  full2(LORA_W + LORA_A, RWKV_D), full2(LORA_G, RWKV_D), row(RWKV_D), row(RWKV_D), row(RWKV_D),
                  pair_row, pair_row],
        out_specs=[loc(GMLP_D), pl.BlockSpec((HEAD_PAIRS, sb, V7X_LANES), lambda i: (0, i, 0)), loc(XA_D),
                   state, loc(GMLP_D)],
        out_shape=[jax.ShapeDtypeStruct((n, GMLP_D), BF16),
                   jax.ShapeDtypeStruct((HEAD_PAIRS, n, V7X_LANES), BF16),
                   jax.ShapeDtypeStruct((n, XA_D), BF16),
                   jax.ShapeDtypeStruct((n, RWKV_HEADS, RWKV_HD, RWKV_HD), F32),
                   jax.ShapeDtypeStruct((n, GMLP_D), F32)],
        compiler_params=_params(("parallel",)),
        name="sample_branches",
    )(z_g, z_r, z_q, shift, s0, ck, cv, ln_g.reshape(1, -1), ln_b.reshape(1, -1), gw, gb,
      mu, w0, w2p, a0, a2p, g2, kkp, ka, rk,
      lnx_g.reshape(HEAD_PAIRS, 1, V7X_LANES), lnx_b.reshape(HEAD_PAIRS, 1, V7X_LANES))


def _merge_kernel(og_ref, or_ref, ox_ref, g0_ref, g1_ref, g2_ref, wg_ref, wr_ref, wx_ref, o_ref,
                  wg_bf, wr_bf, wx_bf):
    @pl.when(pl.program_id(1) == 0)
    def _():
        wg_bf[...] = wg_ref[...].astype(BF16)
        wr_bf[...] = wr_ref[...].astype(BF16)
        wx_bf[...] = wx_ref[...].astype(BF16)

    up_g = jnp.dot(og_ref[...], wg_bf[...], preferred_element_type=F32)
    up_r = jnp.dot(or_ref[0], wr_bf[0:V7X_LANES, :], preferred_element_type=F32)
    for p in range(1, HEAD_PAIRS):
        up_r = up_r + jnp.dot(or_ref[p], wr_bf[p * V7X_LANES:(p + 1) * V7X_LANES, :], preferred_element_type=F32)
    up_x = jnp.dot(ox_ref[...], wx_bf[...], preferred_element_type=F32)
    merged = _sigmoid(g0_ref[...]) * up_g + _sigmoid(g1_ref[...]) * up_r + _sigmoid(g2_ref[...]) * up_x
    o_ref[...] = merged.astype(o_ref.dtype)


def merge(o_g, o_r, o_x, z_gate, w_up_g, w_up_r, w_up_x, *, tb, nb=512):
    m = o_g.shape[0]
    nblk = D_MODEL // nb
    gate = lambda b: pl.BlockSpec((tb, nb), lambda j, i: (i, b * nblk + j))
    wspec = lambda k: pl.BlockSpec((k, nb), lambda j, i: (0, j))
    return pl.pallas_call(
        _merge_kernel,
        grid=(nblk, m // tb),
        in_specs=[pl.BlockSpec((tb, GMLP_D), lambda j, i: (i, 0)),
                  pl.BlockSpec((HEAD_PAIRS, tb, V7X_LANES), lambda j, i: (0, i, 0)),
                  pl.BlockSpec((tb, XA_D), lambda j, i: (i, 0)),
                  gate(0), gate(1), gate(2), wspec(GMLP_D), wspec(RWKV_D), wspec(XA_D)],
        out_specs=pl.BlockSpec((tb, nb), lambda j, i: (i, j)),
        out_shape=jax.ShapeDtypeStruct((m, D_MODEL), BF16),
        scratch_shapes=[pltpu.VMEM((GMLP_D, nb), BF16), pltpu.VMEM((RWKV_D, nb), BF16),
                        pltpu.VMEM((XA_D, nb), BF16)],
        compiler_params=_params(("arbitrary", "arbitrary")),
        name="merge",
    )(o_g, o_r, o_x, z_gate, z_gate, z_gate, w_up_g, w_up_r, w_up_x)


def _extract_top(work, n_rows, tbk):
    riota = lax.broadcasted_iota(jnp.int32, (n_rows, tbk), 0)
    kiota = lax.broadcasted_iota(jnp.int32, (TOPK, tbk), 0)

    def body(p, carry):
        work, rank, vals, idxs = carry
        m = jnp.max(work, axis=0, keepdims=True)
        idx = jnp.min(jnp.where(work == m, riota, n_rows), axis=0, keepdims=True)
        hit = riota == idx
        rank = jnp.where(hit, p, rank)
        work = jnp.where(hit, NEG_INF, work)
        vals = jnp.where(kiota == p, m, vals)
        idxs = jnp.where(kiota == p, idx, idxs)
        return work, rank, vals, idxs

    init = (work, jnp.full((n_rows, tbk), TOPK, jnp.int32), jnp.zeros((TOPK, tbk), F32),
            jnp.zeros((TOPK, tbk), jnp.int32))
    _, rank, vals, idxs = lax.fori_loop(0, TOPK, body, init)
    return rank, vals, idxs


def _peer_topk_kernel(q_ref, keys_ref, r2_ref, lim_ref, e1_ref, e2_ref, *, tbk):
    nt = (((1,), (1,)), ((), ()))
    kiota = lax.broadcasted_iota(jnp.int32, (TOPK, tbk), 0)
    for h in range(PEER_HEADS):
        s, rank, vals = [], [], []
        for c in range(2):
            col = (2 * h + c) * PEER_DH
            sc = lax.dot_general(keys_ref[h, c], q_ref[:, col:col + PEER_DH], nt,
                                 preferred_element_type=F32, precision=HIGHEST)
            rk, vl, _ = _extract_top(sc, N_KEYS, tbk)
            s.append(sc)
            rank.append(rk)
            vals.append(vl)
        cand = jnp.concatenate([vals[0][a:a + 1, :] + vals[1] for a in range(TOPK)], axis=0)
        _, cvals, cidx = _extract_top(cand, TOPK * TOPK, tbk)
        top = cvals[0:1, :]
        z = jnp.sum(jnp.exp(cvals - top), axis=0, keepdims=True)
        count = jnp.zeros((TOPK, tbk), F32)
        for p in range(TOPK):
            count = count + jnp.where(kiota == cidx[p:p + 1, :] // TOPK, 1.0, 0.0)
        lim = jnp.zeros((N_KEYS, tbk), F32)
        for a in range(TOPK):
            lim = jnp.where(rank[0] == a, count[a:a + 1, :], lim)
        lim_ref[h] = lim
        r2_ref[h] = rank[1].astype(F32)
        e1_ref[h] = jnp.exp(s[0] - vals[0][0:1, :]) / z
        e2_ref[h] = jnp.exp(s[1] - vals[1][0:1, :])


def peer_topk(q, keys, *, tbk=128):
    m = q.shape[0]
    out = pl.BlockSpec((PEER_HEADS, N_KEYS, tbk), lambda i: (0, 0, i))
    shape = jax.ShapeDtypeStruct((PEER_HEADS, N_KEYS, m), F32)
    return pl.pallas_call(
        functools.partial(_peer_topk_kernel, tbk=tbk),
        grid=(m // tbk,),
        in_specs=[pl.BlockSpec((tbk, PEER_HEADS * 2 * PEER_DH), lambda i: (i, 0)),
                  pl.BlockSpec((PEER_HEADS, 2, N_KEYS, PEER_DH), lambda i: (0, 0, 0, 0))],
        out_specs=[out] * 4,
        out_shape=[shape] * 4,
        compiler_params=_params(("parallel",)),
        name="peer_topk",
    )(q, keys)


def _peer_dense_kernel(xt_ref, r2_ref, lim_ref, e1_ref, e2_ref, u_ref, vt_ref, o_ref, *, eb):
    e = pl.program_id(1)

    @pl.when(e == 0)
    def _():
        o_ref[...] = jnp.zeros_like(o_ref)

    ht = jnp.dot(u_ref[...], xt_ref[...], preferred_element_type=F32)
    parts = []
    for ii in range(eb // N_KEYS):
        i = e * (eb // N_KEYS) + ii
        gate = None
        for h in range(PEER_HEADS):
            lim = lim_ref[h, pl.ds(i, 1), :]
            e1 = e1_ref[h, pl.ds(i, 1), :]
            term = jnp.where(r2_ref[h] < lim, e2_ref[h] * e1, 0.0)
            gate = term if gate is None else gate + term
        parts.append((gate * _gelu(ht[ii * N_KEYS:(ii + 1) * N_KEYS, :])).astype(BF16))
    coef = jnp.concatenate(parts, axis=0)
    o_ref[...] += jnp.dot(vt_ref[...], coef, preferred_element_type=F32)


def peer_dense(xt, r2, lim, e1, e2, u_bf, vt_bf, *, tbl, eb=256):
    d, m = xt.shape
    n_exp = u_bf.shape[0]
    head = pl.BlockSpec((PEER_HEADS, N_KEYS, tbl), lambda t, e: (0, 0, t))
    return pl.pallas_call(
        functools.partial(_peer_dense_kernel, eb=eb),
        grid=(m // tbl, n_exp // eb),
        in_specs=[pl.BlockSpec((d, tbl), lambda t, e: (0, t)), head, head, head, head,
                  pl.BlockSpec((eb, d), lambda t, e: (e, 0)),
                  pl.BlockSpec((d, eb), lambda t, e: (0, e))],
        out_specs=pl.BlockSpec((d, tbl), lambda t, e: (0, t)),
        out_shape=jax.ShapeDtypeStruct((d, m), F32),
        compiler_params=_params(("parallel", "arbitrary")),
        name="peer_dense",
    )(xt, r2, lim, e1, e2, u_bf, vt_bf)


def _final_kernel(h_ref, pt_ref, g_ref, o_ref):
    x = h_ref[...] + pt_ref[...].T
    o_ref[...] = x * lax.rsqrt(jnp.mean(x * x, axis=-1, keepdims=True) + RMS_EPS) * g_ref[...]


def final_norm(h, peer_t, g, *, tb):
    m, d = h.shape
    return pl.pallas_call(
        _final_kernel,
        grid=(m // tb,),
        in_specs=[pl.BlockSpec((tb, d), lambda i: (i, 0)), pl.BlockSpec((d, tb), lambda i: (0, i)),
                  pl.BlockSpec((1, d), lambda i: (0, 0))],
        out_specs=pl.BlockSpec((tb, d), lambda i: (i, 0)),
        out_shape=jax.ShapeDtypeStruct((m, d), F32),
        compiler_params=_params(("parallel",)),
        name="final_norm",
    )(h, peer_t, g.reshape(1, d))


def kernel(x_prompt, x_sample, mem_prompt, state_shift, state_wkv, cache_mem_k, cache_mem_v, ln1_g, w_in, gmlp_ln_g, gmlp_ln_b, gmlp_ws, gmlp_bs, rwkv_mu, rwkv_w0, rwkv_w2, rwkv_a0, rwkv_a2, rwkv_g2, rwkv_kk, rwkv_ka, rwkv_rk, rwkv_lnx_g, rwkv_lnx_b, mem_norm_g, w_mem_kv, w_up_g, w_up_r, w_up_x, w_out, ln2_g, peer_wq, peer_keys, peer_u, peer_v, final_g):
    depth = w_in.shape[0]
    assert depth == 1, "single-layer step"
    l = 0
    n_batch, seq_len, d = x_prompt.shape
    n_dec = x_sample.shape[0]
    n_prompt = n_batch * seq_len
    m = n_prompt + n_dec
    tb_small = _largest_divisor(m, (640, 128))
    tb_big = _largest_divisor(m, (1664, 640, 128))

    x = jnp.concatenate([x_prompt.reshape(n_prompt, d), x_sample.reshape(n_dec, d)], axis=0)
    xn = rmsnorm(x, ln1_g[l], tb=tb_small, out_dtype=BF16)
    proj = functools.partial(matmul, xn, w_in[l], tb=tb_big, nb=512)
    z_g = proj(col_off=0, n_cols=GMLP_COLS, name="proj_gmlp")
    z_r = proj(col_off=GMLP_COLS, n_cols=RWKV_COLS, name="proj_rwkv")
    z_q = proj(col_off=GMLP_COLS + RWKV_COLS, n_cols=XA_D, name="proj_xattn")
    z_gate = proj(col_off=GMLP_COLS + RWKV_COLS + XA_D, n_cols=GATE_COLS, name="proj_gate")

    memn = rmsnorm(mem_prompt.reshape(n_batch * N_MEM, d), mem_norm_g[l], tb=256, out_dtype=BF16)
    kv = matmul(memn, w_mem_kv[l], tb=N_MEM, nb=512, name="proj_mem_kv")
    p_mk = kv[:, :XA_D].reshape(1, n_batch, N_MEM, XA_HEADS, XA_HD)
    p_mv = kv[:, XA_D:].reshape(1, n_batch, N_MEM, XA_HEADS, XA_HD)

    row = lambda a: a.reshape(1, -1)
    zeros_lora = jnp.zeros((LORA_W, RWKV_D), F32)
    w2p = jnp.concatenate([rwkv_w2[l], zeros_lora], axis=0).astype(BF16)
    a2p = jnp.concatenate([zeros_lora, rwkv_a2[l]], axis=0).astype(BF16)
    rw = (row(rwkv_mu[l]), row(rwkv_w0[l]), w2p, row(rwkv_a0[l]), a2p, rwkv_g2[l].astype(BF16),
          row(rwkv_kk[l]), row(rwkv_ka[l]), row(rwkv_rk[l]))

    o_g_p = gmlp_prompt(z_g, n_prompt, gmlp_ln_g[l], gmlp_ln_b[l], gmlp_ws[l], gmlp_bs[l])
    prep = rwkv_prep(z_r, n_prompt, seq_len, *rw)
    o_r_p, st = rwkv_chunks(prep, n_batch, seq_len, rwkv_lnx_g[l], rwkv_lnx_b[l])
    o_x_p = xattn_prompt(z_q, kv, n_batch, seq_len)

    gw = jnp.repeat(gmlp_ws[l][:, 0, 0], CHUNK).reshape(1, GMLP_D)
    gb = jnp.repeat(gmlp_bs[l][:, 0], CHUNK).reshape(1, GMLP_D)
    o_g_s, o_r_s, o_x_s, s_wkv, s_gv = sample_branches(
        z_g, z_r, z_q, n_prompt, state_shift[l], state_wkv[l],
        cache_mem_k[l].reshape(n_dec, N_MEM, XA_D), cache_mem_v[l].reshape(n_dec, N_MEM, XA_D),
        gmlp_ln_g[l], gmlp_ln_b[l], gw, gb, *rw, rwkv_lnx_g[l], rwkv_lnx_b[l])

    o_g = jnp.concatenate([o_g_p, o_g_s], axis=0)
    o_r = jnp.concatenate([o_r_p, o_r_s], axis=1)
    o_x = jnp.concatenate([o_x_p, o_x_s], axis=0)
    merged = merge(o_g, o_r, o_x, z_gate, w_up_g[l], w_up_r[l], w_up_x[l], tb=tb_small)
    h = matmul(merged, w_out[l], tb=tb_big, nb=512, residual=x, name="proj_out")

    hn, hn_t = rmsnorm(h, ln2_g[l], tb=tb_small, out_dtype=BF16, transposed=True)
    q = matmul(hn, peer_wq[l], tb=tb_big, nb=512, name="proj_peer_q")
    r2, lim, e1, e2 = peer_topk(q, peer_keys[l])
    peer_t = peer_dense(hn_t, r2, lim, e1, e2, peer_u[l].astype(BF16), peer_v[l].T.astype(BF16), tbl=tb_small)
    y = final_norm(h, peer_t, final_g, tb=tb_small)

    st = st.reshape(n_batch, HEAD_PAIRS, 2, RWKV_HD, 2, RWKV_HD)
    p_wkv = jnp.stack([st[:, :, 0, :, 0, :], st[:, :, 1, :, 1, :]], axis=2)
    p_wkv = jnp.swapaxes(p_wkv, -1, -2).reshape(1, n_batch, RWKV_HEADS, RWKV_HD, RWKV_HD)
    p_shift = z_r[:n_prompt].reshape(n_batch, seq_len, RWKV_COLS)[:, -1][None]
    s_shift = z_r[n_prompt:][None]
    return (y[:n_prompt].reshape(n_batch, seq_len, d), y[n_prompt:].reshape(n_dec, 1, d), p_mk, p_mv,
            p_shift, p_wkv, s_shift, s_wkv[None], s_gv.reshape(1, n_dec, 1, GMLP_D))
```

```python
import functools

import jax
import jax.numpy as jnp
from jax import lax
from jax.experimental import pallas as pl
from jax.experimental.pallas import tpu as pltpu

F32 = jnp.float32
BF16 = jnp.bfloat16
HIGHEST = lax.Precision.HIGHEST

D_MODEL = 2048
CHUNK = 128
GMLP_GROUPS = 6
GMLP_D = 768
RWKV_HEADS = 12
RWKV_HD = 64
RWKV_D = 768
LORA_W = 64
LORA_A = 64
LORA_G = 128
RWKV_COLS = 2560
XA_HEADS = 4
XA_HD = 128
XA_D = 512
N_MEM = 256
GMLP_COLS = 2 * GMLP_D
GATE_COLS = 3 * D_MODEL
PEER_HEADS = 8
N_KEYS = 128
PEER_DH = 128
TOPK = 16
RMS_EPS = 1e-6
LN_EPS = 1e-5
GN_EPS = 64e-5

V7X_LANES = 128
V7X_VMEM_LIMIT_BYTES = 56 * 1024 * 1024

HEAD_PAIRS = RWKV_HEADS // 2
RWKV_CHUNK = 64
RWKV_CHUNKS_PER_STEP = 8
NEG_INF = float("-inf")


def _params(semantics):
    return pltpu.CompilerParams(dimension_semantics=semantics, vmem_limit_bytes=V7X_VMEM_LIMIT_BYTES)


def _largest_divisor(m, candidates):
    return next(c for c in candidates if m % c == 0)


def _gelu(x):
    return 0.5 * x * (1.0 + jnp.tanh(0.7978845608028654 * (x + 0.044715 * (x * x * x))))


def _sigmoid(x):
    return 1.0 / (1.0 + jnp.exp(-x))


def _rmsnorm_kernel(x_ref, g_ref, *o_refs, transposed):
    x = x_ref[...]
    y = x * lax.rsqrt(jnp.mean(x * x, axis=-1, keepdims=True) + RMS_EPS) * g_ref[...]
    o_refs[0][...] = y.astype(o_refs[0].dtype)
    if transposed:
        o_refs[1][...] = y.T.astype(o_refs[1].dtype)


def rmsnorm(x, g, *, tb, out_dtype, transposed=False):
    m, d = x.shape
    out_shape = [jax.ShapeDtypeStruct((m, d), out_dtype)]
    out_specs = [pl.BlockSpec((tb, d), lambda i: (i, 0))]
    if transposed:
        out_shape.append(jax.ShapeDtypeStruct((d, m), out_dtype))
        out_specs.append(pl.BlockSpec((d, tb), lambda i: (0, i)))
    res = pl.pallas_call(
        functools.partial(_rmsnorm_kernel, transposed=transposed),
        grid=(m // tb,),
        in_specs=[pl.BlockSpec((tb, d), lambda i: (i, 0)), pl.BlockSpec((1, d), lambda i: (0, 0))],
        out_specs=out_specs,
        out_shape=out_shape,
        compiler_params=_params(("parallel",)),
        name="rmsnorm_t" if transposed else "rmsnorm",
    )(x, g.reshape(1, d))
    return res if transposed else res[0]


def _matmul_kernel(a_ref, w_ref, *rest, has_residual):
    if has_residual:
        r_ref, o_ref, wbf_ref = rest
    else:
        o_ref, wbf_ref = rest

    @pl.when(pl.program_id(1) == 0)
    def _():
        wbf_ref[...] = w_ref[...].astype(BF16)

    acc = jnp.dot(a_ref[...], wbf_ref[...], preferred_element_type=F32)
    if has_residual:
        acc = acc + r_ref[...]
    o_ref[...] = acc.astype(o_ref.dtype)


def matmul(a, w, *, tb, nb, col_off=0, n_cols=None, residual=None, out_dtype=F32, name="matmul"):
    m, k = a.shape
    n = w.shape[1] if n_cols is None else n_cols
    off = col_off // nb
    in_specs = [pl.BlockSpec((tb, k), lambda j, i: (i, 0)),
                pl.BlockSpec((k, nb), lambda j, i: (0, j + off))]
    args = [a, w]
    if residual is not None:
        in_specs.append(pl.BlockSpec((tb, nb), lambda j, i: (i, j)))
        args.append(residual)
    return pl.pallas_call(
        functools.partial(_matmul_kernel, has_residual=residual is not None),
        grid=(n // nb, m // tb),
        in_specs=in_specs,
        out_specs=pl.BlockSpec((tb, nb), lambda j, i: (i, j)),
        out_shape=jax.ShapeDtypeStruct((m, n), out_dtype),
        scratch_shapes=[pltpu.VMEM((k, nb), BF16)],
        compiler_params=_params(("arbitrary", "arbitrary")),
        name=name,
    )(*args)


def _layernorm(v, g, b):
    vc = v - jnp.mean(v, axis=-1, keepdims=True)
    var = jnp.mean(vc * vc, axis=-1, keepdims=True)
    return vc * lax.rsqrt(var + LN_EPS) * g + b


def _gmlp_kernel(z_ref, lng_ref, lnb_ref, ws_ref, bst_ref, o_ref):
    ge = _gelu(z_ref[...])
    u = ge[:, :GMLP_D]
    v = _layernorm(ge[:, GMLP_D:], lng_ref[...], lnb_ref[...])
    row = lax.broadcasted_iota(jnp.int32, (CHUNK, CHUNK), 0)
    col = lax.broadcasted_iota(jnp.int32, (CHUNK, CHUNK), 1)
    causal = col <= row
    for g in range(GMLP_GROUPS):
        sl = slice(g * CHUNK, (g + 1) * CHUNK)
        wm = jnp.where(causal, ws_ref[g], 0.0).astype(BF16)
        mixed = jnp.dot(wm, v[:, sl].astype(BF16), preferred_element_type=F32) + bst_ref[:, g:g + 1]
        o_ref[:, sl] = (u[:, sl] * mixed).astype(o_ref.dtype)


def gmlp_prompt(z_g, n_tokens, ln_g, ln_b, ws, bs):
    return pl.pallas_call(
        _gmlp_kernel,
        grid=(n_tokens // CHUNK,),
        in_specs=[pl.BlockSpec((CHUNK, GMLP_COLS), lambda i: (i, 0)),
                  pl.BlockSpec((1, GMLP_D), lambda i: (0, 0)),
                  pl.BlockSpec((1, GMLP_D), lambda i: (0, 0)),
                  pl.BlockSpec((GMLP_GROUPS, CHUNK, CHUNK), lambda i: (0, 0, 0)),
                  pl.BlockSpec((CHUNK, GMLP_GROUPS), lambda i: (0, 0))],
        out_specs=pl.BlockSpec((CHUNK, GMLP_D), lambda i: (i, 0)),
        out_shape=jax.ShapeDtypeStruct((z_g.shape[0], GMLP_D), BF16),
        compiler_params=_params(("parallel",)),
        name="gmlp_prompt",
    )(z_g, ln_g.reshape(1, GMLP_D), ln_b.reshape(1, GMLP_D), ws, bs.T)


def _head_block_ones():
    r = lax.broadcasted_iota(jnp.int32, (V7X_LANES, V7X_LANES), 0) // RWKV_HD
    c = lax.broadcasted_iota(jnp.int32, (V7X_LANES, V7X_LANES), 1) // RWKV_HD
    return (r == c).astype(F32)


def _head_sum(x, ones_bd):
    return jnp.dot(x, ones_bd, preferred_element_type=F32, precision=HIGHEST)


def _rwkv_mix(zs, w0, w2p, a0, a2p, g2, kkp, ka, rk, ones_bd):
    r = zs[:, 0:RWKV_D]
    k = zs[:, RWKV_D:2 * RWKV_D]
    v = zs[:, 2 * RWKV_D:3 * RWKV_D]
    lwa = zs[:, 3 * RWKV_D:3 * RWKV_D + LORA_W + LORA_A]
    lg = zs[:, 3 * RWKV_D + LORA_W + LORA_A:]
    lora_w = jnp.dot(jnp.tanh(lwa).astype(BF16), w2p, preferred_element_type=F32)
    lora_a = jnp.dot(lwa.astype(BF16), a2p, preferred_element_type=F32)
    x = -(w0 + lora_w)
    softplus = jnp.maximum(x, 0.0) + jnp.log(1.0 + jnp.exp(-jnp.abs(x)))
    logw = -jnp.exp(-softplus - 0.5)
    a = _sigmoid(a0 + lora_a)
    g = jnp.dot(_sigmoid(lg).astype(BF16), g2, preferred_element_type=F32)
    kk = k * kkp
    k2 = k * (1.0 + (a - 1.0) * ka)
    rkk = r * k2 * rk
    out = []
    for p in range(HEAD_PAIRS):
        sl = slice(p * V7X_LANES, (p + 1) * V7X_LANES)
        kk_p = kk[:, sl]
        norm = jnp.sqrt(_head_sum(kk_p * kk_p, ones_bd))
        kkn = kk_p / jnp.maximum(norm, 1e-12)
        bonus = _head_sum(rkk[:, sl], ones_bd) * v[:, sl]
        out.append((r[:, sl], k2[:, sl], v[:, sl], logw[:, sl], kkn, a[:, sl], g[:, sl], bonus))
    return out


def _rwkv_prep_kernel(z_ref, zp_ref, mu_ref, w0_ref, w2_ref, a0_ref, a2_ref, g2_ref, kkp_ref, ka_ref, rk_ref,
                      at_ref, btp_ref, ktp_ref, rt_ref, v_ref, g_ref, bonus_ref, w_ref, pc_ref, cs_scr,
                      *, tb, blocks_per_seq):
    i = pl.program_id(0)
    z = z_ref[...]
    first = (i % blocks_per_seq) == 0
    prev_row = jnp.where(first, 0.0, zp_ref[7:8, :])
    row = lax.broadcasted_iota(jnp.int32, (tb, 1), 0)
    zprev = jnp.where(row == 0, prev_row, pltpu.roll(z, 1, axis=0))
    zs = z + (zprev - z) * mu_ref[...]
    ones_bd = _head_block_ones()
    mixed = _rwkv_mix(zs, w0_ref[...], w2_ref[...], a0_ref[...], a2_ref[...], g2_ref[...], kkp_ref[...],
                      ka_ref[...], rk_ref[...], ones_bd)
    rowc = row % RWKV_CHUNK
    n_chunks = tb // RWKV_CHUNK
    for p, (r, k2, v, logw, kkn, a, g, bonus) in enumerate(mixed):
        cs = logw
        shift = 1
        while shift < RWKV_CHUNK:
            cs = cs + jnp.where(rowc >= shift, pltpu.roll(cs, shift, axis=0), 0.0)
            shift *= 2
        cs_scr[...] = cs
        cs_end = cs_scr[pl.ds(RWKV_CHUNK - 1, n_chunks, stride=RWKV_CHUNK), :]
        rs = logw
        shift = 1
        while shift < RWKV_CHUNK:
            rs = rs + jnp.where(rowc + shift < RWKV_CHUNK, pltpu.roll(rs, tb - shift, axis=0), 0.0)
            shift *= 2
        tail = jnp.exp(rs - logw)
        inv_p = jnp.exp(-cs)
        at_ref[p] = jnp.exp(cs - logw) * kkn
        btp_ref[p] = kkn * a * tail
        ktp_ref[p] = k2 * tail
        rt_ref[p] = jnp.exp(cs) * r
        v_ref[p] = v
        g_ref[p] = g
        bonus_ref[p] = bonus
        w_ref[p] = jnp.concatenate([kkn * a * inv_p, k2 * inv_p], axis=-1)
        pc_ref[p] = jnp.exp(cs_end)


def rwkv_prep(z_r, n_tokens, seq_len, mu, w0, w2p, a0, a2p, g2, kkp, ka, rk, *, tb=512):
    n_chunks = n_tokens // RWKV_CHUNK
    row_spec = lambda w: pl.BlockSpec((1, w), lambda i: (0, 0))
    pair_out = lambda w: pl.BlockSpec((HEAD_PAIRS, tb, w), lambda i: (0, i, 0))
    pair_shape = lambda w: jax.ShapeDtypeStruct((HEAD_PAIRS, n_tokens, w), F32)
    return pl.pallas_call(
        functools.partial(_rwkv_prep_kernel, tb=tb, blocks_per_seq=seq_len // tb),
        grid=(n_tokens // tb,),
        in_specs=[pl.BlockSpec((tb, RWKV_COLS), lambda i: (i, 0)),
                  pl.BlockSpec((8, RWKV_COLS), lambda i: (jnp.maximum(i * (tb // 8) - 1, 0), 0)),
                  row_spec(RWKV_COLS), row_spec(RWKV_D),
                  pl.BlockSpec((LORA_W + LORA_A, RWKV_D), lambda i: (0, 0)),
                  row_spec(RWKV_D),
                  pl.BlockSpec((LORA_W + LORA_A, RWKV_D), lambda i: (0, 0)),
                  pl.BlockSpec((LORA_G, RWKV_D), lambda i: (0, 0)),
                  row_spec(RWKV_D), row_spec(RWKV_D), row_spec(RWKV_D)],
        out_specs=[pair_out(V7X_LANES)] * 7 + [pair_out(2 * V7X_LANES),
                   pl.BlockSpec((HEAD_PAIRS, tb // RWKV_CHUNK, V7X_LANES), lambda i: (0, i, 0))],
        out_shape=[pair_shape(V7X_LANES)] * 7 + [pair_shape(2 * V7X_LANES),
                   jax.ShapeDtypeStruct((HEAD_PAIRS, n_chunks, V7X_LANES), F32)],
        scratch_shapes=[pltpu.VMEM((tb, V7X_LANES), F32)],
        compiler_params=_params(("parallel",)),
        name="rwkv_prep",
    )(z_r, z_r, mu, w0, w2p, a0, a2p, g2, kkp, ka, rk)


def _bmm(x, y):
    return lax.dot_general(x.astype(BF16), y.astype(BF16), (((2,), (1,)), ((0,), (0,))),
                           preferred_element_type=F32)


def _bmm_nt(x, y):
    return lax.dot_general(x.astype(BF16), y.astype(BF16), (((2,), (2,)), ((0,), (0,))),
                           preferred_element_type=F32)


def _mm_tn(x, y, precision=None):
    return lax.dot_general(x, y, (((0,), (0,)), ((), ())), preferred_element_type=F32, precision=precision)


def _group_norm_gate(y, bonus, g, lng, lnb, ones_bd):
    mean = _head_sum(y, ones_bd) * (1.0 / RWKV_HD)
    yc = y - mean
    var = _head_sum(yc * yc, ones_bd) * (1.0 / RWKV_HD)
    return (yc * lax.rsqrt(var + GN_EPS) * lng + lnb + bonus) * g


def _rwkv_chunk_kernel(at_ref, btp_ref, ktp_ref, rt_ref, v_ref, g_ref, bonus_ref, w_ref, pc_ref, lng_ref,
                       lnb_ref, o_ref, st_ref, st_scr):
    nc, c, ln = RWKV_CHUNKS_PER_STEP, RWKV_CHUNK, V7X_LANES
    t = pl.program_id(1)

    @pl.when(t == 0)
    def _():
        st_scr[...] = jnp.zeros_like(st_scr)

    shape3 = (nc, c, ln)
    at = at_ref[0].reshape(shape3)
    rt = rt_ref[0].reshape(shape3)
    v = v_ref[0].reshape(shape3)
    btk = w_ref[0].reshape(nc, c, 2 * ln)
    bt, kt = btk[..., :ln], btk[..., ln:]
    lane = lax.broadcasted_iota(jnp.int32, (1, 1, ln), 2)
    head0 = lane < RWKV_HD
    head0_2 = jnp.concatenate([head0, head0], axis=-1)
    ri = lax.broadcasted_iota(jnp.int32, (1, c, c), 1)
    ci = lax.broadcasted_iota(jnp.int32, (1, c, c), 2)
    strict = ri > ci
    incl = ri >= ci

    xs, mvs, mrbs = [], [], []
    for h in range(2):
        hm = head0 if h == 0 else jnp.logical_not(head0)
        a_h = jnp.where(hm, at, 0.0)
        r_h = jnp.where(hm, rt, 0.0)
        lab = jnp.where(strict, _bmm_nt(a_h, bt), 0.0)
        lak = jnp.where(strict, _bmm_nt(a_h, kt), 0.0)
        mrb = jnp.where(incl, _bmm_nt(r_h, bt), 0.0)
        mrk = jnp.where(incl, _bmm_nt(r_h, kt), 0.0)
        x = jnp.concatenate([at, _bmm(lak, v)], axis=-1)
        lp = -lab
        x = x + _bmm(lp, x)
        n = 2
        while n < c:
            lp = _bmm(lp, lp)
            x = x + _bmm(lp, x)
            n *= 2
        xs.append(x)
        mvs.append(_bmm(mrk, v))
        mrbs.append(mrb)
    x = jnp.where(head0_2, xs[0], xs[1])
    mv = jnp.where(head0, mvs[0], mvs[1])
    mwu = jnp.where(head0_2, _bmm(mrbs[0], x), _bmm(mrbs[1], x))
    qe = rt - mwu[..., :ln]
    yl = mv - mwu[..., ln:]

    rr = lax.broadcasted_iota(jnp.int32, (ln, ln), 0)
    cc = lax.broadcasted_iota(jnp.int32, (ln, ln), 1)
    same_head = (rr // RWKV_HD) == (cc // RWKV_HD)
    eye = rr == cc
    btp = btp_ref[0].reshape(shape3).astype(BF16)
    ktp = ktp_ref[0].reshape(shape3).astype(BF16)
    x_bf = x.astype(BF16)
    v_bf = v.astype(BF16)
    pc = pc_ref[0]
    st = st_scr[...]
    ys = []
    for j in range(nc):
        bwu = _mm_tn(btp[j], x_bf[j])
        kv = _mm_tn(ktp[j], v_bf[j])
        tr = jnp.where(same_head, jnp.where(eye, pc[j:j + 1, :], 0.0) - bwu[:, :ln], 0.0)
        ad = jnp.where(same_head, kv - bwu[:, ln:], 0.0)
        ys.append(jnp.dot(qe[j], st, preferred_element_type=F32, precision=HIGHEST) + yl[j])
        st = jnp.dot(tr, st, preferred_element_type=F32, precision=HIGHEST) + ad
    st_scr[...] = st
    st_ref[0] = st
    y = jnp.concatenate(ys, axis=0)
    o = _group_norm_gate(y, bonus_ref[0], g_ref[0], lng_ref[0], lnb_ref[0], _head_block_ones())
    o_ref[0] = o.astype(o_ref.dtype)


def rwkv_chunks(prep, n_batch, seq_len, n_rows_out, lnx_g, lnx_b):
    at, btp, ktp, rt, v, g, bonus, wbk, pc = prep
    rows = RWKV_CHUNK * RWKV_CHUNKS_PER_STEP
    steps = seq_len // rows
    n_tokens = n_batch * seq_len
    tok = lambda w: pl.BlockSpec((1, rows, w), lambda bp, t: (bp % HEAD_PAIRS, (bp // HEAD_PAIRS) * steps + t, 0))
    pair_row = pl.BlockSpec((1, 1, V7X_LANES), lambda bp, t: (bp % HEAD_PAIRS, 0, 0))
    return pl.pallas_call(
        _rwkv_chunk_kernel,
        grid=(n_batch * HEAD_PAIRS, steps),
        in_specs=[tok(V7X_LANES)] * 7 + [tok(2 * V7X_LANES),
                  pl.BlockSpec((1, RWKV_CHUNKS_PER_STEP, V7X_LANES),
                               lambda bp, t: (bp % HEAD_PAIRS, (bp // HEAD_PAIRS) * steps + t, 0)),
                  pair_row, pair_row],
        out_specs=[tok(V7X_LANES),
                   pl.BlockSpec((1, V7X_LANES, V7X_LANES), lambda bp, t: (bp, 0, 0))],
        out_shape=[jax.ShapeDtypeStruct((HEAD_PAIRS, n_rows_out, V7X_LANES), BF16),
                   jax.ShapeDtypeStruct((n_batch * HEAD_PAIRS, V7X_LANES, V7X_LANES), F32)],
        scratch_shapes=[pltpu.VMEM((V7X_LANES, V7X_LANES), F32)],
        compiler_params=_params(("parallel", "arbitrary")),
        name="rwkv_chunks",
    )(at, btp, ktp, rt, v, g, bonus, wbk, pc,
      lnx_g.reshape(HEAD_PAIRS, 1, V7X_LANES), lnx_b.reshape(HEAD_PAIRS, 1, V7X_LANES))


def _softmax_rows(s):
    e = jnp.exp(s - jnp.max(s, axis=-1, keepdims=True))
    return e / jnp.sum(e, axis=-1, keepdims=True)


def _xattn_kernel(q_ref, k_ref, v_ref, o_ref):
    for h in range(XA_HEADS):
        sl = slice(h * XA_HD, (h + 1) * XA_HD)
        s = lax.dot_general(q_ref[:, sl].astype(BF16), k_ref[:, sl].astype(BF16), (((1,), (1,)), ((), ())),
                            preferred_element_type=F32) * (XA_HD ** -0.5)
        pr = _softmax_rows(s)
        o_ref[:, sl] = jnp.dot(pr.astype(BF16), v_ref[:, sl].astype(BF16),
                               preferred_element_type=F32).astype(o_ref.dtype)


def xattn_prompt(z_q, kv, n_batch, seq_len, *, tq=512):
    steps = seq_len // tq
    return pl.pallas_call(
        _xattn_kernel,
        grid=(n_batch, steps),
        in_specs=[pl.BlockSpec((tq, XA_D), lambda b, t: (b * steps + t, 0)),
                  pl.BlockSpec((N_MEM, XA_D), lambda b, t: (b, 0)),
                  pl.BlockSpec((N_MEM, XA_D), lambda b, t: (b, 1))],
        out_specs=pl.BlockSpec((tq, XA_D), lambda b, t: (b * steps + t, 0)),
        out_shape=jax.ShapeDtypeStruct((z_q.shape[0], XA_D), BF16),
        compiler_params=_params(("parallel", "parallel")),
        name="xattn_prompt",
    )(z_q, kv, kv)


def _sample_mix_kernel(zg_ref, zr_ref, shift_ref, lng_ref, lnb_ref, gw_ref, gb_ref,
                       mu_ref, w0_ref, w2_ref, a0_ref, a2_ref, g2_ref, kkp_ref, ka_ref, rk_ref, og_any,
                       og_ref, gv_ref, wt_ref, kkt_ref, bt_ref, kt_ref, rt_ref, vt_ref, bonus_ref, g_ref):
    del og_any
    ge = _gelu(zg_ref[...])
    v = _layernorm(ge[:, GMLP_D:], lng_ref[...], lnb_ref[...])
    gv_ref[...] = v
    og_ref[...] = (ge[:, :GMLP_D] * (v * gw_ref[...] + gb_ref[...])).astype(og_ref.dtype)

    z = zr_ref[...]
    zs = z + (shift_ref[...] - z) * mu_ref[...]
    mixed = _rwkv_mix(zs, w0_ref[...], w2_ref[...], a0_ref[...], a2_ref[...], g2_ref[...], kkp_ref[...],
                      ka_ref[...], rk_ref[...], _head_block_ones())
    for p, (r, k2, vv, logw, kkn, a, g, bonus) in enumerate(mixed):
        sl = slice(p * V7X_LANES, (p + 1) * V7X_LANES)
        wt_ref[sl, :] = jnp.exp(logw).T
        kkt_ref[sl, :] = kkn.T
        bt_ref[sl, :] = (kkn * a).T
        kt_ref[sl, :] = k2.T
        rt_ref[sl, :] = r.T
        vt_ref[sl, :] = vv.T
        bonus_ref[:, sl] = bonus
        g_ref[:, sl] = g


def sample_mix(z_g, z_r, row0, shift, o_g, ln_g, ln_b, gw, gb, mu, w0, w2p, a0, a2p, g2, kkp, ka, rk):
    n = shift.shape[0]
    blk = row0 // n
    tok = lambda w: pl.BlockSpec((n, w), lambda i: (blk, 0))
    loc = lambda w: pl.BlockSpec((n, w), lambda i: (0, 0))
    row = lambda w: pl.BlockSpec((1, w), lambda i: (0, 0))
    full2 = lambda a, b: pl.BlockSpec((a, b), lambda i: (0, 0))
    chan = jax.ShapeDtypeStruct((RWKV_D, n), F32)
    return pl.pallas_call(
        _sample_mix_kernel,
        grid=(1,),
        in_specs=[tok(GMLP_COLS), tok(RWKV_COLS), loc(RWKV_COLS),
                  row(GMLP_D), row(GMLP_D), row(GMLP_D), row(GMLP_D),
                  row(RWKV_COLS), row(RWKV_D), full2(LORA_W + LORA_A, RWKV_D), row(RWKV_D),
                  full2(LORA_W + LORA_A, RWKV_D), full2(LORA_G, RWKV_D), row(RWKV_D), row(RWKV_D), row(RWKV_D),
                  pl.BlockSpec(memory_space=pl.ANY)],
        out_specs=[tok(GMLP_D), loc(GMLP_D)] + [full2(RWKV_D, n)] * 6 + [loc(RWKV_D), loc(RWKV_D)],
        out_shape=[jax.ShapeDtypeStruct(o_g.shape, o_g.dtype), jax.ShapeDtypeStruct((n, GMLP_D), F32)]
                  + [chan] * 6 + [jax.ShapeDtypeStruct((n, RWKV_D), F32)] * 2,
        input_output_aliases={16: 0},
        compiler_params=_params(("arbitrary",)),
        name="sample_mix",
    )(z_g, z_r, shift, ln_g.reshape(1, -1), ln_b.reshape(1, -1), gw, gb, mu, w0, w2p, a0, a2p, g2, kkp, ka, rk, o_g)


def _sample_state_kernel(s_ref, w_ref, kk_ref, b_ref, k_ref, r_ref, v_ref, snew_ref, y_ref):
    n = s_ref.shape[0]
    st = s_ref[...].T.reshape(RWKV_HD, RWKV_HD, n)
    sa = jnp.sum(st * (-kk_ref[...])[None], axis=1, keepdims=True)
    st_new = st * w_ref[...][None] + sa * b_ref[...][None] + v_ref[...] * k_ref[...][None]
    y_ref[...] = jnp.sum(st_new * r_ref[...][None], axis=1, keepdims=True)
    snew_ref[...] = st_new.reshape(RWKV_HD * RWKV_HD, n).T


def sample_state(s0, wt, kkt, bt, kt, rt, vt):
    n = s0.shape[0]
    hd2 = RWKV_HD * RWKV_HD
    state = pl.BlockSpec((n, hd2), lambda h: (0, h))
    keyvec = pl.BlockSpec((RWKV_HD, n), lambda h: (h, 0))
    valvec = pl.BlockSpec((RWKV_HD, 1, n), lambda h: (h, 0, 0))
    return pl.pallas_call(
        _sample_state_kernel,
        grid=(RWKV_HEADS,),
        in_specs=[state, keyvec, keyvec, keyvec, keyvec, keyvec, valvec],
        out_specs=[state, valvec],
        out_shape=[jax.ShapeDtypeStruct((n, RWKV_HEADS * hd2), F32),
                   jax.ShapeDtypeStruct((RWKV_D, 1, n), F32)],
        compiler_params=_params(("parallel",)),
        name="sample_state",
    )(s0.reshape(n, RWKV_HEADS * hd2), wt, kkt, bt, kt, rt, vt.reshape(RWKV_D, 1, n))


def _sample_rwkv_out_kernel(yt_ref, bonus_ref, g_ref, lng_ref, lnb_ref, or_any, or_ref):
    del or_any
    y = yt_ref[...].T
    ones_bd = _head_block_ones()
    for p in range(HEAD_PAIRS):
        sl = slice(p * V7X_LANES, (p + 1) * V7X_LANES)
        o = _group_norm_gate(y[:, sl], bonus_ref[:, sl], g_ref[:, sl], lng_ref[p], lnb_ref[p], ones_bd)
        or_ref[p] = o.astype(or_ref.dtype)


def sample_rwkv_out(yt, bonus, g, lnx_g, lnx_b, o_r, row0):
    n = bonus.shape[0]
    full2 = lambda a, b: pl.BlockSpec((a, b), lambda i: (0, 0))
    pair_row = pl.BlockSpec((HEAD_PAIRS, 1, V7X_LANES), lambda i: (0, 0, 0))
    return pl.pallas_call(
        _sample_rwkv_out_kernel,
        grid=(1,),
        in_specs=[full2(RWKV_D, n), full2(n, RWKV_D), full2(n, RWKV_D), pair_row, pair_row,
                  pl.BlockSpec(memory_space=pl.ANY)],
        out_specs=pl.BlockSpec((HEAD_PAIRS, n, V7X_LANES), lambda i: (0, row0 // n, 0)),
        out_shape=jax.ShapeDtypeStruct(o_r.shape, o_r.dtype),
        input_output_aliases={5: 0},
        compiler_params=_params(("arbitrary",)),
        name="sample_rwkv_out",
    )(yt, bonus, g, lnx_g.reshape(HEAD_PAIRS, 1, V7X_LANES), lnx_b.reshape(HEAD_PAIRS, 1, V7X_LANES), o_r)


def _xattn_sample_kernel(q_ref, ck_ref, cv_ref, ox_any, ox_ref, *, sb):
    del ox_any
    q = q_ref[...]
    hrow = lax.broadcasted_iota(jnp.int32, (8, XA_D), 0)
    hlane = lax.broadcasted_iota(jnp.int32, (8, XA_D), 1) // XA_HD
    own = hrow == hlane
    nt = (((1,), (1,)), ((), ()))
    outs = []
    for s in range(sb):
        qm = jnp.where(own, q[s:s + 1, :], 0.0).astype(BF16)
        sc = lax.dot_general(qm, ck_ref[s].astype(BF16), nt, preferred_element_type=F32) * (XA_HD ** -0.5)
        pr = _softmax_rows(sc)
        res = jnp.dot(pr.astype(BF16), cv_ref[s].astype(BF16), preferred_element_type=F32)
        outs.append(jnp.sum(jnp.where(own, res, 0.0), axis=0, keepdims=True))
    ox_ref[...] = jnp.concatenate(outs, axis=0).astype(ox_ref.dtype)


def xattn_sample(z_q, ck, cv, o_x, row0, *, sb=8):
    n = ck.shape[0]
    off = row0 // sb
    tok = pl.BlockSpec((sb, XA_D), lambda i: (i + off, 0))
    cache = pl.BlockSpec((sb, N_MEM, XA_D), lambda i: (i, 0, 0))
    return pl.pallas_call(
        functools.partial(_xattn_sample_kernel, sb=sb),
        grid=(n // sb,),
        in_specs=[tok, cache, cache, pl.BlockSpec(memory_space=pl.ANY)],
        out_specs=tok,
        out_shape=jax.ShapeDtypeStruct(o_x.shape, o_x.dtype),
        input_output_aliases={3: 0},
        compiler_params=_params(("arbitrary",)),
        name="xattn_sample",
    )(z_q, ck, cv, o_x)


def _merge_kernel(og_ref, or_ref, ox_ref, g0_ref, g1_ref, g2_ref, wg_ref, wr_ref, wx_ref, o_ref,
                  wg_bf, wr_bf, wx_bf):
    @pl.when(pl.program_id(1) == 0)
    def _():
        wg_bf[...] = wg_ref[...].astype(BF16)
        wr_bf[...] = wr_ref[...].astype(BF16)
        wx_bf[...] = wx_ref[...].astype(BF16)

    up_g = jnp.dot(og_ref[...], wg_bf[...], preferred_element_type=F32)
    up_r = jnp.dot(or_ref[0], wr_bf[0:V7X_LANES, :], preferred_element_type=F32)
    for p in range(1, HEAD_PAIRS):
        up_r = up_r + jnp.dot(or_ref[p], wr_bf[p * V7X_LANES:(p + 1) * V7X_LANES, :], preferred_element_type=F32)
    up_x = jnp.dot(ox_ref[...], wx_bf[...], preferred_element_type=F32)
    merged = _sigmoid(g0_ref[...]) * up_g + _sigmoid(g1_ref[...]) * up_r + _sigmoid(g2_ref[...]) * up_x
    o_ref[...] = merged.astype(o_ref.dtype)


def merge(o_g, o_r, o_x, z_gate, w_up_g, w_up_r, w_up_x, *, tb, nb=512):
    m = o_g.shape[0]
    nblk = D_MODEL // nb
    gate = lambda b: pl.BlockSpec((tb, nb), lambda j, i: (i, b * nblk + j))
    wspec = lambda k: pl.BlockSpec((k, nb), lambda j, i: (0, j))
    return pl.pallas_call(
        _merge_kernel,
        grid=(nblk, m // tb),
        in_specs=[pl.BlockSpec((tb, GMLP_D), lambda j, i: (i, 0)),
                  pl.BlockSpec((HEAD_PAIRS, tb, V7X_LANES), lambda j, i: (0, i, 0)),
                  pl.BlockSpec((tb, XA_D), lambda j, i: (i, 0)),
                  gate(0), gate(1), gate(2), wspec(GMLP_D), wspec(RWKV_D), wspec(XA_D)],
        out_specs=pl.BlockSpec((tb, nb), lambda j, i: (i, j)),
        out_shape=jax.ShapeDtypeStruct((m, D_MODEL), BF16),
        scratch_shapes=[pltpu.VMEM((GMLP_D, nb), BF16), pltpu.VMEM((RWKV_D, nb), BF16),
                        pltpu.VMEM((XA_D, nb), BF16)],
        compiler_params=_params(("arbitrary", "arbitrary")),
        name="merge",
    )(o_g, o_r, o_x, z_gate, z_gate, z_gate, w_up_g, w_up_r, w_up_x)


def _extract_top(work_ref, rank_ref, vals_ref, n_rows):
    width = work_ref.shape[1]
    riota = lax.broadcasted_iota(jnp.int32, (n_rows, V7X_LANES), 0)
    kiota = lax.broadcasted_iota(jnp.int32, (TOPK, V7X_LANES), 0)
    rank_ref[...] = jnp.full(rank_ref.shape, TOPK, jnp.int32)
    vals_ref[...] = jnp.zeros_like(vals_ref)

    def body(p, carry):
        for c in range(width // V7X_LANES):
            sl = slice(c * V7X_LANES, (c + 1) * V7X_LANES)
            w = work_ref[:, sl]
            m = jnp.max(w, axis=0, keepdims=True)
            idx = jnp.min(jnp.where(w == m, riota, n_rows), axis=0, keepdims=True)
            hit = riota == idx
            rank_ref[:, sl] = jnp.where(hit, p, rank_ref[:, sl])
            work_ref[:, sl] = jnp.where(hit, NEG_INF, w)
            vals_ref[:, sl] = jnp.where(kiota == p, m, vals_ref[:, sl])
        return carry

    lax.fori_loop(0, TOPK, body, 0)


_CAND_COUNT = tuple(TOPK // (a + 1) for a in range(TOPK))
_CAND_START = tuple(sum(_CAND_COUNT[:a]) for a in range(TOPK))
_CAND_ROWS = -(-sum(_CAND_COUNT) // 8) * 8


def _peer_topk_kernel(q_ref, keys_ref, r2_ref, lim_ref, e1_ref, e2_ref,
                      s_scr, work_scr, rank_scr, vals_scr, cand_scr, crank_scr, cvals_scr, *, tbk):
    nt = (((1,), (1,)), ((), ()))
    for h in range(PEER_HEADS):
        for c in range(2):
            qcol = (2 * h + c) * PEER_DH
            col = (2 * h + c) * tbk
            sc = lax.dot_general(keys_ref[h, c], q_ref[:, qcol:qcol + PEER_DH], nt,
                                 preferred_element_type=F32, precision=HIGHEST)
            s_scr[:, col:col + tbk] = sc
            work_scr[:, col:col + tbk] = sc
    _extract_top(work_scr, rank_scr, vals_scr, N_KEYS)

    crow = lax.broadcasted_iota(jnp.int32, (_CAND_ROWS, tbk), 0)
    seg = jnp.full((_CAND_ROWS, tbk), TOPK, jnp.int32)
    for a in reversed(range(TOPK)):
        seg = jnp.where(crow < _CAND_START[a] + _CAND_COUNT[a], jnp.minimum(seg, a), seg)
    pad = jnp.zeros((_CAND_ROWS - TOPK, tbk), F32)
    for h in range(PEER_HEADS):
        v1 = vals_scr[:, (2 * h) * tbk:(2 * h + 1) * tbk]
        v2 = jnp.concatenate([vals_scr[:, (2 * h + 1) * tbk:(2 * h + 2) * tbk], pad], axis=0)
        cand = jnp.full((_CAND_ROWS, tbk), NEG_INF, F32)
        for a in range(TOPK):
            shifted = v2 if _CAND_START[a] == 0 else pltpu.roll(v2, _CAND_START[a], axis=0)
            cand = jnp.where(seg == a, v1[a:a + 1, :] + shifted, cand)
        cand_scr[:, h * tbk:(h + 1) * tbk] = cand
    _extract_top(cand_scr, crank_scr, cvals_scr, _CAND_ROWS)

    for h in range(PEER_HEADS):
        hs = slice(h * tbk, (h + 1) * tbk)
        s1 = slice((2 * h) * tbk, (2 * h + 1) * tbk)
        s2 = slice((2 * h + 1) * tbk, (2 * h + 2) * tbk)
        cvals = cvals_scr[:, hs]
        z = jnp.sum(jnp.exp(cvals - cvals[0:1, :]), axis=0, keepdims=True)
        chosen = crank_scr[:, hs] < TOPK
        rank1 = rank_scr[:, s1]
        lim = jnp.zeros((N_KEYS, tbk), F32)
        for a in range(TOPK):
            count = jnp.sum(jnp.where(jnp.logical_and(chosen, seg == a), 1.0, 0.0), axis=0, keepdims=True)
            lim = jnp.where(rank1 == a, count, lim)
        lim_ref[h] = lim
        r2_ref[h] = rank_scr[:, s2].astype(F32).astype(r2_ref.dtype)
        e1_ref[h] = jnp.exp(s_scr[:, s1] - vals_scr[0:1, s1]) / z
        e2_ref[h] = jnp.exp(s_scr[:, s2] - vals_scr[0:1, s2]).astype(e2_ref.dtype)


def peer_topk(q, keys, *, tbk=128):
    m = q.shape[0]
    out = pl.BlockSpec((PEER_HEADS, N_KEYS, tbk), lambda i: (0, 0, i))
    shape = lambda dt: jax.ShapeDtypeStruct((PEER_HEADS, N_KEYS, m), dt)
    wide = 2 * PEER_HEADS * tbk
    return pl.pallas_call(
        functools.partial(_peer_topk_kernel, tbk=tbk),
        grid=(m // tbk,),
        in_specs=[pl.BlockSpec((tbk, PEER_HEADS * 2 * PEER_DH), lambda i: (i, 0)),
                  pl.BlockSpec((PEER_HEADS, 2, N_KEYS, PEER_DH), lambda i: (0, 0, 0, 0))],
        out_specs=[out] * 4,
        out_shape=[shape(BF16), shape(F32), shape(F32), shape(BF16)],
        scratch_shapes=[pltpu.VMEM((N_KEYS, wide), F32), pltpu.VMEM((N_KEYS, wide), F32),
                        pltpu.VMEM((N_KEYS, wide), jnp.int32), pltpu.VMEM((TOPK, wide), F32),
                        pltpu.VMEM((_CAND_ROWS, PEER_HEADS * tbk), F32),
                        pltpu.VMEM((_CAND_ROWS, PEER_HEADS * tbk), jnp.int32),
                        pltpu.VMEM((TOPK, PEER_HEADS * tbk), F32)],
        compiler_params=_params(("parallel",)),
        name="peer_topk",
    )(q, keys)


PEER_SUB = 2 * N_KEYS


def _peer_dense_kernel(xt_ref, r2_ref, lim_ref, e1_ref, e2_ref, u_ref, v_ref, o_ref, coef_a, coef_b,
                       *, eb, n_blocks):
    e = pl.program_id(1)

    @pl.when(e == 0)
    def _():
        o_ref[...] = jnp.zeros_like(o_ref)
        coef_b[...] = jnp.zeros_like(coef_b)

    blk = jnp.minimum(e, n_blocks - 1)

    n_sub = eb // PEER_SUB
    d_sub = o_ref.shape[1] // n_sub

    def hidden(s):
        return jnp.dot(u_ref[s * PEER_SUB:(s + 1) * PEER_SUB, :], xt_ref[...],
                       preferred_element_type=F32).astype(BF16)

    def step(prev_ref, next_ref):
        ht_next = hidden(0)
        for s in range(n_sub):
            ht = ht_next
            cols = slice(s * d_sub, (s + 1) * d_sub)
            o_ref[:, cols] += lax.dot_general(prev_ref[...], v_ref[:, cols], (((0,), (0,)), ((), ())),
                                              preferred_element_type=F32)
            if s + 1 < n_sub:
                ht_next = hidden(s + 1)
            for ii in range(PEER_SUB // N_KEYS):
                i = blk * (eb // N_KEYS) + s * (PEER_SUB // N_KEYS) + ii
                gate = None
                for h in range(PEER_HEADS):
                    lim = lim_ref[h, pl.ds(i, 1), :].astype(BF16)
                    e1 = e1_ref[h, pl.ds(i, 1), :].astype(BF16)
                    term = jnp.where(r2_ref[h] < lim, e2_ref[h] * e1, jnp.zeros((), BF16))
                    gate = term if gate is None else gate + term
                row = s * PEER_SUB + ii * N_KEYS
                next_ref[row:row + N_KEYS, :] = gate * _gelu(ht[ii * N_KEYS:(ii + 1) * N_KEYS, :])

    @pl.when(e % 2 == 0)
    def _():
        step(coef_b, coef_a)

    @pl.when(e % 2 == 1)
    def _():
        step(coef_a, coef_b)


def peer_dense(xt, r2, lim, e1, e2, u_bf, v_bf, *, tbl, eb=1024):
    d, m = xt.shape
    n_blocks = u_bf.shape[0] // eb
    head = pl.BlockSpec((PEER_HEADS, N_KEYS, tbl), lambda t, e: (0, 0, t))
    return pl.pallas_call(
        functools.partial(_peer_dense_kernel, eb=eb, n_blocks=n_blocks),
        grid=(m // tbl, n_blocks + 1),
        in_specs=[pl.BlockSpec((d, tbl), lambda t, e: (0, t)), head, head, head, head,
                  pl.BlockSpec((eb, d), lambda t, e: (jnp.minimum(e, n_blocks - 1), 0)),
                  pl.BlockSpec((eb, d), lambda t, e: (jnp.maximum(e - 1, 0), 0))],
        out_specs=pl.BlockSpec((tbl, d), lambda t, e: (t, 0)),
        out_shape=jax.ShapeDtypeStruct((m, d), F32),
        scratch_shapes=[pltpu.VMEM((eb, tbl), BF16), pltpu.VMEM((eb, tbl), BF16)],
        compiler_params=_params(("parallel", "arbitrary")),
        name="peer_dense",
    )(xt, r2, lim, e1, e2, u_bf, v_bf)


def _final_kernel(h_ref, p_ref, g_ref, o_ref):
    x = h_ref[...] + p_ref[...]
    o_ref[...] = x * lax.rsqrt(jnp.mean(x * x, axis=-1, keepdims=True) + RMS_EPS) * g_ref[...]


def final_norm(h, peer, g, row0, n_rows, *, tb):
    d = h.shape[1]
    off = row0 // tb
    rows = pl.BlockSpec((tb, d), lambda i: (i + off, 0))
    return pl.pallas_call(
        _final_kernel,
        grid=(n_rows // tb,),
        in_specs=[rows, rows, pl.BlockSpec((1, d), lambda i: (0, 0))],
        out_specs=pl.BlockSpec((tb, d), lambda i: (i, 0)),
        out_shape=jax.ShapeDtypeStruct((n_rows, d), F32),
        compiler_params=_params(("parallel",)),
        name="final_norm",
    )(h, peer, g.reshape(1, d))


def kernel(x_prompt, x_sample, mem_prompt, state_shift, state_wkv, cache_mem_k, cache_mem_v, ln1_g, w_in, gmlp_ln_g, gmlp_ln_b, gmlp_ws, gmlp_bs, rwkv_mu, rwkv_w0, rwkv_w2, rwkv_a0, rwkv_a2, rwkv_g2, rwkv_kk, rwkv_ka, rwkv_rk, rwkv_lnx_g, rwkv_lnx_b, mem_norm_g, w_mem_kv, w_up_g, w_up_r, w_up_x, w_out, ln2_g, peer_wq, peer_keys, peer_u, peer_v, final_g):
    depth = w_in.shape[0]
    assert depth == 1, "single-layer step"
    l = 0
    n_batch, seq_len, d = x_prompt.shape
    n_dec = x_sample.shape[0]
    n_prompt = n_batch * seq_len
    m = n_prompt + n_dec
    tb_small = _largest_divisor(m, (640, 128))
    tb_big = _largest_divisor(m, (1664, 640, 128))

    x = jnp.concatenate([x_prompt.reshape(n_prompt, d), x_sample.reshape(n_dec, d)], axis=0)
    xn = rmsnorm(x, ln1_g[l], tb=tb_small, out_dtype=BF16)
    proj = functools.partial(matmul, xn, w_in[l], tb=tb_big, nb=512)
    z_g = proj(col_off=0, n_cols=GMLP_COLS, name="proj_gmlp")
    z_r = proj(col_off=GMLP_COLS, n_cols=RWKV_COLS, name="proj_rwkv")
    z_q = proj(col_off=GMLP_COLS + RWKV_COLS, n_cols=XA_D, name="proj_xattn")
    z_gate = proj(col_off=GMLP_COLS + RWKV_COLS + XA_D, n_cols=GATE_COLS, name="proj_gate")

    memn = rmsnorm(mem_prompt.reshape(n_batch * N_MEM, d), mem_norm_g[l], tb=256, out_dtype=BF16)
    kv = matmul(memn, w_mem_kv[l], tb=N_MEM, nb=512, name="proj_mem_kv")
    p_mk = kv[:, :XA_D].reshape(1, n_batch, N_MEM, XA_HEADS, XA_HD)
    p_mv = kv[:, XA_D:].reshape(1, n_batch, N_MEM, XA_HEADS, XA_HD)

    row = lambda a: a.reshape(1, -1)
    zeros_lora = jnp.zeros((LORA_W, RWKV_D), F32)
    w2p = jnp.concatenate([rwkv_w2[l], zeros_lora], axis=0).astype(BF16)
    a2p = jnp.concatenate([zeros_lora, rwkv_a2[l]], axis=0).astype(BF16)
    rw = (row(rwkv_mu[l]), row(rwkv_w0[l]), w2p, row(rwkv_a0[l]), a2p, rwkv_g2[l].astype(BF16),
          row(rwkv_kk[l]), row(rwkv_ka[l]), row(rwkv_rk[l]))

    o_g = gmlp_prompt(z_g, n_prompt, gmlp_ln_g[l], gmlp_ln_b[l], gmlp_ws[l], gmlp_bs[l])
    prep = rwkv_prep(z_r, n_prompt, seq_len, *rw)
    o_r, st = rwkv_chunks(prep, n_batch, seq_len, m, rwkv_lnx_g[l], rwkv_lnx_b[l])
    o_x = xattn_prompt(z_q, kv, n_batch, seq_len)

    gw = jnp.repeat(gmlp_ws[l][:, 0, 0], CHUNK).reshape(1, GMLP_D)
    gb = jnp.repeat(gmlp_bs[l][:, 0], CHUNK).reshape(1, GMLP_D)
    o_g, s_gv, wt, kkt, bt, kt, rt, vt, bonus_s, g_s = sample_mix(
        z_g, z_r, n_prompt, state_shift[l], o_g, gmlp_ln_g[l], gmlp_ln_b[l], gw, gb, *rw)
    s_wkv, y_s = sample_state(state_wkv[l], wt, kkt, bt, kt, rt, vt)
    o_r = sample_rwkv_out(y_s.reshape(RWKV_D, n_dec), bonus_s, g_s, rwkv_lnx_g[l], rwkv_lnx_b[l], o_r, n_prompt)
    o_x = xattn_sample(z_q, cache_mem_k[l].reshape(n_dec, N_MEM, XA_D), cache_mem_v[l].reshape(n_dec, N_MEM, XA_D),
                       o_x, n_prompt)

    merged = merge(o_g, o_r, o_x, z_gate, w_up_g[l], w_up_r[l], w_up_x[l], tb=tb_small)
    h = matmul(merged, w_out[l], tb=tb_big, nb=512, residual=x, name="proj_out")

    hn, hn_t = rmsnorm(h, ln2_g[l], tb=tb_small, out_dtype=BF16, transposed=True)
    q = matmul(hn, peer_wq[l], tb=tb_big, nb=512, name="proj_peer_q")
    r2, lim, e1, e2 = peer_topk(q, peer_keys[l])
    peer = peer_dense(hn_t, r2, lim, e1, e2, peer_u[l].astype(BF16), peer_v[l].astype(BF16), tbl=tb_small)
    y_prompt = final_norm(h, peer, final_g, 0, n_prompt, tb=512)
    y_sample = final_norm(h, peer, final_g, n_prompt, n_dec, tb=n_dec)

    st = st.reshape(n_batch, HEAD_PAIRS, 2, RWKV_HD, 2, RWKV_HD)
    p_wkv = jnp.stack([st[:, :, 0, :, 0, :], st[:, :, 1, :, 1, :]], axis=2)
    p_wkv = jnp.swapaxes(p_wkv, -1, -2).reshape(1, n_batch, RWKV_HEADS, RWKV_HD, RWKV_HD)
    p_shift = z_r[seq_len - 1:n_prompt:seq_len][None]
    s_shift = z_r[n_prompt:][None]
    s_wkv = s_wkv.reshape(1, n_dec, RWKV_HEADS, RWKV_HD, RWKV_HD)
    return (y_prompt.reshape(n_batch, seq_len, d), y_sample.reshape(n_dec, 1, d), p_mk, p_mv,
            p_shift, p_wkv, s_shift, s_wkv, s_gv.reshape(1, n_dec, 1, GMLP_D))
```

```python
import functools

import jax
import jax.numpy as jnp
from jax import lax
from jax.experimental import pallas as pl
from jax.experimental.pallas import tpu as pltpu

F32 = jnp.float32
BF16 = jnp.bfloat16
HIGHEST = lax.Precision.HIGHEST

D_MODEL = 2048
CHUNK = 128
GMLP_GROUPS = 6
GMLP_D = 768
RWKV_HEADS = 12
RWKV_HD = 64
RWKV_D = 768
LORA_W = 64
LORA_A = 64
LORA_G = 128
RWKV_COLS = 2560
XA_HEADS = 4
XA_HD = 128
XA_D = 512
N_MEM = 256
GMLP_COLS = 2 * GMLP_D
GATE_COLS = 3 * D_MODEL
PEER_HEADS = 8
N_KEYS = 128
PEER_DH = 128
TOPK = 16
RMS_EPS = 1e-6
LN_EPS = 1e-5
GN_EPS = 64e-5

V7X_LANES = 128
V7X_VMEM_LIMIT_BYTES = 56 * 1024 * 1024

HEAD_PAIRS = RWKV_HEADS // 2
RWKV_CHUNK = 64
RWKV_CHUNKS_PER_STEP = 16
NEG_INF = float("-inf")


def _params(semantics):
    return pltpu.CompilerParams(dimension_semantics=semantics, vmem_limit_bytes=V7X_VMEM_LIMIT_BYTES)


def _largest_divisor(m, candidates):
    return next(c for c in candidates if m % c == 0)


def _gelu(x):
    return 0.5 * x * (1.0 + jnp.tanh(0.7978845608028654 * (x + 0.044715 * (x * x * x))))


def _gelu_to_bf16(x):
    inner = (0.7978845608028654 * x) * (1.0 + 0.044715 * (x * x))
    xb = x.astype(BF16)
    return (0.5 * xb) * (1.0 + jnp.tanh(inner.astype(BF16)))


def _sigmoid(x):
    return 1.0 / (1.0 + jnp.exp(-x))


def _rmsnorm_kernel(x_ref, g_ref, *o_refs, transposed):
    x = x_ref[...]
    y = x * lax.rsqrt(jnp.mean(x * x, axis=-1, keepdims=True) + RMS_EPS) * g_ref[...]
    o_refs[0][...] = y.astype(o_refs[0].dtype)
    if transposed:
        o_refs[1][...] = y.T.astype(o_refs[1].dtype)


def rmsnorm(x, g, *, tb, out_dtype, transposed=False):
    m, d = x.shape
    out_shape = [jax.ShapeDtypeStruct((m, d), out_dtype)]
    out_specs = [pl.BlockSpec((tb, d), lambda i: (i, 0))]
    if transposed:
        out_shape.append(jax.ShapeDtypeStruct((d, m), out_dtype))
        out_specs.append(pl.BlockSpec((d, tb), lambda i: (0, i)))
    res = pl.pallas_call(
        functools.partial(_rmsnorm_kernel, transposed=transposed),
        grid=(m // tb,),
        in_specs=[pl.BlockSpec((tb, d), lambda i: (i, 0)), pl.BlockSpec((1, d), lambda i: (0, 0))],
        out_specs=out_specs,
        out_shape=out_shape,
        compiler_params=_params(("parallel",)),
        name="rmsnorm_t" if transposed else "rmsnorm",
    )(x, g.reshape(1, d))
    return res if transposed else res[0]


def _matmul_kernel(a_ref, w_ref, *rest, has_residual):
    if has_residual:
        r_ref, o_ref, wbf_ref = rest
    else:
        o_ref, wbf_ref = rest

    @pl.when(pl.program_id(1) == 0)
    def _():
        wbf_ref[...] = w_ref[...].astype(BF16)

    acc = jnp.dot(a_ref[...], wbf_ref[...], preferred_element_type=F32)
    if has_residual:
        acc = acc + r_ref[...]
    o_ref[...] = acc.astype(o_ref.dtype)


def matmul(a, w, *, tb, nb, col_off=0, n_cols=None, residual=None, out_dtype=F32, name="matmul"):
    m, k = a.shape
    n = w.shape[1] if n_cols is None else n_cols
    off = col_off // nb
    in_specs = [pl.BlockSpec((tb, k), lambda j, i: (i, 0)),
                pl.BlockSpec((k, nb), lambda j, i: (0, j + off))]
    args = [a, w]
    if residual is not None:
        in_specs.append(pl.BlockSpec((tb, nb), lambda j, i: (i, j)))
        args.append(residual)
    return pl.pallas_call(
        functools.partial(_matmul_kernel, has_residual=residual is not None),
        grid=(n // nb, m // tb),
        in_specs=in_specs,
        out_specs=pl.BlockSpec((tb, nb), lambda j, i: (i, j)),
        out_shape=jax.ShapeDtypeStruct((m, n), out_dtype),
        scratch_shapes=[pltpu.VMEM((k, nb), BF16)],
        compiler_params=_params(("arbitrary", "arbitrary")),
        name=name,
    )(*args)


def _layernorm(v, g, b):
    vc = v - jnp.mean(v, axis=-1, keepdims=True)
    var = jnp.mean(vc * vc, axis=-1, keepdims=True)
    return vc * lax.rsqrt(var + LN_EPS) * g + b


def _gmlp_kernel(z_ref, lng_ref, lnb_ref, ws_ref, bst_ref, o_ref):
    ge = _gelu(z_ref[...])
    u = ge[:, :GMLP_D]
    v = _layernorm(ge[:, GMLP_D:], lng_ref[...], lnb_ref[...])
    row = lax.broadcasted_iota(jnp.int32, (CHUNK, CHUNK), 0)
    col = lax.broadcasted_iota(jnp.int32, (CHUNK, CHUNK), 1)
    causal = col <= row
    for g in range(GMLP_GROUPS):
        sl = slice(g * CHUNK, (g + 1) * CHUNK)
        wm = jnp.where(causal, ws_ref[g], 0.0).astype(BF16)
        mixed = jnp.dot(wm, v[:, sl].astype(BF16), preferred_element_type=F32) + bst_ref[:, g:g + 1]
        o_ref[:, sl] = (u[:, sl] * mixed).astype(o_ref.dtype)


def gmlp_prompt(z_g, n_tokens, ln_g, ln_b, ws, bs):
    return pl.pallas_call(
        _gmlp_kernel,
        grid=(n_tokens // CHUNK,),
        in_specs=[pl.BlockSpec((CHUNK, GMLP_COLS), lambda i: (i, 0)),
                  pl.BlockSpec((1, GMLP_D), lambda i: (0, 0)),
                  pl.BlockSpec((1, GMLP_D), lambda i: (0, 0)),
                  pl.BlockSpec((GMLP_GROUPS, CHUNK, CHUNK), lambda i: (0, 0, 0)),
                  pl.BlockSpec((CHUNK, GMLP_GROUPS), lambda i: (0, 0))],
        out_specs=pl.BlockSpec((CHUNK, GMLP_D), lambda i: (i, 0)),
        out_shape=jax.ShapeDtypeStruct((z_g.shape[0], GMLP_D), BF16),
        compiler_params=_params(("parallel",)),
        name="gmlp_prompt",
    )(z_g, ln_g.reshape(1, GMLP_D), ln_b.reshape(1, GMLP_D), ws, bs.T)


def _head_block_ones():
    r = lax.broadcasted_iota(jnp.int32, (V7X_LANES, V7X_LANES), 0) // RWKV_HD
    c = lax.broadcasted_iota(jnp.int32, (V7X_LANES, V7X_LANES), 1) // RWKV_HD
    return (r == c).astype(F32)


def _head_sum(x, ones_bd):
    return jnp.dot(x, ones_bd, preferred_element_type=F32, precision=HIGHEST)


def _rwkv_mix(zs, w0, w2p, a0, a2p, g2, kkp, ka, rk, ones_bd):
    r = zs[:, 0:RWKV_D]
    k = zs[:, RWKV_D:2 * RWKV_D]
    v = zs[:, 2 * RWKV_D:3 * RWKV_D]
    lwa = zs[:, 3 * RWKV_D:3 * RWKV_D + LORA_W + LORA_A]
    lg = zs[:, 3 * RWKV_D + LORA_W + LORA_A:]
    lora_w = jnp.dot(jnp.tanh(lwa).astype(BF16), w2p, preferred_element_type=F32)
    lora_a = jnp.dot(lwa.astype(BF16), a2p, preferred_element_type=F32)
    x = -(w0 + lora_w)
    softplus = jnp.maximum(x, 0.0) + jnp.log(1.0 + jnp.exp(-jnp.abs(x)))
    logw = -jnp.exp(-softplus - 0.5)
    a = _sigmoid(a0 + lora_a)
    g = jnp.dot(_sigmoid(lg).astype(BF16), g2, preferred_element_type=F32)
    kk = k * kkp
    k2 = k * (1.0 + (a - 1.0) * ka)
    rkk = r * k2 * rk
    out = []
    for p in range(HEAD_PAIRS):
        sl = slice(p * V7X_LANES, (p + 1) * V7X_LANES)
        kk_p = kk[:, sl]
        norm = jnp.sqrt(_head_sum(kk_p * kk_p, ones_bd))
        kkn = kk_p / jnp.maximum(norm, 1e-12)
        bonus = _head_sum(rkk[:, sl], ones_bd) * v[:, sl]
        out.append((r[:, sl], k2[:, sl], v[:, sl], logw[:, sl], kkn, a[:, sl], g[:, sl], bonus))
    return out


def _rwkv_prep_kernel(z_ref, zp_ref, mu_ref, w0_ref, w2_ref, a0_ref, a2_ref, g2_ref, kkp_ref, ka_ref, rk_ref,
                      at_ref, btp_ref, ktp_ref, rt_ref, v_ref, g_ref, bonus_ref, w_ref, pc_ref, cs_scr,
                      *, tb, blocks_per_seq):
    i = pl.program_id(0)
    z = z_ref[...]
    first = (i % blocks_per_seq) == 0
    prev_row = jnp.where(first, 0.0, zp_ref[7:8, :])
    row = lax.broadcasted_iota(jnp.int32, (tb, 1), 0)
    zprev = jnp.where(row == 0, prev_row, pltpu.roll(z, 1, axis=0))
    zs = z + (zprev - z) * mu_ref[...]
    ones_bd = _head_block_ones()
    mixed = _rwkv_mix(zs, w0_ref[...], w2_ref[...], a0_ref[...], a2_ref[...], g2_ref[...], kkp_ref[...],
                      ka_ref[...], rk_ref[...], ones_bd)
    rowc = row % RWKV_CHUNK
    n_chunks = tb // RWKV_CHUNK
    for p, (r, k2, v, logw, kkn, a, g, bonus) in enumerate(mixed):
        cs = logw
        shift = 1
        while shift < RWKV_CHUNK:
            cs = cs + jnp.where(rowc >= shift, pltpu.roll(cs, shift, axis=0), 0.0)
            shift *= 2
        cs_scr[...] = cs
        cs_end = cs_scr[pl.ds(RWKV_CHUNK - 1, n_chunks, stride=RWKV_CHUNK), :]
        rs = logw
        shift = 1
        while shift < RWKV_CHUNK:
            rs = rs + jnp.where(rowc + shift < RWKV_CHUNK, pltpu.roll(rs, tb - shift, axis=0), 0.0)
            shift *= 2
        tail = jnp.exp(rs - logw)
        inv_p = jnp.exp(-cs)
        at_ref[p] = jnp.exp(cs - logw) * kkn
        btp_ref[p] = kkn * a * tail
        ktp_ref[p] = k2 * tail
        rt_ref[p] = jnp.exp(cs) * r
        v_ref[p] = v
        g_ref[p] = g
        bonus_ref[p] = bonus
        w_ref[p] = jnp.concatenate([kkn * a * inv_p, k2 * inv_p], axis=-1)
        pc_ref[p] = jnp.exp(cs_end)


def rwkv_prep(z_r, n_tokens, seq_len, mu, w0, w2p, a0, a2p, g2, kkp, ka, rk, *, tb=512):
    n_chunks = n_tokens // RWKV_CHUNK
    row_spec = lambda w: pl.BlockSpec((1, w), lambda i: (0, 0))
    pair_out = lambda w: pl.BlockSpec((HEAD_PAIRS, tb, w), lambda i: (0, i, 0))
    pair_shape = lambda w: jax.ShapeDtypeStruct((HEAD_PAIRS, n_tokens, w), F32)
    return pl.pallas_call(
        functools.partial(_rwkv_prep_kernel, tb=tb, blocks_per_seq=seq_len // tb),
        grid=(n_tokens // tb,),
        in_specs=[pl.BlockSpec((tb, RWKV_COLS), lambda i: (i, 0)),
                  pl.BlockSpec((8, RWKV_COLS), lambda i: (jnp.maximum(i * (tb // 8) - 1, 0), 0)),
                  row_spec(RWKV_COLS), row_spec(RWKV_D),
                  pl.BlockSpec((LORA_W + LORA_A, RWKV_D), lambda i: (0, 0)),
                  row_spec(RWKV_D),
                  pl.BlockSpec((LORA_W + LORA_A, RWKV_D), lambda i: (0, 0)),
                  pl.BlockSpec((LORA_G, RWKV_D), lambda i: (0, 0)),
                  row_spec(RWKV_D), row_spec(RWKV_D), row_spec(RWKV_D)],
        out_specs=[pair_out(V7X_LANES)] * 7 + [pair_out(2 * V7X_LANES),
                   pl.BlockSpec((HEAD_PAIRS, tb // RWKV_CHUNK, V7X_LANES), lambda i: (0, i, 0))],
        out_shape=[pair_shape(V7X_LANES)] * 7 + [pair_shape(2 * V7X_LANES),
                   jax.ShapeDtypeStruct((HEAD_PAIRS, n_chunks, V7X_LANES), F32)],
        scratch_shapes=[pltpu.VMEM((tb, V7X_LANES), F32)],
        compiler_params=_params(("parallel",)),
        name="rwkv_prep",
    )(z_r, z_r, mu, w0, w2p, a0, a2p, g2, kkp, ka, rk)


def _bmm(x, y):
    return lax.dot_general(x.astype(BF16), y.astype(BF16), (((2,), (1,)), ((0,), (0,))),
                           preferred_element_type=F32)


def _bmm_nt(x, y):
    return lax.dot_general(x.astype(BF16), y.astype(BF16), (((2,), (2,)), ((0,), (0,))),
                           preferred_element_type=F32)


def _mm_tn(x, y, precision=None):
    return lax.dot_general(x, y, (((0,), (0,)), ((), ())), preferred_element_type=F32, precision=precision)


def _group_norm_gate(y, bonus, g, lng, lnb, ones_bd):
    mean = _head_sum(y, ones_bd) * (1.0 / RWKV_HD)
    yc = y - mean
    var = _head_sum(yc * yc, ones_bd) * (1.0 / RWKV_HD)
    return (yc * lax.rsqrt(var + GN_EPS) * lng + lnb + bonus) * g


def _rwkv_chunk_kernel(at_ref, btp_ref, ktp_ref, rt_ref, v_ref, g_ref, bonus_ref, w_ref, pc_ref, lng_ref,
                       lnb_ref, o_ref, st_ref, st_scr):
    nc, c, ln = RWKV_CHUNKS_PER_STEP, RWKV_CHUNK, V7X_LANES
    t = pl.program_id(1)

    @pl.when(t == 0)
    def _():
        st_scr[...] = jnp.zeros_like(st_scr)

    shape3 = (nc, c, ln)
    at = at_ref[0].reshape(shape3)
    rt = rt_ref[0].reshape(shape3)
    v = v_ref[0].reshape(shape3)
    btk = w_ref[0].reshape(nc, c, 2 * ln)
    bt, kt = btk[..., :ln], btk[..., ln:]
    lane = lax.broadcasted_iota(jnp.int32, (1, 1, ln), 2)
    head0 = lane < RWKV_HD
    head0_2 = jnp.concatenate([head0, head0], axis=-1)
    zero = jnp.zeros(shape3, F32)

    lhs = jnp.concatenate([jnp.where(head0, at, 0.0), jnp.where(head0, rt, 0.0),
                           jnp.where(head0, 0.0, at), jnp.where(head0, 0.0, rt)], axis=1)
    gram = _bmm_nt(lhs, jnp.concatenate([bt, kt], axis=1))
    gr = lax.broadcasted_iota(jnp.int32, (1, 4 * c, 2 * c), 1)
    gc = lax.broadcasted_iota(jnp.int32, (1, 4 * c, 2 * c), 2) % c
    causal = gr % c + (gr // c) % 2 > gc
    gram = jnp.where(causal, gram, 0.0)
    zv = jnp.concatenate([zero, v], axis=1)
    first = lax.broadcasted_iota(jnp.int32, (1, 1, 2 * c), 2) < c

    xs, mvs, mwus = [], [], []
    for h in range(2):
        g_h = gram[:, 2 * h * c:(2 * h + 2) * c, :]
        lmv = _bmm(g_h, zv)
        x = jnp.concatenate([at, lmv[:, :c, :]], axis=-1)
        lp = jnp.where(first, -g_h[:, :c, :], 0.0)
        pad2 = jnp.zeros((nc, c, 2 * ln), F32)
        x = x + _bmm(lp, jnp.concatenate([x, pad2], axis=1))
        n = 2
        while n < c:
            lp = _bmm(lp, jnp.concatenate([lp, zero], axis=1))
            x = x + _bmm(lp, jnp.concatenate([x, pad2], axis=1))
            n *= 2
        xs.append(x)
        mvs.append(lmv[:, c:, :])
        mrb = jnp.where(first, g_h[:, c:, :], 0.0)
        mwus.append(_bmm(mrb, jnp.concatenate([x, pad2], axis=1)))
    x = jnp.where(head0_2, xs[0], xs[1])
    mv = jnp.where(head0, mvs[0], mvs[1])
    mwu = jnp.where(head0_2, mwus[0], mwus[1])
    qe = (rt - mwu[..., :ln]).astype(BF16)
    yl = mv - mwu[..., ln:]

    rr = lax.broadcasted_iota(jnp.int32, (ln, ln), 0)
    cc = lax.broadcasted_iota(jnp.int32, (ln, ln), 1)
    same_head = (rr // RWKV_HD) == (cc // RWKV_HD)
    eye = rr == cc
    btp = btp_ref[0].reshape(shape3).astype(BF16)
    ktp = ktp_ref[0].reshape(shape3).astype(BF16)
    x_bf = x.astype(BF16)
    v_bf = v.astype(BF16)
    pc = pc_ref[0]
    st = st_scr[...]
    ys = []
    for j in range(nc):
        bwu = _mm_tn(btp[j], x_bf[j])
        kv = _mm_tn(ktp[j], v_bf[j])
        tr = jnp.where(same_head, jnp.where(eye, pc[j:j + 1, :], 0.0) - bwu[:, :ln], 0.0)
        ad = jnp.where(same_head, kv - bwu[:, ln:], 0.0)
        st_hi = st.astype(BF16)
        st_lo = (st - st_hi.astype(F32)).astype(BF16)
        ys.append(jnp.dot(qe[j], st_hi, preferred_element_type=F32) + yl[j])
        tr_bf = tr.astype(BF16)
        st = (jnp.dot(tr_bf, st_hi, preferred_element_type=F32)
              + jnp.dot(tr_bf, st_lo, preferred_element_type=F32) + ad)
    st_scr[...] = st
    st_ref[0] = st
    y = jnp.concatenate(ys, axis=0)
    o = _group_norm_gate(y, bonus_ref[0], g_ref[0], lng_ref[0], lnb_ref[0], _head_block_ones())
    o_ref[0] = o.astype(o_ref.dtype)


def rwkv_chunks(prep, n_batch, seq_len, n_rows_out, lnx_g, lnx_b):
    at, btp, ktp, rt, v, g, bonus, wbk, pc = prep
    rows = RWKV_CHUNK * RWKV_CHUNKS_PER_STEP
    steps = seq_len // rows
    n_tokens = n_batch * seq_len
    tok = lambda w: pl.BlockSpec((1, rows, w), lambda bp, t: (bp % HEAD_PAIRS, (bp // HEAD_PAIRS) * steps + t, 0))
    pair_row = pl.BlockSpec((1, 1, V7X_LANES), lambda bp, t: (bp % HEAD_PAIRS, 0, 0))
    return pl.pallas_call(
        _rwkv_chunk_kernel,
        grid=(n_batch * HEAD_PAIRS, steps),
        in_specs=[tok(V7X_LANES)] * 7 + [tok(2 * V7X_LANES),
                  pl.BlockSpec((1, RWKV_CHUNKS_PER_STEP, V7X_LANES),
                               lambda bp, t: (bp % HEAD_PAIRS, (bp // HEAD_PAIRS) * steps + t, 0)),
                  pair_row, pair_row],
        out_specs=[tok(V7X_LANES),
                   pl.BlockSpec((1, V7X_LANES, V7X_LANES), lambda bp, t: (bp, 0, 0))],
        out_shape=[jax.ShapeDtypeStruct((HEAD_PAIRS, n_rows_out, V7X_LANES), BF16),
                   jax.ShapeDtypeStruct((n_batch * HEAD_PAIRS, V7X_LANES, V7X_LANES), F32)],
        scratch_shapes=[pltpu.VMEM((V7X_LANES, V7X_LANES), F32)],
        compiler_params=_params(("parallel", "arbitrary")),
        name="rwkv_chunks",
    )(at, btp, ktp, rt, v, g, bonus, wbk, pc,
      lnx_g.reshape(HEAD_PAIRS, 1, V7X_LANES), lnx_b.reshape(HEAD_PAIRS, 1, V7X_LANES))


def _softmax_rows(s):
    e = jnp.exp(s - jnp.max(s, axis=-1, keepdims=True))
    return e / jnp.sum(e, axis=-1, keepdims=True)


def _xattn_kernel(q_ref, k_ref, v_ref, o_ref):
    for h in range(XA_HEADS):
        sl = slice(h * XA_HD, (h + 1) * XA_HD)
        s = lax.dot_general(q_ref[:, sl].astype(BF16), k_ref[:, sl].astype(BF16), (((1,), (1,)), ((), ())),
                            preferred_element_type=F32) * (XA_HD ** -0.5)
        pr = _softmax_rows(s)
        o_ref[:, sl] = jnp.dot(pr.astype(BF16), v_ref[:, sl].astype(BF16),
                               preferred_element_type=F32).astype(o_ref.dtype)


def xattn_prompt(z_q, kv, n_batch, seq_len, *, tq=512):
    steps = seq_len // tq
    return pl.pallas_call(
        _xattn_kernel,
        grid=(n_batch, steps),
        in_specs=[pl.BlockSpec((tq, XA_D), lambda b, t: (b * steps + t, 0)),
                  pl.BlockSpec((N_MEM, XA_D), lambda b, t: (b, 0)),
                  pl.BlockSpec((N_MEM, XA_D), lambda b, t: (b, 1))],
        out_specs=pl.BlockSpec((tq, XA_D), lambda b, t: (b * steps + t, 0)),
        out_shape=jax.ShapeDtypeStruct((z_q.shape[0], XA_D), BF16),
        compiler_params=_params(("parallel", "parallel")),
        name="xattn_prompt",
    )(z_q, kv, kv)


def _sample_mix_kernel(zg_ref, zr_ref, shift_ref, lng_ref, lnb_ref, gw_ref, gb_ref,
                       mu_ref, w0_ref, w2_ref, a0_ref, a2_ref, g2_ref, kkp_ref, ka_ref, rk_ref, og_any,
                       og_ref, gv_ref, wt_ref, kkt_ref, bt_ref, kt_ref, rt_ref, vt_ref, bonus_ref, g_ref):
    del og_any
    ge = _gelu(zg_ref[...])
    v = _layernorm(ge[:, GMLP_D:], lng_ref[...], lnb_ref[...])
    gv_ref[...] = v
    og_ref[...] = (ge[:, :GMLP_D] * (v * gw_ref[...] + gb_ref[...])).astype(og_ref.dtype)

    z = zr_ref[...]
    zs = z + (shift_ref[...] - z) * mu_ref[...]
    mixed = _rwkv_mix(zs, w0_ref[...], w2_ref[...], a0_ref[...], a2_ref[...], g2_ref[...], kkp_ref[...],
                      ka_ref[...], rk_ref[...], _head_block_ones())
    for p, (r, k2, vv, logw, kkn, a, g, bonus) in enumerate(mixed):
        sl = slice(p * V7X_LANES, (p + 1) * V7X_LANES)
        wt_ref[sl, :] = jnp.exp(logw).T
        kkt_ref[sl, :] = kkn.T
        bt_ref[sl, :] = (kkn * a).T
        kt_ref[sl, :] = k2.T
        rt_ref[sl, :] = r.T
        vt_ref[sl, :] = vv.T
        bonus_ref[:, sl] = bonus
        g_ref[:, sl] = g


def sample_mix(z_g, z_r, row0, shift, o_g, ln_g, ln_b, gw, gb, mu, w0, w2p, a0, a2p, g2, kkp, ka, rk):
    n = shift.shape[0]
    blk = row0 // n
    tok = lambda w: pl.BlockSpec((n, w), lambda i: (blk, 0))
    loc = lambda w: pl.BlockSpec((n, w), lambda i: (0, 0))
    row = lambda w: pl.BlockSpec((1, w), lambda i: (0, 0))
    full2 = lambda a, b: pl.BlockSpec((a, b), lambda i: (0, 0))
    chan = jax.ShapeDtypeStruct((RWKV_D, n), F32)
    return pl.pallas_call(
        _sample_mix_kernel,
        grid=(1,),
        in_specs=[tok(GMLP_COLS), tok(RWKV_COLS), loc(RWKV_COLS),
                  row(GMLP_D), row(GMLP_D), row(GMLP_D), row(GMLP_D),
                  row(RWKV_COLS), row(RWKV_D), full2(LORA_W + LORA_A, RWKV_D), row(RWKV_D),
                  full2(LORA_W + LORA_A, RWKV_D), full2(LORA_G, RWKV_D), row(RWKV_D), row(RWKV_D), row(RWKV_D),
                  pl.BlockSpec(memory_space=pl.ANY)],
        out_specs=[tok(GMLP_D), loc(GMLP_D)] + [full2(RWKV_D, n)] * 6 + [loc(RWKV_D), loc(RWKV_D)],
        out_shape=[jax.ShapeDtypeStruct(o_g.shape, o_g.dtype), jax.ShapeDtypeStruct((n, GMLP_D), F32)]
                  + [chan] * 6 + [jax.ShapeDtypeStruct((n, RWKV_D), F32)] * 2,
        input_output_aliases={16: 0},
        compiler_params=_params(("arbitrary",)),
        name="sample_mix",
    )(z_g, z_r, shift, ln_g.reshape(1, -1), ln_b.reshape(1, -1), gw, gb, mu, w0, w2p, a0, a2p, g2, kkp, ka, rk, o_g)


def _sample_state_kernel(s_ref, w_ref, kk_ref, b_ref, k_ref, r_ref, v_ref, snew_ref, y_ref):
    n = s_ref.shape[0]
    st = s_ref[...].T.reshape(RWKV_HD, RWKV_HD, n)
    sa = jnp.sum(st * (-kk_ref[...])[None], axis=1, keepdims=True)
    st_new = st * w_ref[...][None] + sa * b_ref[...][None] + v_ref[...] * k_ref[...][None]
    y_ref[...] = jnp.sum(st_new * r_ref[...][None], axis=1, keepdims=True)
    snew_ref[...] = st_new.reshape(RWKV_HD * RWKV_HD, n).T


def sample_state(s0, wt, kkt, bt, kt, rt, vt):
    n = s0.shape[0]
    hd2 = RWKV_HD * RWKV_HD
    state = pl.BlockSpec((n, hd2), lambda h: (0, h))
    keyvec = pl.BlockSpec((RWKV_HD, n), lambda h: (h, 0))
    valvec = pl.BlockSpec((RWKV_HD, 1, n), lambda h: (h, 0, 0))
    return pl.pallas_call(
        _sample_state_kernel,
        grid=(RWKV_HEADS,),
        in_specs=[state, keyvec, keyvec, keyvec, keyvec, keyvec, valvec],
        out_specs=[state, valvec],
        out_shape=[jax.ShapeDtypeStruct((n, RWKV_HEADS * hd2), F32),
                   jax.ShapeDtypeStruct((RWKV_D, 1, n), F32)],
        compiler_params=_params(("parallel",)),
        name="sample_state",
    )(s0.reshape(n, RWKV_HEADS * hd2), wt, kkt, bt, kt, rt, vt.reshape(RWKV_D, 1, n))


def _sample_rwkv_out_kernel(yt_ref, bonus_ref, g_ref, lng_ref, lnb_ref, or_any, or_ref):
    del or_any
    y = yt_ref[...].T
    ones_bd = _head_block_ones()
    for p in range(HEAD_PAIRS):
        sl = slice(p * V7X_LANES, (p + 1) * V7X_LANES)
        o = _group_norm_gate(y[:, sl], bonus_ref[:, sl], g_ref[:, sl], lng_ref[p], lnb_ref[p], ones_bd)
        or_ref[p] = o.astype(or_ref.dtype)


def sample_rwkv_out(yt, bonus, g, lnx_g, lnx_b, o_r, row0):
    n = bonus.shape[0]
    full2 = lambda a, b: pl.BlockSpec((a, b), lambda i: (0, 0))
    pair_row = pl.BlockSpec((HEAD_PAIRS, 1, V7X_LANES), lambda i: (0, 0, 0))
    return pl.pallas_call(
        _sample_rwkv_out_kernel,
        grid=(1,),
        in_specs=[full2(RWKV_D, n), full2(n, RWKV_D), full2(n, RWKV_D), pair_row, pair_row,
                  pl.BlockSpec(memory_space=pl.ANY)],
        out_specs=pl.BlockSpec((HEAD_PAIRS, n, V7X_LANES), lambda i: (0, row0 // n, 0)),
        out_shape=jax.ShapeDtypeStruct(o_r.shape, o_r.dtype),
        input_output_aliases={5: 0},
        compiler_params=_params(("arbitrary",)),
        name="sample_rwkv_out",
    )(yt, bonus, g, lnx_g.reshape(HEAD_PAIRS, 1, V7X_LANES), lnx_b.reshape(HEAD_PAIRS, 1, V7X_LANES), o_r)


def _xattn_sample_kernel(q_ref, ck_ref, cv_ref, ox_any, ox_ref, *, sb):
    del ox_any
    q = q_ref[...]
    rows = N_MEM * XA_HEADS
    hrow = lax.broadcasted_iota(jnp.int32, (8, rows), 0) % XA_HEADS
    hcol = lax.broadcasted_iota(jnp.int32, (8, rows), 1) % XA_HEADS
    own = hrow == hcol
    h8 = lax.broadcasted_iota(jnp.int32, (8, XA_HD), 0) % XA_HEADS
    nt = (((1,), (1,)), ((), ()))
    outs = []
    for s in range(sb):
        qm = jnp.zeros((8, XA_HD), F32)
        for h in range(XA_HEADS):
            qm = jnp.where(h8 == h, q[s:s + 1, h * XA_HD:(h + 1) * XA_HD], qm)
        sc = lax.dot_general(qm.astype(BF16), ck_ref[s].astype(BF16), nt,
                             preferred_element_type=F32) * (XA_HD ** -0.5)
        pr = _softmax_rows(jnp.where(own, sc, NEG_INF))
        res = jnp.dot(pr.astype(BF16), cv_ref[s].astype(BF16), preferred_element_type=F32)
        outs.append(jnp.concatenate([res[h:h + 1, :] for h in range(XA_HEADS)], axis=1))
    ox_ref[...] = jnp.concatenate(outs, axis=0).astype(ox_ref.dtype)


def xattn_sample(z_q, ck, cv, o_x, row0, *, sb=8):
    n = ck.shape[0]
    off = row0 // sb
    tok = pl.BlockSpec((sb, XA_D), lambda i: (i + off, 0))
    cache = pl.BlockSpec((sb, N_MEM * XA_HEADS, XA_HD), lambda i: (i, 0, 0))
    return pl.pallas_call(
        functools.partial(_xattn_sample_kernel, sb=sb),
        grid=(n // sb,),
        in_specs=[tok, cache, cache, pl.BlockSpec(memory_space=pl.ANY)],
        out_specs=tok,
        out_shape=jax.ShapeDtypeStruct(o_x.shape, o_x.dtype),
        input_output_aliases={3: 0},
        compiler_params=_params(("arbitrary",)),
        name="xattn_sample",
    )(z_q, ck, cv, o_x)


def _merge_kernel(og_ref, or_ref, ox_ref, g0_ref, g1_ref, g2_ref, wg_ref, wr_ref, wx_ref, o_ref,
                  wg_bf, wr_bf, wx_bf):
    @pl.when(pl.program_id(1) == 0)
    def _():
        wg_bf[...] = wg_ref[...].astype(BF16)
        wr_bf[...] = wr_ref[...].astype(BF16)
        wx_bf[...] = wx_ref[...].astype(BF16)

    up_g = jnp.dot(og_ref[...], wg_bf[...], preferred_element_type=F32)
    up_r = jnp.dot(or_ref[0], wr_bf[0:V7X_LANES, :], preferred_element_type=F32)
    for p in range(1, HEAD_PAIRS):
        up_r = up_r + jnp.dot(or_ref[p], wr_bf[p * V7X_LANES:(p + 1) * V7X_LANES, :], preferred_element_type=F32)
    up_x = jnp.dot(ox_ref[...], wx_bf[...], preferred_element_type=F32)
    merged = _sigmoid(g0_ref[...]) * up_g + _sigmoid(g1_ref[...]) * up_r + _sigmoid(g2_ref[...]) * up_x
    o_ref[...] = merged.astype(o_ref.dtype)


def merge(o_g, o_r, o_x, z_gate, w_up_g, w_up_r, w_up_x, *, tb, nb=512):
    m = o_g.shape[0]
    nblk = D_MODEL // nb
    gate = lambda b: pl.BlockSpec((tb, nb), lambda j, i: (i, b * nblk + j))
    wspec = lambda k: pl.BlockSpec((k, nb), lambda j, i: (0, j))
    return pl.pallas_call(
        _merge_kernel,
        grid=(nblk, m // tb),
        in_specs=[pl.BlockSpec((tb, GMLP_D), lambda j, i: (i, 0)),
                  pl.BlockSpec((HEAD_PAIRS, tb, V7X_LANES), lambda j, i: (0, i, 0)),
                  pl.BlockSpec((tb, XA_D), lambda j, i: (i, 0)),
                  gate(0), gate(1), gate(2), wspec(GMLP_D), wspec(RWKV_D), wspec(XA_D)],
        out_specs=pl.BlockSpec((tb, nb), lambda j, i: (i, j)),
        out_shape=jax.ShapeDtypeStruct((m, D_MODEL), BF16),
        scratch_shapes=[pltpu.VMEM((GMLP_D, nb), BF16), pltpu.VMEM((RWKV_D, nb), BF16),
                        pltpu.VMEM((XA_D, nb), BF16)],
        compiler_params=_params(("arbitrary", "arbitrary")),
        name="merge",
    )(o_g, o_r, o_x, z_gate, z_gate, z_gate, w_up_g, w_up_r, w_up_x)


def _extract_top(work_ref, rank_ref, vals_ref, n_rows):
    width = work_ref.shape[1]
    riota = lax.broadcasted_iota(jnp.int32, (n_rows, V7X_LANES), 0)
    kiota = lax.broadcasted_iota(jnp.int32, (TOPK, V7X_LANES), 0)
    rank_ref[...] = jnp.full(rank_ref.shape, TOPK, jnp.int32)
    vals_ref[...] = jnp.zeros_like(vals_ref)

    def body(p, carry):
        for c in range(width // V7X_LANES):
            sl = slice(c * V7X_LANES, (c + 1) * V7X_LANES)
            w = work_ref[:, sl]
            m = jnp.max(w, axis=0, keepdims=True)
            idx = jnp.min(jnp.where(w == m, riota, n_rows), axis=0, keepdims=True)
            hit = riota == idx
            rank_ref[:, sl] = jnp.where(hit, p, rank_ref[:, sl])
            work_ref[:, sl] = jnp.where(hit, NEG_INF, w)
            vals_ref[:, sl] = jnp.where(kiota == p, m, vals_ref[:, sl])
        return carry

    lax.fori_loop(0, TOPK, body, 0)


_CAND_COUNT = tuple(TOPK // (a + 1) for a in range(TOPK))
_CAND_START = tuple(sum(_CAND_COUNT[:a]) for a in range(TOPK))
_CAND_ROWS = -(-sum(_CAND_COUNT) // 8) * 8


def _peer_topk_kernel(q_ref, keys_ref, r2_ref, lim_ref, e1_ref, e2_ref,
                      s_scr, work_scr, rank_scr, vals_scr, cand_scr, crank_scr, cvals_scr, *, tbk):
    nt = (((1,), (1,)), ((), ()))
    for h in range(PEER_HEADS):
        for c in range(2):
            qcol = (2 * h + c) * PEER_DH
            col = (2 * h + c) * tbk
            sc = lax.dot_general(keys_ref[h, c], q_ref[:, qcol:qcol + PEER_DH], nt,
                                 preferred_element_type=F32, precision=HIGHEST)
            s_scr[:, col:col + tbk] = sc
            work_scr[:, col:col + tbk] = sc
    _extract_top(work_scr, rank_scr, vals_scr, N_KEYS)

    crow = lax.broadcasted_iota(jnp.int32, (_CAND_ROWS, tbk), 0)
    seg = jnp.full((_CAND_ROWS, tbk), TOPK, jnp.int32)
    for a in reversed(range(TOPK)):
        seg = jnp.where(crow < _CAND_START[a] + _CAND_COUNT[a], jnp.minimum(seg, a), seg)
    pad = jnp.zeros((_CAND_ROWS - TOPK, tbk), F32)
    for h in range(PEER_HEADS):
        v1 = vals_scr[:, (2 * h) * tbk:(2 * h + 1) * tbk]
        v2 = jnp.concatenate([vals_scr[:, (2 * h + 1) * tbk:(2 * h + 2) * tbk], pad], axis=0)
        cand = jnp.full((_CAND_ROWS, tbk), NEG_INF, F32)
        for a in range(TOPK):
            shifted = v2 if _CAND_START[a] == 0 else pltpu.roll(v2, _CAND_START[a], axis=0)
            cand = jnp.where(seg == a, v1[a:a + 1, :] + shifted, cand)
        cand_scr[:, h * tbk:(h + 1) * tbk] = cand
    _extract_top(cand_scr, crank_scr, cvals_scr, _CAND_ROWS)

    for h in range(PEER_HEADS):
        hs = slice(h * tbk, (h + 1) * tbk)
        s1 = slice((2 * h) * tbk, (2 * h + 1) * tbk)
        s2 = slice((2 * h + 1) * tbk, (2 * h + 2) * tbk)
        cvals = cvals_scr[:, hs]
        z = jnp.sum(jnp.exp(cvals - cvals[0:1, :]), axis=0, keepdims=True)
        chosen = crank_scr[:, hs] < TOPK
        rank1 = rank_scr[:, s1]
        lim = jnp.zeros((N_KEYS, tbk), F32)
        for a in range(TOPK):
            count = jnp.sum(jnp.where(jnp.logical_and(chosen, seg == a), 1.0, 0.0), axis=0, keepdims=True)
            lim = jnp.where(rank1 == a, count, lim)
        lim_ref[h] = lim
        r2_ref[h] = rank_scr[:, s2].astype(F32).astype(r2_ref.dtype)
        e1_ref[h] = jnp.exp(s_scr[:, s1] - vals_scr[0:1, s1]) / z
        e2_ref[h] = jnp.exp(s_scr[:, s2] - vals_scr[0:1, s2]).astype(e2_ref.dtype)


def peer_topk(q, keys, *, tbk=128):
    m = q.shape[0]
    out = pl.BlockSpec((PEER_HEADS, N_KEYS, tbk), lambda i: (0, 0, i))
    shape = lambda dt: jax.ShapeDtypeStruct((PEER_HEADS, N_KEYS, m), dt)
    wide = 2 * PEER_HEADS * tbk
    return pl.pallas_call(
        functools.partial(_peer_topk_kernel, tbk=tbk),
        grid=(m // tbk,),
        in_specs=[pl.BlockSpec((tbk, PEER_HEADS * 2 * PEER_DH), lambda i: (i, 0)),
                  pl.BlockSpec((PEER_HEADS, 2, N_KEYS, PEER_DH), lambda i: (0, 0, 0, 0))],
        out_specs=[out] * 4,
        out_shape=[shape(BF16), shape(F32), shape(F32), shape(BF16)],
        scratch_shapes=[pltpu.VMEM((N_KEYS, wide), F32), pltpu.VMEM((N_KEYS, wide), F32),
                        pltpu.VMEM((N_KEYS, wide), jnp.int32), pltpu.VMEM((TOPK, wide), F32),
                        pltpu.VMEM((_CAND_ROWS, PEER_HEADS * tbk), F32),
                        pltpu.VMEM((_CAND_ROWS, PEER_HEADS * tbk), jnp.int32),
                        pltpu.VMEM((TOPK, PEER_HEADS * tbk), F32)],
        compiler_params=_params(("parallel",)),
        name="peer_topk",
    )(q, keys)


PEER_SUB = 2 * N_KEYS


def _peer_dense_kernel(xt_ref, r2_ref, lim_ref, e1_ref, e2_ref, u_ref, v_ref, o_ref, coef_a, coef_b,
                       *, eb, n_blocks):
    e = pl.program_id(1)

    @pl.when(e == 0)
    def _():
        o_ref[...] = jnp.zeros_like(o_ref)
        coef_b[...] = jnp.zeros_like(coef_b)

    blk = jnp.minimum(e, n_blocks - 1)

    n_sub = eb // PEER_SUB
    d_sub = o_ref.shape[1] // n_sub

    def hidden(s):
        return jnp.dot(u_ref[s * PEER_SUB:(s + 1) * PEER_SUB, :], xt_ref[...],
                       preferred_element_type=F32)

    def step(prev_ref, next_ref):
        ht_next = hidden(0)
        for s in range(n_sub):
            ht = ht_next
            cols = slice(s * d_sub, (s + 1) * d_sub)
            o_ref[:, cols] += lax.dot_general(prev_ref[...], v_ref[:, cols], (((0,), (0,)), ((), ())),
                                              preferred_element_type=F32)
            if s + 1 < n_sub:
                ht_next = hidden(s + 1)
            for ii in range(PEER_SUB // N_KEYS):
                i = blk * (eb // N_KEYS) + s * (PEER_SUB // N_KEYS) + ii
                gate = None
                for h in range(PEER_HEADS):
                    lim = lim_ref[h, pl.ds(i, 1), :].astype(BF16)
                    e1 = e1_ref[h, pl.ds(i, 1), :].astype(BF16)
                    term = jnp.where(r2_ref[h] < lim, e2_ref[h] * e1, jnp.zeros((), BF16))
                    gate = term if gate is None else gate + term
                row = s * PEER_SUB + ii * N_KEYS
                next_ref[row:row + N_KEYS, :] = gate * _gelu_to_bf16(ht[ii * N_KEYS:(ii + 1) * N_KEYS, :])

    @pl.when(e % 2 == 0)
    def _():
        step(coef_b, coef_a)

    @pl.when(e % 2 == 1)
    def _():
        step(coef_a, coef_b)


def peer_dense(xt, r2, lim, e1, e2, u_bf, v_bf, *, tbl, eb=1024):
    d, m = xt.shape
    n_blocks = u_bf.shape[0] // eb
    head = pl.BlockSpec((PEER_HEADS, N_KEYS, tbl), lambda t, e: (0, 0, t))
    return pl.pallas_call(
        functools.partial(_peer_dense_kernel, eb=eb, n_blocks=n_blocks),
        grid=(m // tbl, n_blocks + 1),
        in_specs=[pl.BlockSpec((d, tbl), lambda t, e: (0, t)), head, head, head, head,
                  pl.BlockSpec((eb, d), lambda t, e: (jnp.minimum(e, n_blocks - 1), 0)),
                  pl.BlockSpec((eb, d), lambda t, e: (jnp.maximum(e - 1, 0), 0))],
        out_specs=pl.BlockSpec((tbl, d), lambda t, e: (t, 0)),
        out_shape=jax.ShapeDtypeStruct((m, d), F32),
        scratch_shapes=[pltpu.VMEM((eb, tbl), BF16), pltpu.VMEM((eb, tbl), BF16)],
        compiler_params=_params(("parallel", "arbitrary")),
        name="peer_dense",
    )(xt, r2, lim, e1, e2, u_bf, v_bf)


def _final_kernel(h_ref, p_ref, g_ref, o_ref):
    x = h_ref[...] + p_ref[...]
    o_ref[...] = x * lax.rsqrt(jnp.mean(x * x, axis=-1, keepdims=True) + RMS_EPS) * g_ref[...]


def final_norm(h, peer, g, row0, n_rows, *, tb):
    d = h.shape[1]
    off = row0 // tb
    rows = pl.BlockSpec((tb, d), lambda i: (i + off, 0))
    return pl.pallas_call(
        _final_kernel,
        grid=(n_rows // tb,),
        in_specs=[rows, rows, pl.BlockSpec((1, d), lambda i: (0, 0))],
        out_specs=pl.BlockSpec((tb, d), lambda i: (i, 0)),
        out_shape=jax.ShapeDtypeStruct((n_rows, d), F32),
        compiler_params=_params(("parallel",)),
        name="final_norm",
    )(h, peer, g.reshape(1, d))


def kernel(x_prompt, x_sample, mem_prompt, state_shift, state_wkv, cache_mem_k, cache_mem_v, ln1_g, w_in, gmlp_ln_g, gmlp_ln_b, gmlp_ws, gmlp_bs, rwkv_mu, rwkv_w0, rwkv_w2, rwkv_a0, rwkv_a2, rwkv_g2, rwkv_kk, rwkv_ka, rwkv_rk, rwkv_lnx_g, rwkv_lnx_b, mem_norm_g, w_mem_kv, w_up_g, w_up_r, w_up_x, w_out, ln2_g, peer_wq, peer_keys, peer_u, peer_v, final_g):
    depth = w_in.shape[0]
    assert depth == 1, "single-layer step"
    l = 0
    n_batch, seq_len, d = x_prompt.shape
    n_dec = x_sample.shape[0]
    n_prompt = n_batch * seq_len
    m = n_prompt + n_dec
    tb_small = _largest_divisor(m, (640, 128))
    tb_big = _largest_divisor(m, (1664, 640, 128))

    x = jnp.concatenate([x_prompt.reshape(n_prompt, d), x_sample.reshape(n_dec, d)], axis=0)
    xn = rmsnorm(x, ln1_g[l], tb=tb_small, out_dtype=BF16)
    proj = functools.partial(matmul, xn, w_in[l], tb=tb_big, nb=512)
    z_g = proj(col_off=0, n_cols=GMLP_COLS, name="proj_gmlp")
    z_r = proj(col_off=GMLP_COLS, n_cols=RWKV_COLS, name="proj_rwkv")
    z_q = proj(col_off=GMLP_COLS + RWKV_COLS, n_cols=XA_D, name="proj_xattn")
    z_gate = proj(col_off=GMLP_COLS + RWKV_COLS + XA_D, n_cols=GATE_COLS, name="proj_gate")

    memn = rmsnorm(mem_prompt.reshape(n_batch * N_MEM, d), mem_norm_g[l], tb=256, out_dtype=BF16)
    kv = matmul(memn, w_mem_kv[l], tb=N_MEM, nb=512, name="proj_mem_kv")
    p_mk = kv[:, :XA_D].reshape(1, n_batch, N_MEM, XA_HEADS, XA_HD)
    p_mv = kv[:, XA_D:].reshape(1, n_batch, N_MEM, XA_HEADS, XA_HD)

    row = lambda a: a.reshape(1, -1)
    zeros_lora = jnp.zeros((LORA_W, RWKV_D), F32)
    w2p = jnp.concatenate([rwkv_w2[l], zeros_lora], axis=0).astype(BF16)
    a2p = jnp.concatenate([zeros_lora, rwkv_a2[l]], axis=0).astype(BF16)
    rw = (row(rwkv_mu[l]), row(rwkv_w0[l]), w2p, row(rwkv_a0[l]), a2p, rwkv_g2[l].astype(BF16),
          row(rwkv_kk[l]), row(rwkv_ka[l]), row(rwkv_rk[l]))

    o_g = gmlp_prompt(z_g, n_prompt, gmlp_ln_g[l], gmlp_ln_b[l], gmlp_ws[l], gmlp_bs[l])
    prep = rwkv_prep(z_r, n_prompt, seq_len, *rw)
    o_r, st = rwkv_chunks(prep, n_batch, seq_len, m, rwkv_lnx_g[l], rwkv_lnx_b[l])
    o_x = xattn_prompt(z_q, kv, n_batch, seq_len)

    gw = jnp.repeat(gmlp_ws[l][:, 0, 0], CHUNK).reshape(1, GMLP_D)
    gb = jnp.repeat(gmlp_bs[l][:, 0], CHUNK).reshape(1, GMLP_D)
    o_g, s_gv, wt, kkt, bt, kt, rt, vt, bonus_s, g_s = sample_mix(
        z_g, z_r, n_prompt, state_shift[l], o_g, gmlp_ln_g[l], gmlp_ln_b[l], gw, gb, *rw)
    s_wkv, y_s = sample_state(state_wkv[l], wt, kkt, bt, kt, rt, vt)
    o_r = sample_rwkv_out(y_s.reshape(RWKV_D, n_dec), bonus_s, g_s, rwkv_lnx_g[l], rwkv_lnx_b[l], o_r, n_prompt)
    o_x = xattn_sample(z_q, cache_mem_k[l].reshape(n_dec, N_MEM * XA_HEADS, XA_HD),
                       cache_mem_v[l].reshape(n_dec, N_MEM * XA_HEADS, XA_HD), o_x, n_prompt)

    merged = merge(o_g, o_r, o_x, z_gate, w_up_g[l], w_up_r[l], w_up_x[l], tb=tb_small)
    h = matmul(merged, w_out[l], tb=tb_big, nb=512, residual=x, name="proj_out")

    hn, hn_t = rmsnorm(h, ln2_g[l], tb=tb_small, out_dtype=BF16, transposed=True)
    q = matmul(hn, peer_wq[l], tb=tb_big, nb=512, name="proj_peer_q")
    r2, lim, e1, e2 = peer_topk(q, peer_keys[l])
    peer = peer_dense(hn_t, r2, lim, e1, e2, peer_u[l].astype(BF16), peer_v[l].astype(BF16), tbl=tb_small)
    y_prompt = final_norm(h, peer, final_g, 0, n_prompt, tb=512)
    y_sample = final_norm(h, peer, final_g, n_prompt, n_dec, tb=n_dec)

    st = st.reshape(n_batch, HEAD_PAIRS, 2, RWKV_HD, 2, RWKV_HD)
    p_wkv = jnp.stack([st[:, :, 0, :, 0, :], st[:, :, 1, :, 1, :]], axis=2)
    p_wkv = jnp.swapaxes(p_wkv, -1, -2).reshape(1, n_batch, RWKV_HEADS, RWKV_HD, RWKV_HD)
    p_shift = z_r[seq_len - 1:n_prompt:seq_len][None]
    s_shift = z_r[n_prompt:][None]
    s_wkv = s_wkv.reshape(1, n_dec, RWKV_HEADS, RWKV_HD, RWKV_HD)
    return (y_prompt.reshape(n_batch, seq_len, d), y_sample.reshape(n_dec, 1, d), p_mk, p_mv,
            p_shift, p_wkv, s_shift, s_wkv, s_gv.reshape(1, n_dec, 1, GMLP_D))
```

```python
import functools

import jax
import jax.numpy as jnp
from jax import lax
from jax.experimental import pallas as pl
from jax.experimental.pallas import tpu as pltpu

F32 = jnp.float32
BF16 = jnp.bfloat16
HIGHEST = lax.Precision.HIGHEST

D_MODEL = 2048
CHUNK = 128
GMLP_GROUPS = 6
GMLP_D = 768
RWKV_HEADS = 12
RWKV_HD = 64
RWKV_D = 768
LORA_W = 64
LORA_A = 64
LORA_G = 128
RWKV_COLS = 2560
XA_HEADS = 4
XA_HD = 128
XA_D = 512
N_MEM = 256
GMLP_COLS = 2 * GMLP_D
GATE_COLS = 3 * D_MODEL
PEER_HEADS = 8
N_KEYS = 128
PEER_DH = 128
TOPK = 16
RMS_EPS = 1e-6
LN_EPS = 1e-5
GN_EPS = 64e-5

V7X_LANES = 128
V7X_VMEM_LIMIT_BYTES = 56 * 1024 * 1024

HEAD_PAIRS = RWKV_HEADS // 2
RWKV_CHUNK = 64
RWKV_CHUNKS_PER_STEP = 16
NEG_INF = float("-inf")


def _params(semantics):
    return pltpu.CompilerParams(dimension_semantics=semantics, vmem_limit_bytes=V7X_VMEM_LIMIT_BYTES)


def _largest_divisor(m, candidates):
    return next(c for c in candidates if m % c == 0)


def _gelu(x):
    return 0.5 * x * (1.0 + jnp.tanh(0.7978845608028654 * (x + 0.044715 * (x * x * x))))


def _gelu_to_bf16(x):
    inner = (0.7978845608028654 * x) * (1.0 + 0.044715 * (x * x))
    xb = x.astype(BF16)
    return (0.5 * xb) * (1.0 + jnp.tanh(inner.astype(BF16)))


def _split_bf16(x):
    hi = x.astype(BF16)
    return hi, (x - hi.astype(F32)).astype(BF16)


def _sigmoid(x):
    return 1.0 / (1.0 + jnp.exp(-x))


def _rmsnorm_kernel(x_ref, g_ref, *o_refs, transposed):
    x = x_ref[...]
    y = x * lax.rsqrt(jnp.mean(x * x, axis=-1, keepdims=True) + RMS_EPS) * g_ref[...]
    o_refs[0][...] = y.astype(o_refs[0].dtype)
    if transposed:
        o_refs[1][...] = y.T.astype(o_refs[1].dtype)


def rmsnorm(x, g, *, tb, out_dtype, transposed=False):
    m, d = x.shape
    out_shape = [jax.ShapeDtypeStruct((m, d), out_dtype)]
    out_specs = [pl.BlockSpec((tb, d), lambda i: (i, 0))]
    if transposed:
        out_shape.append(jax.ShapeDtypeStruct((d, m), out_dtype))
        out_specs.append(pl.BlockSpec((d, tb), lambda i: (0, i)))
    res = pl.pallas_call(
        functools.partial(_rmsnorm_kernel, transposed=transposed),
        grid=(m // tb,),
        in_specs=[pl.BlockSpec((tb, d), lambda i: (i, 0)), pl.BlockSpec((1, d), lambda i: (0, 0))],
        out_specs=out_specs,
        out_shape=out_shape,
        compiler_params=_params(("parallel",)),
        name="rmsnorm_t" if transposed else "rmsnorm",
    )(x, g.reshape(1, d))
    return res if transposed else res[0]


def _matmul_kernel(a_ref, w_ref, *rest, has_residual):
    if has_residual:
        r_ref, o_ref, wbf_ref = rest
    else:
        o_ref, wbf_ref = rest

    @pl.when(pl.program_id(1) == 0)
    def _():
        wbf_ref[...] = w_ref[...].astype(BF16)

    acc = jnp.dot(a_ref[...], wbf_ref[...], preferred_element_type=F32)
    if has_residual:
        acc = acc + r_ref[...]
    o_ref[...] = acc.astype(o_ref.dtype)


def matmul(a, w, *, tb, nb, col_off=0, n_cols=None, residual=None, out_dtype=F32, name="matmul"):
    m, k = a.shape
    n = w.shape[1] if n_cols is None else n_cols
    off = col_off // nb
    in_specs = [pl.BlockSpec((tb, k), lambda j, i: (i, 0)),
                pl.BlockSpec((k, nb), lambda j, i: (0, j + off))]
    args = [a, w]
    if residual is not None:
        in_specs.append(pl.BlockSpec((tb, nb), lambda j, i: (i, j)))
        args.append(residual)
    return pl.pallas_call(
        functools.partial(_matmul_kernel, has_residual=residual is not None),
        grid=(n // nb, m // tb),
        in_specs=in_specs,
        out_specs=pl.BlockSpec((tb, nb), lambda j, i: (i, j)),
        out_shape=jax.ShapeDtypeStruct((m, n), out_dtype),
        scratch_shapes=[pltpu.VMEM((k, nb), BF16)],
        compiler_params=_params(("arbitrary", "arbitrary")),
        name=name,
    )(*args)


def _layernorm(v, g, b):
    vc = v - jnp.mean(v, axis=-1, keepdims=True)
    var = jnp.mean(vc * vc, axis=-1, keepdims=True)
    return vc * lax.rsqrt(var + LN_EPS) * g + b


def _gmlp_kernel(z_ref, lng_ref, lnb_ref, ws_ref, bst_ref, buf_any, o_ref):
    del buf_any
    ge = _gelu(z_ref[...])
    u = ge[:, :GMLP_D]
    v = _layernorm(ge[:, GMLP_D:], lng_ref[...], lnb_ref[...])
    row = lax.broadcasted_iota(jnp.int32, (CHUNK, CHUNK), 0)
    col = lax.broadcasted_iota(jnp.int32, (CHUNK, CHUNK), 1)
    causal = col <= row
    for g in range(GMLP_GROUPS):
        sl = slice(g * CHUNK, (g + 1) * CHUNK)
        wm = jnp.where(causal, ws_ref[g], 0.0).astype(BF16)
        mixed = jnp.dot(wm, v[:, sl].astype(BF16), preferred_element_type=F32) + bst_ref[:, g:g + 1]
        o_ref[:, sl] = (u[:, sl] * mixed).astype(o_ref.dtype)


def gmlp_prompt(z_g, buf, n_tokens, ln_g, ln_b, ws, bs):
    return pl.pallas_call(
        _gmlp_kernel,
        grid=(n_tokens // CHUNK,),
        in_specs=[pl.BlockSpec((CHUNK, GMLP_COLS), lambda i: (i, 0)),
                  pl.BlockSpec((1, GMLP_D), lambda i: (0, 0)),
                  pl.BlockSpec((1, GMLP_D), lambda i: (0, 0)),
                  pl.BlockSpec((GMLP_GROUPS, CHUNK, CHUNK), lambda i: (0, 0, 0)),
                  pl.BlockSpec((CHUNK, GMLP_GROUPS), lambda i: (0, 0)),
                  pl.BlockSpec(memory_space=pl.ANY)],
        out_specs=pl.BlockSpec((CHUNK, GMLP_D), lambda i: (i, 0)),
        out_shape=jax.ShapeDtypeStruct(buf.shape, buf.dtype),
        input_output_aliases={5: 0},
        compiler_params=_params(("parallel",)),
        name="gmlp_prompt",
    )(z_g, ln_g.reshape(1, GMLP_D), ln_b.reshape(1, GMLP_D), ws, bs.T, buf)


def _head_block_ones():
    r = lax.broadcasted_iota(jnp.int32, (V7X_LANES, V7X_LANES), 0) // RWKV_HD
    c = lax.broadcasted_iota(jnp.int32, (V7X_LANES, V7X_LANES), 1) // RWKV_HD
    return (r == c).astype(F32)


def _head_sum(x, ones_bd):
    return jnp.dot(x, ones_bd, preferred_element_type=F32, precision=HIGHEST)


def _rwkv_mix(zs, w0, w2p, a0, a2p, g2, kkp, ka, rk, ones_bd):
    r = zs[:, 0:RWKV_D]
    k = zs[:, RWKV_D:2 * RWKV_D]
    v = zs[:, 2 * RWKV_D:3 * RWKV_D]
    lwa = zs[:, 3 * RWKV_D:3 * RWKV_D + LORA_W + LORA_A]
    lg = zs[:, 3 * RWKV_D + LORA_W + LORA_A:]
    lora_w = jnp.dot(jnp.tanh(lwa).astype(BF16), w2p, preferred_element_type=F32)
    lora_a = jnp.dot(lwa.astype(BF16), a2p, preferred_element_type=F32)
    x = -(w0 + lora_w)
    softplus = jnp.maximum(x, 0.0) + jnp.log(1.0 + jnp.exp(-jnp.abs(x)))
    logw = -jnp.exp(-softplus - 0.5)
    a = _sigmoid(a0 + lora_a)
    g = jnp.dot(_sigmoid(lg).astype(BF16), g2, preferred_element_type=F32)
    kk = k * kkp
    k2 = k * (1.0 + (a - 1.0) * ka)
    rkk = r * k2 * rk
    out = []
    for p in range(HEAD_PAIRS):
        sl = slice(p * V7X_LANES, (p + 1) * V7X_LANES)
        kk_p = kk[:, sl]
        norm = jnp.sqrt(_head_sum(kk_p * kk_p, ones_bd))
        kkn = kk_p / jnp.maximum(norm, 1e-12)
        bonus = _head_sum(rkk[:, sl], ones_bd) * v[:, sl]
        out.append((r[:, sl], k2[:, sl], v[:, sl], logw[:, sl], kkn, a[:, sl], g[:, sl], bonus))
    return out


def _rwkv_prep_kernel(z_ref, zp_ref, mu_ref, w0_ref, w2_ref, a0_ref, a2_ref, g2_ref, kkp_ref, ka_ref, rk_ref,
                      at_ref, btp_ref, ktp_ref, rt_ref, v_ref, g_ref, bonus_ref, w_ref, pc_ref, cs_scr,
                      *, tb, blocks_per_seq):
    i = pl.program_id(0)
    z = z_ref[...]
    first = (i % blocks_per_seq) == 0
    prev_row = jnp.where(first, 0.0, zp_ref[7:8, :])
    row = lax.broadcasted_iota(jnp.int32, (tb, 1), 0)
    zprev = jnp.where(row == 0, prev_row, pltpu.roll(z, 1, axis=0))
    zs = z + (zprev - z) * mu_ref[...]
    ones_bd = _head_block_ones()
    mixed = _rwkv_mix(zs, w0_ref[...], w2_ref[...], a0_ref[...], a2_ref[...], g2_ref[...], kkp_ref[...],
                      ka_ref[...], rk_ref[...], ones_bd)
    rowc = row % RWKV_CHUNK
    n_chunks = tb // RWKV_CHUNK
    for p, (r, k2, v, logw, kkn, a, g, bonus) in enumerate(mixed):
        cs = logw
        shift = 1
        while shift < RWKV_CHUNK:
            cs = cs + jnp.where(rowc >= shift, pltpu.roll(cs, shift, axis=0), 0.0)
            shift *= 2
        cs_scr[...] = cs
        cs_end = cs_scr[pl.ds(RWKV_CHUNK - 1, n_chunks, stride=RWKV_CHUNK), :]
        rs = logw
        shift = 1
        while shift < RWKV_CHUNK:
            rs = rs + jnp.where(rowc + shift < RWKV_CHUNK, pltpu.roll(rs, tb - shift, axis=0), 0.0)
            shift *= 2
        tail = jnp.exp(rs - logw)
        inv_p = jnp.exp(-cs)
        at_ref[p] = (jnp.exp(cs - logw) * kkn).astype(BF16)
        btp_ref[p] = (kkn * a * tail).astype(BF16)
        ktp_ref[p] = (k2 * tail).astype(BF16)
        rt_ref[p] = (jnp.exp(cs) * r).astype(BF16)
        v_ref[p] = v.astype(BF16)
        g_ref[p] = g
        bonus_ref[p] = bonus
        w_ref[p] = jnp.concatenate([kkn * a * inv_p, k2 * inv_p], axis=-1).astype(BF16)
        pc_ref[p] = jnp.exp(cs_end)


def rwkv_prep(z_r, n_tokens, seq_len, mu, w0, w2p, a0, a2p, g2, kkp, ka, rk, *, tb=512):
    n_chunks = n_tokens // RWKV_CHUNK
    row_spec = lambda w: pl.BlockSpec((1, w), lambda i: (0, 0))
    pair_out = lambda w: pl.BlockSpec((HEAD_PAIRS, tb, w), lambda i: (0, i, 0))
    pair_shape = lambda w, dt=BF16: jax.ShapeDtypeStruct((HEAD_PAIRS, n_tokens, w), dt)
    return pl.pallas_call(
        functools.partial(_rwkv_prep_kernel, tb=tb, blocks_per_seq=seq_len // tb),
        grid=(n_tokens // tb,),
        in_specs=[pl.BlockSpec((tb, RWKV_COLS), lambda i: (i, 0)),
                  pl.BlockSpec((8, RWKV_COLS), lambda i: (jnp.maximum(i * (tb // 8) - 1, 0), 0)),
                  row_spec(RWKV_COLS), row_spec(RWKV_D),
                  pl.BlockSpec((LORA_W + LORA_A, RWKV_D), lambda i: (0, 0)),
                  row_spec(RWKV_D),
                  pl.BlockSpec((LORA_W + LORA_A, RWKV_D), lambda i: (0, 0)),
                  pl.BlockSpec((LORA_G, RWKV_D), lambda i: (0, 0)),
                  row_spec(RWKV_D), row_spec(RWKV_D), row_spec(RWKV_D)],
        out_specs=[pair_out(V7X_LANES)] * 7 + [pair_out(2 * V7X_LANES),
                   pl.BlockSpec((HEAD_PAIRS, tb // RWKV_CHUNK, V7X_LANES), lambda i: (0, i, 0))],
        out_shape=[pair_shape(V7X_LANES)] * 5 + [pair_shape(V7X_LANES, F32)] * 2 + [pair_shape(2 * V7X_LANES),
                   jax.ShapeDtypeStruct((HEAD_PAIRS, n_chunks, V7X_LANES), F32)],
        scratch_shapes=[pltpu.VMEM((tb, V7X_LANES), F32)],
        compiler_params=_params(("parallel",)),
        name="rwkv_prep",
    )(z_r, z_r, mu, w0, w2p, a0, a2p, g2, kkp, ka, rk)


def _bmm(x, y):
    return lax.dot_general(x.astype(BF16), y.astype(BF16), (((2,), (1,)), ((0,), (0,))),
                           preferred_element_type=F32)


def _bmm_nt(x, y):
    return lax.dot_general(x.astype(BF16), y.astype(BF16), (((2,), (2,)), ((0,), (0,))),
                           preferred_element_type=F32)


def _mm_tn(x, y, precision=None):
    return lax.dot_general(x, y, (((0,), (0,)), ((), ())), preferred_element_type=F32, precision=precision)


def _group_norm_gate(y, bonus, g, lng, lnb, ones_bd):
    mean = _head_sum(y, ones_bd) * (1.0 / RWKV_HD)
    yc = y - mean
    var = _head_sum(yc * yc, ones_bd) * (1.0 / RWKV_HD)
    return (yc * lax.rsqrt(var + GN_EPS) * lng + lnb + bonus) * g


def _rwkv_chunk_kernel(at_ref, btp_ref, ktp_ref, rt_ref, v_ref, g_ref, bonus_ref, w_ref, pc_ref, lng_ref,
                       lnb_ref, buf_any, o_ref, st_ref, st_scr):
    del buf_any
    nc, c, ln = RWKV_CHUNKS_PER_STEP, RWKV_CHUNK, V7X_LANES
    t = pl.program_id(1)

    @pl.when(t == 0)
    def _():
        st_scr[...] = jnp.zeros_like(st_scr)

    shape3 = (nc, c, ln)
    at = at_ref[0].reshape(shape3)
    rt = rt_ref[0].reshape(shape3)
    v = v_ref[0].reshape(shape3)
    btk = w_ref[0].reshape(nc, c, 2 * ln)
    bt, kt = btk[..., :ln], btk[..., ln:]
    lane = lax.broadcasted_iota(jnp.int32, (1, 1, ln), 2)
    head0 = lane < RWKV_HD
    head0_2 = jnp.concatenate([head0, head0], axis=-1)
    zero = jnp.zeros(shape3, F32)

    lhs = jnp.concatenate([jnp.where(head0, at, 0.0), jnp.where(head0, rt, 0.0),
                           jnp.where(head0, 0.0, at), jnp.where(head0, 0.0, rt)], axis=1)
    gram = _bmm_nt(lhs, jnp.concatenate([bt, kt], axis=1))
    gr = lax.broadcasted_iota(jnp.int32, (1, 4 * c, 2 * c), 1)
    gc = lax.broadcasted_iota(jnp.int32, (1, 4 * c, 2 * c), 2) % c
    causal = gr % c + (gr // c) % 2 > gc
    gram = jnp.where(causal, gram, 0.0)
    zv = jnp.concatenate([jnp.zeros(shape3, BF16), v], axis=1)
    first = lax.broadcasted_iota(jnp.int32, (1, 1, 2 * c), 2) < c

    xs, mvs, mwus = [], [], []
    for h in range(2):
        g_h = gram[:, 2 * h * c:(2 * h + 2) * c, :]
        lmv = _bmm(g_h, zv)
        x = jnp.concatenate([at.astype(F32), lmv[:, :c, :]], axis=-1)
        lp = jnp.where(first, -g_h[:, :c, :], 0.0)
        pad2 = jnp.zeros((nc, c, 2 * ln), F32)
        x = x + _bmm(lp, jnp.concatenate([x, pad2], axis=1))
        n = 2
        while n < c:
            lp = _bmm(lp, jnp.concatenate([lp, zero], axis=1))
            x = x + _bmm(lp, jnp.concatenate([x, pad2], axis=1))
            n *= 2
        xs.append(x)
        mvs.append(lmv[:, c:, :])
        mrb = jnp.where(first, g_h[:, c:, :], 0.0)
        mwus.append(_bmm(mrb, jnp.concatenate([x, pad2], axis=1)))
    x = jnp.where(head0_2, xs[0], xs[1])
    mv = jnp.where(head0, mvs[0], mvs[1])
    mwu = jnp.where(head0_2, mwus[0], mwus[1])
    qe = (rt.astype(F32) - mwu[..., :ln]).astype(BF16)
    yl = mv - mwu[..., ln:]

    rr = lax.broadcasted_iota(jnp.int32, (ln, ln), 0)
    cc = lax.broadcasted_iota(jnp.int32, (ln, ln), 1)
    same_head = (rr // RWKV_HD) == (cc // RWKV_HD)
    eye = rr == cc
    btp = btp_ref[0].reshape(shape3).astype(BF16)
    ktp = ktp_ref[0].reshape(shape3).astype(BF16)
    x_bf = x.astype(BF16)
    v_bf = v.astype(BF16)
    pc = pc_ref[0]
    st = st_scr[...]
    ys = []
    for j in range(nc):
        bwu = _mm_tn(btp[j], x_bf[j])
        kv = _mm_tn(ktp[j], v_bf[j])
        tr = jnp.where(same_head, jnp.where(eye, pc[j:j + 1, :], 0.0) - bwu[:, :ln], 0.0)
        ad = jnp.where(same_head, kv - bwu[:, ln:], 0.0)
        st_hi = st.astype(BF16)
        st_lo = (st - st_hi.astype(F32)).astype(BF16)
        ys.append(jnp.dot(qe[j], st_hi, preferred_element_type=F32) + yl[j])
        tr_bf = tr.astype(BF16)
        st = (jnp.dot(tr_bf, st_hi, preferred_element_type=F32)
              + jnp.dot(tr_bf, st_lo, preferred_element_type=F32) + ad)
    st_scr[...] = st
    st_ref[0] = st
    y = jnp.concatenate(ys, axis=0)
    o = _group_norm_gate(y, bonus_ref[0], g_ref[0], lng_ref[0], lnb_ref[0], _head_block_ones())
    o_ref[0] = o.astype(o_ref.dtype)


def rwkv_chunks(prep, buf, n_batch, seq_len, lnx_g, lnx_b):
    at, btp, ktp, rt, v, g, bonus, wbk, pc = prep
    rows = RWKV_CHUNK * RWKV_CHUNKS_PER_STEP
    steps = seq_len // rows
    n_tokens = n_batch * seq_len
    tok = lambda w: pl.BlockSpec((1, rows, w), lambda bp, t: (bp % HEAD_PAIRS, (bp // HEAD_PAIRS) * steps + t, 0))
    pair_row = pl.BlockSpec((1, 1, V7X_LANES), lambda bp, t: (bp % HEAD_PAIRS, 0, 0))
    return pl.pallas_call(
        _rwkv_chunk_kernel,
        grid=(n_batch * HEAD_PAIRS, steps),
        in_specs=[tok(V7X_LANES)] * 7 + [tok(2 * V7X_LANES),
                  pl.BlockSpec((1, RWKV_CHUNKS_PER_STEP, V7X_LANES),
                               lambda bp, t: (bp % HEAD_PAIRS, (bp // HEAD_PAIRS) * steps + t, 0)),
                  pair_row, pair_row, pl.BlockSpec(memory_space=pl.ANY)],
        out_specs=[tok(V7X_LANES),
                   pl.BlockSpec((1, V7X_LANES, V7X_LANES), lambda bp, t: (bp, 0, 0))],
        out_shape=[jax.ShapeDtypeStruct(buf.shape, buf.dtype),
                   jax.ShapeDtypeStruct((n_batch * HEAD_PAIRS, V7X_LANES, V7X_LANES), F32)],
        input_output_aliases={11: 0},
        scratch_shapes=[pltpu.VMEM((V7X_LANES, V7X_LANES), F32)],
        compiler_params=_params(("parallel", "arbitrary")),
        name="rwkv_chunks",
    )(at, btp, ktp, rt, v, g, bonus, wbk, pc,
      lnx_g.reshape(HEAD_PAIRS, 1, V7X_LANES), lnx_b.reshape(HEAD_PAIRS, 1, V7X_LANES), buf)


def _softmax_rows(s):
    e = jnp.exp(s - jnp.max(s, axis=-1, keepdims=True))
    return e / jnp.sum(e, axis=-1, keepdims=True)


def _xattn_kernel(q_ref, k_ref, v_ref, buf_any, o_ref):
    del buf_any
    for h in range(XA_HEADS):
        sl = slice(h * XA_HD, (h + 1) * XA_HD)
        s = lax.dot_general(q_ref[:, sl].astype(BF16), k_ref[:, sl].astype(BF16), (((1,), (1,)), ((), ())),
                            preferred_element_type=F32) * (XA_HD ** -0.5)
        pr = _softmax_rows(s)
        o_ref[:, sl] = jnp.dot(pr.astype(BF16), v_ref[:, sl].astype(BF16),
                               preferred_element_type=F32).astype(o_ref.dtype)


def xattn_prompt(z_q, kv, buf, n_batch, seq_len, *, tq=512):
    steps = seq_len // tq
    return pl.pallas_call(
        _xattn_kernel,
        grid=(n_batch, steps),
        in_specs=[pl.BlockSpec((tq, XA_D), lambda b, t: (b * steps + t, 0)),
                  pl.BlockSpec((N_MEM, XA_D), lambda b, t: (b, 0)),
                  pl.BlockSpec((N_MEM, XA_D), lambda b, t: (b, 1)),
                  pl.BlockSpec(memory_space=pl.ANY)],
        out_specs=pl.BlockSpec((tq, XA_D), lambda b, t: (b * steps + t, 0)),
        out_shape=jax.ShapeDtypeStruct(buf.shape, buf.dtype),
        input_output_aliases={3: 0},
        compiler_params=_params(("parallel", "parallel")),
        name="xattn_prompt",
    )(z_q, kv, kv, buf)


def _sample_mix_kernel(zg_ref, zr_ref, shift_ref, lng_ref, lnb_ref, gw_ref, gb_ref,
                       mu_ref, w0_ref, w2_ref, a0_ref, a2_ref, g2_ref, kkp_ref, ka_ref, rk_ref, og_any,
                       og_ref, gv_ref, wt_ref, kkt_ref, bt_ref, kt_ref, rt_ref, vt_ref, bonus_ref, g_ref):
    del og_any
    ge = _gelu(zg_ref[...])
    v = _layernorm(ge[:, GMLP_D:], lng_ref[...], lnb_ref[...])
    gv_ref[...] = v
    og_ref[...] = (ge[:, :GMLP_D] * (v * gw_ref[...] + gb_ref[...])).astype(og_ref.dtype)

    z = zr_ref[...]
    zs = z + (shift_ref[...] - z) * mu_ref[...]
    mixed = _rwkv_mix(zs, w0_ref[...], w2_ref[...], a0_ref[...], a2_ref[...], g2_ref[...], kkp_ref[...],
                      ka_ref[...], rk_ref[...], _head_block_ones())
    for p, (r, k2, vv, logw, kkn, a, g, bonus) in enumerate(mixed):
        sl = slice(p * V7X_LANES, (p + 1) * V7X_LANES)
        wt_ref[sl, :] = jnp.exp(logw).T
        kkt_ref[sl, :] = kkn.T
        bt_ref[sl, :] = (kkn * a).T
        kt_ref[sl, :] = k2.T
        rt_ref[sl, :] = r.T
        vt_ref[sl, :] = vv.T
        bonus_ref[:, sl] = bonus
        g_ref[:, sl] = g


def sample_mix(z_g, z_r, row0, shift, o_g, ln_g, ln_b, gw, gb, mu, w0, w2p, a0, a2p, g2, kkp, ka, rk):
    n = shift.shape[0]
    blk = row0 // n
    tok = lambda w: pl.BlockSpec((n, w), lambda i: (blk, 0))
    loc = lambda w: pl.BlockSpec((n, w), lambda i: (0, 0))
    row = lambda w: pl.BlockSpec((1, w), lambda i: (0, 0))
    full2 = lambda a, b: pl.BlockSpec((a, b), lambda i: (0, 0))
    chan = jax.ShapeDtypeStruct((RWKV_D, n), F32)
    return pl.pallas_call(
        _sample_mix_kernel,
        grid=(1,),
        in_specs=[tok(GMLP_COLS), tok(RWKV_COLS), loc(RWKV_COLS),
                  row(GMLP_D), row(GMLP_D), row(GMLP_D), row(GMLP_D),
                  row(RWKV_COLS), row(RWKV_D), full2(LORA_W + LORA_A, RWKV_D), row(RWKV_D),
                  full2(LORA_W + LORA_A, RWKV_D), full2(LORA_G, RWKV_D), row(RWKV_D), row(RWKV_D), row(RWKV_D),
                  pl.BlockSpec(memory_space=pl.ANY)],
        out_specs=[tok(GMLP_D), loc(GMLP_D)] + [full2(RWKV_D, n)] * 6 + [loc(RWKV_D), loc(RWKV_D)],
        out_shape=[jax.ShapeDtypeStruct(o_g.shape, o_g.dtype), jax.ShapeDtypeStruct((n, GMLP_D), F32)]
                  + [chan] * 6 + [jax.ShapeDtypeStruct((n, RWKV_D), F32)] * 2,
        input_output_aliases={16: 0},
        compiler_params=_params(("arbitrary",)),
        name="sample_mix",
    )(z_g, z_r, shift, ln_g.reshape(1, -1), ln_b.reshape(1, -1), gw, gb, mu, w0, w2p, a0, a2p, g2, kkp, ka, rk, o_g)


def _sample_state_kernel(s_ref, w_ref, kk_ref, b_ref, k_ref, r_ref, v_ref, snew_ref, y_ref):
    n = s_ref.shape[0]
    st = s_ref[...].T.reshape(RWKV_HD, RWKV_HD, n)
    sa = jnp.sum(st * (-kk_ref[...])[None], axis=1, keepdims=True)
    st_new = st * w_ref[...][None] + sa * b_ref[...][None] + v_ref[...] * k_ref[...][None]
    y_ref[...] = jnp.sum(st_new * r_ref[...][None], axis=1, keepdims=True)
    snew_ref[...] = st_new.reshape(RWKV_HD * RWKV_HD, n).T


def sample_state(s0, wt, kkt, bt, kt, rt, vt):
    n = s0.shape[0]
    hd2 = RWKV_HD * RWKV_HD
    state = pl.BlockSpec((n, hd2), lambda h: (0, h))
    keyvec = pl.BlockSpec((RWKV_HD, n), lambda h: (h, 0))
    valvec = pl.BlockSpec((RWKV_HD, 1, n), lambda h: (h, 0, 0))
    return pl.pallas_call(
        _sample_state_kernel,
        grid=(RWKV_HEADS,),
        in_specs=[state, keyvec, keyvec, keyvec, keyvec, keyvec, valvec],
        out_specs=[state, valvec],
        out_shape=[jax.ShapeDtypeStruct((n, RWKV_HEADS * hd2), F32),
                   jax.ShapeDtypeStruct((RWKV_D, 1, n), F32)],
        compiler_params=_params(("parallel",)),
        name="sample_state",
    )(s0.reshape(n, RWKV_HEADS * hd2), wt, kkt, bt, kt, rt, vt.reshape(RWKV_D, 1, n))


def _sample_rwkv_out_kernel(yt_ref, bonus_ref, g_ref, lng_ref, lnb_ref, or_any, or_ref):
    del or_any
    y = yt_ref[...].T
    ones_bd = _head_block_ones()
    for p in range(HEAD_PAIRS):
        sl = slice(p * V7X_LANES, (p + 1) * V7X_LANES)
        o = _group_norm_gate(y[:, sl], bonus_ref[:, sl], g_ref[:, sl], lng_ref[p], lnb_ref[p], ones_bd)
        or_ref[p] = o.astype(or_ref.dtype)


def sample_rwkv_out(yt, bonus, g, lnx_g, lnx_b, o_r, row0):
    n = bonus.shape[0]
    full2 = lambda a, b: pl.BlockSpec((a, b), lambda i: (0, 0))
    pair_row = pl.BlockSpec((HEAD_PAIRS, 1, V7X_LANES), lambda i: (0, 0, 0))
    return pl.pallas_call(
        _sample_rwkv_out_kernel,
        grid=(1,),
        in_specs=[full2(RWKV_D, n), full2(n, RWKV_D), full2(n, RWKV_D), pair_row, pair_row,
                  pl.BlockSpec(memory_space=pl.ANY)],
        out_specs=pl.BlockSpec((HEAD_PAIRS, n, V7X_LANES), lambda i: (0, row0 // n, 0)),
        out_shape=jax.ShapeDtypeStruct(o_r.shape, o_r.dtype),
        input_output_aliases={5: 0},
        compiler_params=_params(("arbitrary",)),
        name="sample_rwkv_out",
    )(yt, bonus, g, lnx_g.reshape(HEAD_PAIRS, 1, V7X_LANES), lnx_b.reshape(HEAD_PAIRS, 1, V7X_LANES), o_r)


def _xattn_sample_kernel(q_ref, ck_ref, cv_ref, ox_any, ox_ref, *, sb):
    del ox_any
    q = q_ref[...]
    rows = N_MEM * XA_HEADS
    hrow = lax.broadcasted_iota(jnp.int32, (8, rows), 0) % XA_HEADS
    hcol = lax.broadcasted_iota(jnp.int32, (8, rows), 1) % XA_HEADS
    own = hrow == hcol
    h8 = lax.broadcasted_iota(jnp.int32, (8, XA_HD), 0) % XA_HEADS
    nt = (((1,), (1,)), ((), ()))
    outs = []
    for s in range(sb):
        qm = jnp.zeros((8, XA_HD), F32)
        for h in range(XA_HEADS):
            qm = jnp.where(h8 == h, q[s:s + 1, h * XA_HD:(h + 1) * XA_HD], qm)
        sc = lax.dot_general(qm.astype(BF16), ck_ref[s].astype(BF16), nt,
                             preferred_element_type=F32) * (XA_HD ** -0.5)
        pr = _softmax_rows(jnp.where(own, sc, NEG_INF))
        res = jnp.dot(pr.astype(BF16), cv_ref[s].astype(BF16), preferred_element_type=F32)
        outs.append(jnp.concatenate([res[h:h + 1, :] for h in range(XA_HEADS)], axis=1))
    ox_ref[...] = jnp.concatenate(outs, axis=0).astype(ox_ref.dtype)


def xattn_sample(z_q, ck, cv, o_x, row0, *, sb=8):
    n = ck.shape[0]
    off = row0 // sb
    tok = pl.BlockSpec((sb, XA_D), lambda i: (i + off, 0))
    cache = pl.BlockSpec((sb, N_MEM * XA_HEADS, XA_HD), lambda i: (i, 0, 0))
    return pl.pallas_call(
        functools.partial(_xattn_sample_kernel, sb=sb),
        grid=(n // sb,),
        in_specs=[tok, cache, cache, pl.BlockSpec(memory_space=pl.ANY)],
        out_specs=tok,
        out_shape=jax.ShapeDtypeStruct(o_x.shape, o_x.dtype),
        input_output_aliases={3: 0},
        compiler_params=_params(("arbitrary",)),
        name="xattn_sample",
    )(z_q, ck, cv, o_x)


def _merge_kernel(og_ref, or_ref, ox_ref, g0_ref, g1_ref, g2_ref, wg_ref, wr_ref, wx_ref, o_ref,
                  wg_bf, wr_bf, wx_bf):
    @pl.when(pl.program_id(1) == 0)
    def _():
        wg_bf[...] = wg_ref[...].astype(BF16)
        wr_bf[...] = wr_ref[...].astype(BF16)
        wx_bf[...] = wx_ref[...].astype(BF16)

    up_g = jnp.dot(og_ref[...], wg_bf[...], preferred_element_type=F32)
    up_r = jnp.dot(or_ref[0], wr_bf[0:V7X_LANES, :], preferred_element_type=F32)
    for p in range(1, HEAD_PAIRS):
        up_r = up_r + jnp.dot(or_ref[p], wr_bf[p * V7X_LANES:(p + 1) * V7X_LANES, :], preferred_element_type=F32)
    up_x = jnp.dot(ox_ref[...], wx_bf[...], preferred_element_type=F32)
    merged = _sigmoid(g0_ref[...]) * up_g + _sigmoid(g1_ref[...]) * up_r + _sigmoid(g2_ref[...]) * up_x
    o_ref[...] = merged.astype(o_ref.dtype)


def merge(o_g, o_r, o_x, z_gate, w_up_g, w_up_r, w_up_x, *, tb, nb=512):
    m = o_g.shape[0]
    nblk = D_MODEL // nb
    gate = lambda b: pl.BlockSpec((tb, nb), lambda j, i: (i, b * nblk + j))
    wspec = lambda k: pl.BlockSpec((k, nb), lambda j, i: (0, j))
    return pl.pallas_call(
        _merge_kernel,
        grid=(nblk, m // tb),
        in_specs=[pl.BlockSpec((tb, GMLP_D), lambda j, i: (i, 0)),
                  pl.BlockSpec((HEAD_PAIRS, tb, V7X_LANES), lambda j, i: (0, i, 0)),
                  pl.BlockSpec((tb, XA_D), lambda j, i: (i, 0)),
                  gate(0), gate(1), gate(2), wspec(GMLP_D), wspec(RWKV_D), wspec(XA_D)],
        out_specs=pl.BlockSpec((tb, nb), lambda j, i: (i, j)),
        out_shape=jax.ShapeDtypeStruct((m, D_MODEL), BF16),
        scratch_shapes=[pltpu.VMEM((GMLP_D, nb), BF16), pltpu.VMEM((RWKV_D, nb), BF16),
                        pltpu.VMEM((XA_D, nb), BF16)],
        compiler_params=_params(("arbitrary", "arbitrary")),
        name="merge",
    )(o_g, o_r, o_x, z_gate, z_gate, z_gate, w_up_g, w_up_r, w_up_x)


def _extract_top(work_ref, rank_ref, vals_ref, n_rows):
    width = work_ref.shape[1]
    riota = lax.broadcasted_iota(jnp.int32, (n_rows, V7X_LANES), 0)
    kiota = lax.broadcasted_iota(jnp.int32, (TOPK, V7X_LANES), 0)
    rank_ref[...] = jnp.full(rank_ref.shape, TOPK, jnp.int32)
    vals_ref[...] = jnp.zeros_like(vals_ref)

    def body(p, carry):
        for c in range(width // V7X_LANES):
            sl = slice(c * V7X_LANES, (c + 1) * V7X_LANES)
            w = work_ref[:, sl]
            m = jnp.max(w, axis=0, keepdims=True)
            idx = jnp.min(jnp.where(w == m, riota, n_rows), axis=0, keepdims=True)
            hit = riota == idx
            rank_ref[:, sl] = jnp.where(hit, p, rank_ref[:, sl])
            work_ref[:, sl] = jnp.where(hit, NEG_INF, w)
            vals_ref[:, sl] = jnp.where(kiota == p, m, vals_ref[:, sl])
        return carry

    lax.fori_loop(0, TOPK, body, 0)


_CAND_COUNT = tuple(TOPK // (a + 1) for a in range(TOPK))
_CAND_START = tuple(sum(_CAND_COUNT[:a]) for a in range(TOPK))
_CAND_ROWS = -(-sum(_CAND_COUNT) // 8) * 8


def _peer_topk_kernel(q_ref, keys_ref, r2_ref, lim_ref, e1_ref, e2_ref,
                      s_scr, work_scr, rank_scr, vals_scr, cand_scr, crank_scr, cvals_scr, *, tbk):
    nt = (((1,), (1,)), ((), ()))
    for h in range(PEER_HEADS):
        for c in range(2):
            qcol = (2 * h + c) * PEER_DH
            col = (2 * h + c) * tbk
            k_hi, k_lo = _split_bf16(keys_ref[h, c])
            q_hi, q_lo = _split_bf16(q_ref[:, qcol:qcol + PEER_DH])
            sc = (lax.dot_general(k_hi, q_hi, nt, preferred_element_type=F32)
                  + lax.dot_general(k_hi, q_lo, nt, preferred_element_type=F32)
                  + lax.dot_general(k_lo, q_hi, nt, preferred_element_type=F32))
            s_scr[:, col:col + tbk] = sc
            work_scr[:, col:col + tbk] = sc
    _extract_top(work_scr, rank_scr, vals_scr, N_KEYS)

    crow = lax.broadcasted_iota(jnp.int32, (_CAND_ROWS, tbk), 0)
    seg = jnp.full((_CAND_ROWS, tbk), TOPK, jnp.int32)
    for a in reversed(range(TOPK)):
        seg = jnp.where(crow < _CAND_START[a] + _CAND_COUNT[a], jnp.minimum(seg, a), seg)
    pad = jnp.zeros((_CAND_ROWS - TOPK, tbk), F32)
    for h in range(PEER_HEADS):
        v1 = vals_scr[:, (2 * h) * tbk:(2 * h + 1) * tbk]
        v2 = jnp.concatenate([vals_scr[:, (2 * h + 1) * tbk:(2 * h + 2) * tbk], pad], axis=0)
        cand = jnp.full((_CAND_ROWS, tbk), NEG_INF, F32)
        for a in range(TOPK):
            shifted = v2 if _CAND_START[a] == 0 else pltpu.roll(v2, _CAND_START[a], axis=0)
            cand = jnp.where(seg == a, v1[a:a + 1, :] + shifted, cand)
        cand_scr[:, h * tbk:(h + 1) * tbk] = cand
    _extract_top(cand_scr, crank_scr, cvals_scr, _CAND_ROWS)

    for h in range(PEER_HEADS):
        hs = slice(h * tbk, (h + 1) * tbk)
        s1 = slice((2 * h) * tbk, (2 * h + 1) * tbk)
        s2 = slice((2 * h + 1) * tbk, (2 * h + 2) * tbk)
        cvals = cvals_scr[:, hs]
        z = jnp.sum(jnp.exp(cvals - cvals[0:1, :]), axis=0, keepdims=True)
        chosen = crank_scr[:, hs] < TOPK
        rank1 = rank_scr[:, s1]
        lim = jnp.zeros((N_KEYS, tbk), F32)
        for a in range(TOPK):
            count = jnp.sum(jnp.where(jnp.logical_and(chosen, seg == a), 1.0, 0.0), axis=0, keepdims=True)
            lim = jnp.where(rank1 == a, count, lim)
        lim_ref[h] = lim
        r2_ref[h] = rank_scr[:, s2].astype(F32).astype(r2_ref.dtype)
        e1_ref[h] = jnp.exp(s_scr[:, s1] - vals_scr[0:1, s1]) / z
        e2_ref[h] = jnp.exp(s_scr[:, s2] - vals_scr[0:1, s2]).astype(e2_ref.dtype)


def peer_topk(q, keys, *, tbk=128):
    m = q.shape[0]
    out = pl.BlockSpec((PEER_HEADS, N_KEYS, tbk), lambda i: (0, 0, i))
    shape = lambda dt: jax.ShapeDtypeStruct((PEER_HEADS, N_KEYS, m), dt)
    wide = 2 * PEER_HEADS * tbk
    return pl.pallas_call(
        functools.partial(_peer_topk_kernel, tbk=tbk),
        grid=(m // tbk,),
        in_specs=[pl.BlockSpec((tbk, PEER_HEADS * 2 * PEER_DH), lambda i: (i, 0)),
                  pl.BlockSpec((PEER_HEADS, 2, N_KEYS, PEER_DH), lambda i: (0, 0, 0, 0))],
        out_specs=[out] * 4,
        out_shape=[shape(BF16), shape(F32), shape(F32), shape(BF16)],
        scratch_shapes=[pltpu.VMEM((N_KEYS, wide), F32), pltpu.VMEM((N_KEYS, wide), F32),
                        pltpu.VMEM((N_KEYS, wide), jnp.int32), pltpu.VMEM((TOPK, wide), F32),
                        pltpu.VMEM((_CAND_ROWS, PEER_HEADS * tbk), F32),
                        pltpu.VMEM((_CAND_ROWS, PEER_HEADS * tbk), jnp.int32),
                        pltpu.VMEM((TOPK, PEER_HEADS * tbk), F32)],
        compiler_params=_params(("parallel",)),
        name="peer_topk",
    )(q, keys)


PEER_SUB = 2 * N_KEYS


def _peer_dense_kernel(xt_ref, r2_ref, lim_ref, e1_ref, e2_ref, u_ref, v_ref, o_ref, coef_a, coef_b, ht_scr,
                       *, eb, n_blocks):
    e = pl.program_id(1)

    @pl.when(e == 0)
    def _():
        o_ref[...] = jnp.zeros_like(o_ref)
        coef_b[...] = jnp.zeros_like(coef_b)

    blk = jnp.minimum(e, n_blocks - 1)

    n_sub = eb // PEER_SUB
    d_sub = o_ref.shape[1] // n_sub

    def hidden(s):
        ht_scr[s % 2] = jnp.dot(u_ref[s * PEER_SUB:(s + 1) * PEER_SUB, :], xt_ref[...],
                                preferred_element_type=F32)

    def step(prev_ref, next_ref):
        hidden(0)
        for s in range(n_sub):
            ht = ht_scr.at[s % 2]
            cols = slice(s * d_sub, (s + 1) * d_sub)
            o_ref[:, cols] += lax.dot_general(prev_ref[...], v_ref[:, cols], (((0,), (0,)), ((), ())),
                                              preferred_element_type=F32)
            if s + 1 < n_sub:
                hidden(s + 1)
            for ii in range(PEER_SUB // N_KEYS):
                i = blk * (eb // N_KEYS) + s * (PEER_SUB // N_KEYS) + ii
                row = s * PEER_SUB + ii * N_KEYS
                gate = None
                for h in range(PEER_HEADS):
                    lim = lim_ref[h, pl.ds(i, 1), :].astype(BF16)
                    e1 = e1_ref[h, pl.ds(i, 1), :].astype(BF16)
                    term = jnp.where(r2_ref[h] < lim, e2_ref[h] * e1, jnp.zeros((), BF16))
                    gate = term if gate is None else gate + term
                next_ref[row:row + N_KEYS, :] = gate * _gelu_to_bf16(ht[ii * N_KEYS:(ii + 1) * N_KEYS, :])

    @pl.when(e % 2 == 0)
    def _():
        step(coef_b, coef_a)

    @pl.when(e % 2 == 1)
    def _():
        step(coef_a, coef_b)


def peer_dense(xt, r2, lim, e1, e2, u_bf, v_bf, *, tbl, eb=1024):
    d, m = xt.shape
    n_blocks = u_bf.shape[0] // eb
    head = pl.BlockSpec((PEER_HEADS, N_KEYS, tbl), lambda t, e: (0, 0, t))
    return pl.pallas_call(
        functools.partial(_peer_dense_kernel, eb=eb, n_blocks=n_blocks),
        grid=(m // tbl, n_blocks + 1),
        in_specs=[pl.BlockSpec((d, tbl), lambda t, e: (0, t)), head, head, head, head,
                  pl.BlockSpec((eb, d), lambda t, e: (jnp.minimum(e, n_blocks - 1), 0)),
                  pl.BlockSpec((eb, d), lambda t, e: (jnp.maximum(e - 1, 0), 0))],
        out_specs=pl.BlockSpec((tbl, d), lambda t, e: (t, 0)),
        out_shape=jax.ShapeDtypeStruct((m, d), F32),
        scratch_shapes=[pltpu.VMEM((eb, tbl), BF16), pltpu.VMEM((eb, tbl), BF16),
                        pltpu.VMEM((2, PEER_SUB, tbl), F32)],
        compiler_params=_params(("parallel", "arbitrary")),
        name="peer_dense",
    )(xt, r2, lim, e1, e2, u_bf, v_bf)


def _final_kernel(h_ref, p_ref, g_ref, o_ref):
    x = h_ref[...] + p_ref[...]
    o_ref[...] = x * lax.rsqrt(jnp.mean(x * x, axis=-1, keepdims=True) + RMS_EPS) * g_ref[...]


def final_norm(h, peer, g, row0, n_rows, *, tb):
    d = h.shape[1]
    off = row0 // tb
    rows = pl.BlockSpec((tb, d), lambda i: (i + off, 0))
    return pl.pallas_call(
        _final_kernel,
        grid=(n_rows // tb,),
        in_specs=[rows, rows, pl.BlockSpec((1, d), lambda i: (0, 0))],
        out_specs=pl.BlockSpec((tb, d), lambda i: (i, 0)),
        out_shape=jax.ShapeDtypeStruct((n_rows, d), F32),
        compiler_params=_params(("parallel",)),
        name="final_norm",
    )(h, peer, g.reshape(1, d))


def kernel(x_prompt, x_sample, mem_prompt, state_shift, state_wkv, cache_mem_k, cache_mem_v, ln1_g, w_in, gmlp_ln_g, gmlp_ln_b, gmlp_ws, gmlp_bs, rwkv_mu, rwkv_w0, rwkv_w2, rwkv_a0, rwkv_a2, rwkv_g2, rwkv_kk, rwkv_ka, rwkv_rk, rwkv_lnx_g, rwkv_lnx_b, mem_norm_g, w_mem_kv, w_up_g, w_up_r, w_up_x, w_out, ln2_g, peer_wq, peer_keys, peer_u, peer_v, final_g):
    depth = w_in.shape[0]
    assert depth == 1, "single-layer step"
    l = 0
    n_batch, seq_len, d = x_prompt.shape
    n_dec = x_sample.shape[0]
    n_prompt = n_batch * seq_len
    m = n_prompt + n_dec
    tb_small = _largest_divisor(m, (640, 128))
    tb_big = _largest_divisor(m, (1664, 640, 128))

    x = jnp.concatenate([x_prompt.reshape(n_prompt, d), x_sample.reshape(n_dec, d)], axis=0)
    xn = rmsnorm(x, ln1_g[l], tb=tb_small, out_dtype=BF16)
    proj = functools.partial(matmul, xn, w_in[l], tb=tb_big, nb=512)
    z_g = proj(col_off=0, n_cols=GMLP_COLS, name="proj_gmlp")
    z_r = proj(col_off=GMLP_COLS, n_cols=RWKV_COLS, name="proj_rwkv")
    z_q = proj(col_off=GMLP_COLS + RWKV_COLS, n_cols=XA_D, name="proj_xattn")
    z_gate = proj(col_off=GMLP_COLS + RWKV_COLS + XA_D, n_cols=GATE_COLS, name="proj_gate")

    memn = rmsnorm(mem_prompt.reshape(n_batch * N_MEM, d), mem_norm_g[l], tb=256, out_dtype=BF16)
    kv = matmul(memn, w_mem_kv[l], tb=N_MEM, nb=512, name="proj_mem_kv")
    p_mk = kv[:, :XA_D].reshape(1, n_batch, N_MEM, XA_HEADS, XA_HD)
    p_mv = kv[:, XA_D:].reshape(1, n_batch, N_MEM, XA_HEADS, XA_HD)

    row = lambda a: a.reshape(1, -1)
    zeros_lora = jnp.zeros((LORA_W, RWKV_D), F32)
    w2p = jnp.concatenate([rwkv_w2[l], zeros_lora], axis=0).astype(BF16)
    a2p = jnp.concatenate([zeros_lora, rwkv_a2[l]], axis=0).astype(BF16)
    rw = (row(rwkv_mu[l]), row(rwkv_w0[l]), w2p, row(rwkv_a0[l]), a2p, rwkv_g2[l].astype(BF16),
          row(rwkv_kk[l]), row(rwkv_ka[l]), row(rwkv_rk[l]))

    o_g = gmlp_prompt(z_g, jnp.zeros((m, GMLP_D), BF16), n_prompt, gmlp_ln_g[l], gmlp_ln_b[l], gmlp_ws[l],
                      gmlp_bs[l])
    prep = rwkv_prep(z_r, n_prompt, seq_len, *rw)
    o_r, st = rwkv_chunks(prep, jnp.zeros((HEAD_PAIRS, m, V7X_LANES), BF16), n_batch, seq_len,
                          rwkv_lnx_g[l], rwkv_lnx_b[l])
    o_x = xattn_prompt(z_q, kv, jnp.zeros((m, XA_D), BF16), n_batch, seq_len)

    gw = jnp.repeat(gmlp_ws[l][:, 0, 0], CHUNK).reshape(1, GMLP_D)
    gb = jnp.repeat(gmlp_bs[l][:, 0], CHUNK).reshape(1, GMLP_D)
    o_g, s_gv, wt, kkt, bt, kt, rt, vt, bonus_s, g_s = sample_mix(
        z_g, z_r, n_prompt, state_shift[l], o_g, gmlp_ln_g[l], gmlp_ln_b[l], gw, gb, *rw)
    s_wkv, y_s = sample_state(state_wkv[l], wt, kkt, bt, kt, rt, vt)
    o_r = sample_rwkv_out(y_s.reshape(RWKV_D, n_dec), bonus_s, g_s, rwkv_lnx_g[l], rwkv_lnx_b[l], o_r, n_prompt)
    o_x = xattn_sample(z_q, cache_mem_k[l].reshape(n_dec, N_MEM * XA_HEADS, XA_HD),
                       cache_mem_v[l].reshape(n_dec, N_MEM * XA_HEADS, XA_HD), o_x, n_prompt)

    merged = merge(o_g, o_r, o_x, z_gate, w_up_g[l], w_up_r[l], w_up_x[l], tb=tb_small)
    h = matmul(merged, w_out[l], tb=tb_big, nb=512, residual=x, name="proj_out")

    hn, hn_t = rmsnorm(h, ln2_g[l], tb=tb_small, out_dtype=BF16, transposed=True)
    q = matmul(hn, peer_wq[l], tb=tb_big, nb=512, name="proj_peer_q")
    r2, lim, e1, e2 = peer_topk(q, peer_keys[l])
    peer = peer_dense(hn_t, r2, lim, e1, e2, peer_u[l].astype(BF16), peer_v[l].astype(BF16), tbl=tb_small)
    y_prompt = final_norm(h, peer, final_g, 0, n_prompt, tb=512)
    y_sample = final_norm(h, peer, final_g, n_prompt, n_dec, tb=n_dec)

    st = st.reshape(n_batch, HEAD_PAIRS, 2, RWKV_HD, 2, RWKV_HD)
    p_wkv = jnp.stack([st[:, :, 0, :, 0, :], st[:, :, 1, :, 1, :]], axis=2)
    p_wkv = jnp.swapaxes(p_wkv, -1, -2).reshape(1, n_batch, RWKV_HEADS, RWKV_HD, RWKV_HD)
    p_shift = z_r[seq_len - 1:n_prompt:seq_len][None]
    s_shift = z_r[n_prompt:][None]
    s_wkv = s_wkv.reshape(1, n_dec, RWKV_HEADS, RWKV_HD, RWKV_HD)
    return (y_prompt.reshape(n_batch, seq_len, d), y_sample.reshape(n_dec, 1, d), p_mk, p_mv,
            p_shift, p_wkv, s_shift, s_wkv, s_gv.reshape(1, n_dec, 1, GMLP_D))
```

```python
import functools

import jax
import jax.numpy as jnp
from jax import lax
from jax.experimental import pallas as pl
from jax.experimental.pallas import tpu as pltpu

F32 = jnp.float32
BF16 = jnp.bfloat16

D_MODEL = 2048
CHUNK = 128
GMLP_GROUPS = 6
GMLP_D = 768
RWKV_HEADS = 12
RWKV_HD = 64
RWKV_D = 768
LORA_W = 64
LORA_A = 64
LORA_G = 128
RWKV_COLS = 2560
XA_HEADS = 4
XA_HD = 128
XA_D = 512
N_MEM = 256
GMLP_COLS = 2 * GMLP_D
GATE_COLS = 3 * D_MODEL
PEER_HEADS = 8
N_KEYS = 128
PEER_DH = 128
TOPK = 16
RMS_EPS = 1e-6
LN_EPS = 1e-5
GN_EPS = 64e-5

V7X_LANES = 128
V7X_VMEM_LIMIT_BYTES = 56 * 1024 * 1024

HEAD_PAIRS = RWKV_HEADS // 2
RWKV_CHUNK = 64
RWKV_CHUNKS_PER_STEP = 16
NEG_INF = float("-inf")


def _params(semantics):
    return pltpu.CompilerParams(dimension_semantics=semantics, vmem_limit_bytes=V7X_VMEM_LIMIT_BYTES)


def _largest_divisor(m, candidates):
    return next(c for c in candidates if m % c == 0)


def _gelu(x):
    return 0.5 * x * (1.0 + jnp.tanh(0.7978845608028654 * (x + 0.044715 * (x * x * x))))


def _gelu_to_bf16(x):
    inner = (0.7978845608028654 * x) * (1.0 + 0.044715 * (x * x))
    xb = x.astype(BF16)
    return (0.5 * xb) * (1.0 + jnp.tanh(inner.astype(BF16)))


def _split_bf16(x):
    hi = x.astype(BF16)
    return hi, (x - hi.astype(F32)).astype(BF16)


def _split3_bf16(x):
    hi = x.astype(BF16)
    rest = x - hi.astype(F32)
    mid = rest.astype(BF16)
    return hi, mid, (rest - mid.astype(F32)).astype(BF16)


def _sigmoid(x):
    return 1.0 / (1.0 + jnp.exp(-x))


def _rmsnorm_kernel(x_ref, g_ref, *o_refs, transposed):
    x = x_ref[...]
    y = x * lax.rsqrt(jnp.mean(x * x, axis=-1, keepdims=True) + RMS_EPS) * g_ref[...]
    o_refs[0][...] = y.astype(o_refs[0].dtype)
    if transposed:
        o_refs[1][...] = y.T.astype(o_refs[1].dtype)


def rmsnorm(x, g, *, tb, out_dtype, transposed=False):
    m, d = x.shape
    out_shape = [jax.ShapeDtypeStruct((m, d), out_dtype)]
    out_specs = [pl.BlockSpec((tb, d), lambda i: (i, 0))]
    if transposed:
        out_shape.append(jax.ShapeDtypeStruct((d, m), out_dtype))
        out_specs.append(pl.BlockSpec((d, tb), lambda i: (0, i)))
    res = pl.pallas_call(
        functools.partial(_rmsnorm_kernel, transposed=transposed),
        grid=(m // tb,),
        in_specs=[pl.BlockSpec((tb, d), lambda i: (i, 0)), pl.BlockSpec((1, d), lambda i: (0, 0))],
        out_specs=out_specs,
        out_shape=out_shape,
        compiler_params=_params(("parallel",)),
        name="rmsnorm_t" if transposed else "rmsnorm",
    )(x, g.reshape(1, d))
    return res if transposed else res[0]


def _matmul_kernel(a_ref, w_ref, *rest, has_residual):
    if has_residual:
        r_ref, o_ref, wbf_ref = rest
    else:
        o_ref, wbf_ref = rest

    @pl.when(pl.program_id(1) == 0)
    def _():
        wbf_ref[...] = w_ref[...].astype(BF16)

    acc = jnp.dot(a_ref[...], wbf_ref[...], preferred_element_type=F32)
    if has_residual:
        acc = acc + r_ref[...]
    o_ref[...] = acc.astype(o_ref.dtype)


def matmul(a, w, *, tb, nb, col_off=0, n_cols=None, residual=None, out_dtype=F32, name="matmul"):
    m, k = a.shape
    n = w.shape[1] if n_cols is None else n_cols
    off = col_off // nb
    in_specs = [pl.BlockSpec((tb, k), lambda j, i: (i, 0)),
                pl.BlockSpec((k, nb), lambda j, i: (0, j + off))]
    args = [a, w]
    if residual is not None:
        in_specs.append(pl.BlockSpec((tb, nb), lambda j, i: (i, j)))
        args.append(residual)
    return pl.pallas_call(
        functools.partial(_matmul_kernel, has_residual=residual is not None),
        grid=(n // nb, m // tb),
        in_specs=in_specs,
        out_specs=pl.BlockSpec((tb, nb), lambda j, i: (i, j)),
        out_shape=jax.ShapeDtypeStruct((m, n), out_dtype),
        scratch_shapes=[pltpu.VMEM((k, nb), BF16)],
        compiler_params=_params(("arbitrary", "arbitrary")),
        name=name,
    )(*args)


def _layernorm(v, g, b):
    vc = v - jnp.mean(v, axis=-1, keepdims=True)
    var = jnp.mean(vc * vc, axis=-1, keepdims=True)
    return vc * lax.rsqrt(var + LN_EPS) * g + b


def _gmlp_kernel(z_ref, lng_ref, lnb_ref, ws_ref, bst_ref, buf_any, o_ref):
    del buf_any
    ge = _gelu(z_ref[...])
    u = ge[:, :GMLP_D]
    v = _layernorm(ge[:, GMLP_D:], lng_ref[...], lnb_ref[...])
    row = lax.broadcasted_iota(jnp.int32, (CHUNK, CHUNK), 0)
    col = lax.broadcasted_iota(jnp.int32, (CHUNK, CHUNK), 1)
    causal = col <= row
    for g in range(GMLP_GROUPS):
        sl = slice(g * CHUNK, (g + 1) * CHUNK)
        wm = jnp.where(causal, ws_ref[g], 0.0).astype(BF16)
        mixed = jnp.dot(wm, v[:, sl].astype(BF16), preferred_element_type=F32) + bst_ref[:, g:g + 1]
        o_ref[:, sl] = (u[:, sl] * mixed).astype(o_ref.dtype)


def gmlp_prompt(z_g, buf, n_tokens, ln_g, ln_b, ws, bs):
    return pl.pallas_call(
        _gmlp_kernel,
        grid=(n_tokens // CHUNK,),
        in_specs=[pl.BlockSpec((CHUNK, GMLP_COLS), lambda i: (i, 0)),
                  pl.BlockSpec((1, GMLP_D), lambda i: (0, 0)),
                  pl.BlockSpec((1, GMLP_D), lambda i: (0, 0)),
                  pl.BlockSpec((GMLP_GROUPS, CHUNK, CHUNK), lambda i: (0, 0, 0)),
                  pl.BlockSpec((CHUNK, GMLP_GROUPS), lambda i: (0, 0)),
                  pl.BlockSpec(memory_space=pl.ANY)],
        out_specs=pl.BlockSpec((CHUNK, GMLP_D), lambda i: (i, 0)),
        out_shape=jax.ShapeDtypeStruct(buf.shape, buf.dtype),
        input_output_aliases={5: 0},
        compiler_params=_params(("parallel",)),
        name="gmlp_prompt",
    )(z_g, ln_g.reshape(1, GMLP_D), ln_b.reshape(1, GMLP_D), ws, bs.T, buf)


def _head_block_ones():
    r = lax.broadcasted_iota(jnp.int32, (V7X_LANES, V7X_LANES), 0) // RWKV_HD
    c = lax.broadcasted_iota(jnp.int32, (V7X_LANES, V7X_LANES), 1) // RWKV_HD
    return (r == c).astype(F32)


def _head_sum(x, ones_bd):
    ones = ones_bd.astype(BF16)
    return sum(jnp.dot(t, ones, preferred_element_type=F32) for t in _split3_bf16(x))


def _rwkv_mix(zs, w0, w2p, a0, a2p, g2, kkp, ka, rk, ones_bd):
    r = zs[:, 0:RWKV_D]
    k = zs[:, RWKV_D:2 * RWKV_D]
    v = zs[:, 2 * RWKV_D:3 * RWKV_D]
    lwa = zs[:, 3 * RWKV_D:3 * RWKV_D + LORA_W + LORA_A]
    lg = zs[:, 3 * RWKV_D + LORA_W + LORA_A:]
    lora_w = jnp.dot(jnp.tanh(lwa).astype(BF16), w2p, preferred_element_type=F32)
    lora_a = jnp.dot(lwa.astype(BF16), a2p, preferred_element_type=F32)
    x = -(w0 + lora_w)
    softplus = jnp.maximum(x, 0.0) + jnp.log(1.0 + jnp.exp(-jnp.abs(x)))
    logw = -jnp.exp(-softplus - 0.5)
    a = _sigmoid(a0 + lora_a)
    g = jnp.dot(_sigmoid(lg).astype(BF16), g2, preferred_element_type=F32)
    kk = k * kkp
    k2 = k * (1.0 + (a - 1.0) * ka)
    rkk = r * k2 * rk
    out = []
    for p in range(HEAD_PAIRS):
        sl = slice(p * V7X_LANES, (p + 1) * V7X_LANES)
        kk_p = kk[:, sl]
        norm = jnp.sqrt(_head_sum(kk_p * kk_p, ones_bd))
        kkn = kk_p / jnp.maximum(norm, 1e-12)
        bonus = _head_sum(rkk[:, sl], ones_bd) * v[:, sl]
        out.append((r[:, sl], k2[:, sl], v[:, sl], logw[:, sl], kkn, a[:, sl], g[:, sl], bonus))
    return out


def _rwkv_prep_kernel(z_ref, zp_ref, mu_ref, w0_ref, w2_ref, a0_ref, a2_ref, g2_ref, kkp_ref, ka_ref, rk_ref,
                      at_ref, btp_ref, ktp_ref, rt_ref, v_ref, g_ref, bonus_ref, w_ref, pc_ref, cs_scr,
                      *, tb, blocks_per_seq):
    i = pl.program_id(0)
    z = z_ref[...]
    first = (i % blocks_per_seq) == 0
    prev_row = jnp.where(first, 0.0, zp_ref[7:8, :])
    row = lax.broadcasted_iota(jnp.int32, (tb, 1), 0)
    zprev = jnp.where(row == 0, prev_row, pltpu.roll(z, 1, axis=0))
    zs = z + (zprev - z) * mu_ref[...]
    ones_bd = _head_block_ones()
    mixed = _rwkv_mix(zs, w0_ref[...], w2_ref[...], a0_ref[...], a2_ref[...], g2_ref[...], kkp_ref[...],
                      ka_ref[...], rk_ref[...], ones_bd)
    n_chunks = tb // RWKV_CHUNK
    shape3 = (n_chunks, RWKV_CHUNK, V7X_LANES)
    tr = lax.broadcasted_iota(jnp.int32, (n_chunks, RWKV_CHUNK, RWKV_CHUNK), 1)
    tc = lax.broadcasted_iota(jnp.int32, (n_chunks, RWKV_CHUNK, RWKV_CHUNK), 2)
    tri = (tc <= tr).astype(BF16)
    bdot = lambda x: lax.dot_general(tri, x, (((2,), (1,)), ((0,), (0,))), preferred_element_type=F32)
    for p, (r, k2, v, logw, kkn, a, g, bonus) in enumerate(mixed):
        hi, mid, lo = _split3_bf16(logw.reshape(shape3))
        cs3 = bdot(hi) + bdot(mid) + bdot(lo)
        total = cs3[:, RWKV_CHUNK - 1:RWKV_CHUNK, :]
        cs = cs3.reshape(tb, V7X_LANES)
        tail = jnp.exp(total - cs3).reshape(tb, V7X_LANES)
        cs_scr[...] = cs
        cs_end = cs_scr[pl.ds(RWKV_CHUNK - 1, n_chunks, stride=RWKV_CHUNK), :]
        inv_p = jnp.exp(-cs)
        at_ref[p] = (jnp.exp(cs - logw) * kkn).astype(BF16)
        btp_ref[p] = (kkn * a * tail).astype(BF16)
        ktp_ref[p] = (k2 * tail).astype(BF16)
        rt_ref[p] = (jnp.exp(cs) * r).astype(BF16)
        v_ref[p] = v.astype(BF16)
        g_ref[p] = g
        bonus_ref[p] = bonus
        w_ref[p] = jnp.concatenate([kkn * a * inv_p, k2 * inv_p], axis=-1).astype(BF16)
        pc_ref[p] = jnp.exp(cs_end)


def rwkv_prep(z_r, n_tokens, seq_len, mu, w0, w2p, a0, a2p, g2, kkp, ka, rk, *, tb=512):
    n_chunks = n_tokens // RWKV_CHUNK
    row_spec = lambda w: pl.BlockSpec((1, w), lambda i: (0, 0))
    pair_out = lambda w: pl.BlockSpec((HEAD_PAIRS, tb, w), lambda i: (0, i, 0))
    pair_shape = lambda w, dt=BF16: jax.ShapeDtypeStruct((HEAD_PAIRS, n_tokens, w), dt)
    return pl.pallas_call(
        functools.partial(_rwkv_prep_kernel, tb=tb, blocks_per_seq=seq_len // tb),
        grid=(n_tokens // tb,),
        in_specs=[pl.BlockSpec((tb, RWKV_COLS), lambda i: (i, 0)),
                  pl.BlockSpec((8, RWKV_COLS), lambda i: (jnp.maximum(i * (tb // 8) - 1, 0), 0)),
                  row_spec(RWKV_COLS), row_spec(RWKV_D),
                  pl.BlockSpec((LORA_W + LORA_A, RWKV_D), lambda i: (0, 0)),
                  row_spec(RWKV_D),
                  pl.BlockSpec((LORA_W + LORA_A, RWKV_D), lambda i: (0, 0)),
                  pl.BlockSpec((LORA_G, RWKV_D), lambda i: (0, 0)),
                  row_spec(RWKV_D), row_spec(RWKV_D), row_spec(RWKV_D)],
        out_specs=[pair_out(V7X_LANES)] * 7 + [pair_out(2 * V7X_LANES),
                   pl.BlockSpec((HEAD_PAIRS, tb // RWKV_CHUNK, V7X_LANES), lambda i: (0, i, 0))],
        out_shape=[pair_shape(V7X_LANES)] * 5 + [pair_shape(V7X_LANES, F32)] * 2 + [pair_shape(2 * V7X_LANES),
                   jax.ShapeDtypeStruct((HEAD_PAIRS, n_chunks, V7X_LANES), F32)],
        scratch_shapes=[pltpu.VMEM((tb, V7X_LANES), F32)],
        compiler_params=_params(("parallel",)),
        name="rwkv_prep",
    )(z_r, z_r, mu, w0, w2p, a0, a2p, g2, kkp, ka, rk)


def _bmm(x, y):
    return lax.dot_general(x.astype(BF16), y.astype(BF16), (((2,), (1,)), ((0,), (0,))),
                           preferred_element_type=F32)


def _bmm_nt(x, y):
    return lax.dot_general(x.astype(BF16), y.astype(BF16), (((2,), (2,)), ((0,), (0,))),
                           preferred_element_type=F32)


def _mm_tn(x, y, precision=None):
    return lax.dot_general(x, y, (((0,), (0,)), ((), ())), preferred_element_type=F32, precision=precision)


def _group_norm_gate(y, bonus, g, lng, lnb, ones_bd):
    mean = _head_sum(y, ones_bd) * (1.0 / RWKV_HD)
    yc = y - mean
    var = _head_sum(yc * yc, ones_bd) * (1.0 / RWKV_HD)
    return (yc * lax.rsqrt(var + GN_EPS) * lng + lnb + bonus) * g


def _rwkv_chunk_kernel(at_ref, btp_ref, ktp_ref, rt_ref, v_ref, g_ref, bonus_ref, w_ref, pc_ref, lng_ref,
                       lnb_ref, buf_any, o_ref, st_ref, st_scr):
    del buf_any
    nc, c, ln = RWKV_CHUNKS_PER_STEP, RWKV_CHUNK, V7X_LANES
    t = pl.program_id(1)

    @pl.when(t == 0)
    def _():
        st_scr[...] = jnp.zeros_like(st_scr)

    shape3 = (nc, c, ln)
    at = at_ref[0].reshape(shape3)
    rt = rt_ref[0].reshape(shape3)
    v = v_ref[0].reshape(shape3)
    btk = w_ref[0].reshape(nc, c, 2 * ln)
    bt, kt = btk[..., :ln], btk[..., ln:]
    lane = lax.broadcasted_iota(jnp.int32, (1, 1, ln), 2)
    head0 = lane < RWKV_HD
    head0_2 = jnp.concatenate([head0, head0], axis=-1)
    zero = jnp.zeros(shape3, F32)

    lhs = jnp.concatenate([jnp.where(head0, at, 0.0), jnp.where(head0, rt, 0.0),
                           jnp.where(head0, 0.0, at), jnp.where(head0, 0.0, rt)], axis=1)
    gram = _bmm_nt(lhs, jnp.concatenate([bt, kt], axis=1))
    gr = lax.broadcasted_iota(jnp.int32, (1, 4 * c, 2 * c), 1)
    gc = lax.broadcasted_iota(jnp.int32, (1, 4 * c, 2 * c), 2) % c
    causal = gr % c + (gr // c) % 2 > gc
    gram = jnp.where(causal, gram, 0.0)
    zv = jnp.concatenate([jnp.zeros(shape3, BF16), v], axis=1)
    first = lax.broadcasted_iota(jnp.int32, (1, 1, 2 * c), 2) < c

    xs, mvs, mwus = [], [], []
    for h in range(2):
        g_h = gram[:, 2 * h * c:(2 * h + 2) * c, :]
        lmv = _bmm(g_h, zv)
        x = jnp.concatenate([at.astype(F32), lmv[:, :c, :]], axis=-1)
        lp = jnp.where(first, -g_h[:, :c, :], 0.0)
        pad2 = jnp.zeros((nc, c, 2 * ln), F32)
        x = x + _bmm(lp, jnp.concatenate([x, pad2], axis=1))
        n = 2
        while n < c:
            lp = _bmm(lp, jnp.concatenate([lp, zero], axis=1))
            x = x + _bmm(lp, jnp.concatenate([x, pad2], axis=1))
            n *= 2
        xs.append(x)
        mvs.append(lmv[:, c:, :])
        mrb = jnp.where(first, g_h[:, c:, :], 0.0)
        mwus.append(_bmm(mrb, jnp.concatenate([x, pad2], axis=1)))
    x = jnp.where(head0_2, xs[0], xs[1])
    mv = jnp.where(head0, mvs[0], mvs[1])
    mwu = jnp.where(head0_2, mwus[0], mwus[1])
    qe = (rt.astype(F32) - mwu[..., :ln]).astype(BF16)
    yl = mv - mwu[..., ln:]

    rr = lax.broadcasted_iota(jnp.int32, (ln, ln), 0)
    cc = lax.broadcasted_iota(jnp.int32, (ln, ln), 1)
    same_head = (rr // RWKV_HD) == (cc // RWKV_HD)
    eye = rr == cc
    btp = btp_ref[0].reshape(shape3).astype(BF16)
    ktp = ktp_ref[0].reshape(shape3).astype(BF16)
    x_bf = x.astype(BF16)
    v_bf = v.astype(BF16)
    pc = pc_ref[0]
    st = st_scr[...]
    ys = []
    for j in range(nc):
        bwu = _mm_tn(btp[j], x_bf[j])
        kv = _mm_tn(ktp[j], v_bf[j])
        tr = jnp.where(same_head, jnp.where(eye, pc[j:j + 1, :], 0.0) - bwu[:, :ln], 0.0)
        ad = jnp.where(same_head, kv - bwu[:, ln:], 0.0)
        st_hi = st.astype(BF16)
        st_lo = (st - st_hi.astype(F32)).astype(BF16)
        ys.append(jnp.dot(qe[j], st_hi, preferred_element_type=F32) + yl[j])
        tr_bf = tr.astype(BF16)
        st = (jnp.dot(tr_bf, st_hi, preferred_element_type=F32)
              + jnp.dot(tr_bf, st_lo, preferred_element_type=F32) + ad)
    st_scr[...] = st
    st_ref[0] = st
    y = jnp.concatenate(ys, axis=0)
    o = _group_norm_gate(y, bonus_ref[0], g_ref[0], lng_ref[0], lnb_ref[0], _head_block_ones())
    o_ref[0] = o.astype(o_ref.dtype)


def rwkv_chunks(prep, buf, n_batch, seq_len, lnx_g, lnx_b):
    at, btp, ktp, rt, v, g, bonus, wbk, pc = prep
    rows = RWKV_CHUNK * RWKV_CHUNKS_PER_STEP
    steps = seq_len // rows
    n_tokens = n_batch * seq_len
    tok = lambda w: pl.BlockSpec((1, rows, w), lambda bp, t: (bp % HEAD_PAIRS, (bp // HEAD_PAIRS) * steps + t, 0))
    pair_row = pl.BlockSpec((1, 1, V7X_LANES), lambda bp, t: (bp % HEAD_PAIRS, 0, 0))
    return pl.pallas_call(
        _rwkv_chunk_kernel,
        grid=(n_batch * HEAD_PAIRS, steps),
        in_specs=[tok(V7X_LANES)] * 7 + [tok(2 * V7X_LANES),
                  pl.BlockSpec((1, RWKV_CHUNKS_PER_STEP, V7X_LANES),
                               lambda bp, t: (bp % HEAD_PAIRS, (bp // HEAD_PAIRS) * steps + t, 0)),
                  pair_row, pair_row, pl.BlockSpec(memory_space=pl.ANY)],
        out_specs=[tok(V7X_LANES),
                   pl.BlockSpec((1, V7X_LANES, V7X_LANES), lambda bp, t: (bp, 0, 0))],
        out_shape=[jax.ShapeDtypeStruct(buf.shape, buf.dtype),
                   jax.ShapeDtypeStruct((n_batch * HEAD_PAIRS, V7X_LANES, V7X_LANES), F32)],
        input_output_aliases={11: 0},
        scratch_shapes=[pltpu.VMEM((V7X_LANES, V7X_LANES), F32)],
        compiler_params=_params(("parallel", "arbitrary")),
        name="rwkv_chunks",
    )(at, btp, ktp, rt, v, g, bonus, wbk, pc,
      lnx_g.reshape(HEAD_PAIRS, 1, V7X_LANES), lnx_b.reshape(HEAD_PAIRS, 1, V7X_LANES), buf)


def _softmax_rows(s):
    e = jnp.exp(s - jnp.max(s, axis=-1, keepdims=True))
    return e / jnp.sum(e, axis=-1, keepdims=True)


def _xattn_kernel(q_ref, k_ref, v_ref, buf_any, o_ref):
    del buf_any
    for h in range(XA_HEADS):
        sl = slice(h * XA_HD, (h + 1) * XA_HD)
        s = lax.dot_general(q_ref[:, sl].astype(BF16), k_ref[:, sl].astype(BF16), (((1,), (1,)), ((), ())),
                            preferred_element_type=F32) * (XA_HD ** -0.5)
        pr = _softmax_rows(s)
        o_ref[:, sl] = jnp.dot(pr.astype(BF16), v_ref[:, sl].astype(BF16),
                               preferred_element_type=F32).astype(o_ref.dtype)


def xattn_prompt(z_q, kv, buf, n_batch, seq_len, *, tq=512):
    steps = seq_len // tq
    return pl.pallas_call(
        _xattn_kernel,
        grid=(n_batch, steps),
        in_specs=[pl.BlockSpec((tq, XA_D), lambda b, t: (b * steps + t, 0)),
                  pl.BlockSpec((N_MEM, XA_D), lambda b, t: (b, 0)),
                  pl.BlockSpec((N_MEM, XA_D), lambda b, t: (b, 1)),
                  pl.BlockSpec(memory_space=pl.ANY)],
        out_specs=pl.BlockSpec((tq, XA_D), lambda b, t: (b * steps + t, 0)),
        out_shape=jax.ShapeDtypeStruct(buf.shape, buf.dtype),
        input_output_aliases={3: 0},
        compiler_params=_params(("parallel", "parallel")),
        name="xattn_prompt",
    )(z_q, kv, kv, buf)


def _sample_mix_kernel(zg_ref, zr_ref, shift_ref, lng_ref, lnb_ref, gw_ref, gb_ref,
                       mu_ref, w0_ref, w2_ref, a0_ref, a2_ref, g2_ref, kkp_ref, ka_ref, rk_ref, og_any,
                       og_ref, gv_ref, wt_ref, kkt_ref, bt_ref, kt_ref, rt_ref, vt_ref, bonus_ref, g_ref):
    del og_any
    ge = _gelu(zg_ref[...])
    v = _layernorm(ge[:, GMLP_D:], lng_ref[...], lnb_ref[...])
    gv_ref[...] = v
    og_ref[...] = (ge[:, :GMLP_D] * (v * gw_ref[...] + gb_ref[...])).astype(og_ref.dtype)

    z = zr_ref[...]
    zs = z + (shift_ref[...] - z) * mu_ref[...]
    mixed = _rwkv_mix(zs, w0_ref[...], w2_ref[...], a0_ref[...], a2_ref[...], g2_ref[...], kkp_ref[...],
                      ka_ref[...], rk_ref[...], _head_block_ones())
    for p, (r, k2, vv, logw, kkn, a, g, bonus) in enumerate(mixed):
        sl = slice(p * V7X_LANES, (p + 1) * V7X_LANES)
        wt_ref[sl, :] = jnp.exp(logw).T
        kkt_ref[sl, :] = kkn.T
        bt_ref[sl, :] = (kkn * a).T
        kt_ref[sl, :] = k2.T
        rt_ref[sl, :] = r.T
        vt_ref[sl, :] = vv.T
        bonus_ref[:, sl] = bonus
        g_ref[:, sl] = g


def sample_mix(z_g, z_r, row0, shift, o_g, ln_g, ln_b, gw, gb, mu, w0, w2p, a0, a2p, g2, kkp, ka, rk):
    n = shift.shape[0]
    blk = row0 // n
    tok = lambda w: pl.BlockSpec((n, w), lambda i: (blk, 0))
    loc = lambda w: pl.BlockSpec((n, w), lambda i: (0, 0))
    row = lambda w: pl.BlockSpec((1, w), lambda i: (0, 0))
    full2 = lambda a, b: pl.BlockSpec((a, b), lambda i: (0, 0))
    chan = jax.ShapeDtypeStruct((RWKV_D, n), F32)
    return pl.pallas_call(
        _sample_mix_kernel,
        grid=(1,),
        in_specs=[tok(GMLP_COLS), tok(RWKV_COLS), loc(RWKV_COLS),
                  row(GMLP_D), row(GMLP_D), row(GMLP_D), row(GMLP_D),
                  row(RWKV_COLS), row(RWKV_D), full2(LORA_W + LORA_A, RWKV_D), row(RWKV_D),
                  full2(LORA_W + LORA_A, RWKV_D), full2(LORA_G, RWKV_D), row(RWKV_D), row(RWKV_D), row(RWKV_D),
                  pl.BlockSpec(memory_space=pl.ANY)],
        out_specs=[tok(GMLP_D), loc(GMLP_D)] + [full2(RWKV_D, n)] * 6 + [loc(RWKV_D), loc(RWKV_D)],
        out_shape=[jax.ShapeDtypeStruct(o_g.shape, o_g.dtype), jax.ShapeDtypeStruct((n, GMLP_D), F32)]
                  + [chan] * 6 + [jax.ShapeDtypeStruct((n, RWKV_D), F32)] * 2,
        input_output_aliases={16: 0},
        compiler_params=_params(("arbitrary",)),
        name="sample_mix",
    )(z_g, z_r, shift, ln_g.reshape(1, -1), ln_b.reshape(1, -1), gw, gb, mu, w0, w2p, a0, a2p, g2, kkp, ka, rk, o_g)


def _sample_state_kernel(s_ref, w_ref, kk_ref, b_ref, k_ref, r_ref, v_ref, snew_ref, y_ref):
    n = s_ref.shape[0]
    st = s_ref[...].T.reshape(RWKV_HD, RWKV_HD, n)
    sa = jnp.sum(st * (-kk_ref[...])[None], axis=1, keepdims=True)
    st_new = st * w_ref[...][None] + sa * b_ref[...][None] + v_ref[...] * k_ref[...][None]
    y_ref[...] = jnp.sum(st_new * r_ref[...][None], axis=1, keepdims=True)
    snew_ref[...] = st_new.reshape(RWKV_HD * RWKV_HD, n).T


def sample_state(s0, wt, kkt, bt, kt, rt, vt):
    n = s0.shape[0]
    hd2 = RWKV_HD * RWKV_HD
    state = pl.BlockSpec((n, hd2), lambda h: (0, h))
    keyvec = pl.BlockSpec((RWKV_HD, n), lambda h: (h, 0))
    valvec = pl.BlockSpec((RWKV_HD, 1, n), lambda h: (h, 0, 0))
    return pl.pallas_call(
        _sample_state_kernel,
        grid=(RWKV_HEADS,),
        in_specs=[state, keyvec, keyvec, keyvec, keyvec, keyvec, valvec],
        out_specs=[state, valvec],
        out_shape=[jax.ShapeDtypeStruct((n, RWKV_HEADS * hd2), F32),
                   jax.ShapeDtypeStruct((RWKV_D, 1, n), F32)],
        compiler_params=_params(("parallel",)),
        name="sample_state",
    )(s0.reshape(n, RWKV_HEADS * hd2), wt, kkt, bt, kt, rt, vt.reshape(RWKV_D, 1, n))


def _sample_rwkv_out_kernel(yt_ref, bonus_ref, g_ref, lng_ref, lnb_ref, or_any, or_ref):
    del or_any
    y = yt_ref[...].T
    ones_bd = _head_block_ones()
    for p in range(HEAD_PAIRS):
        sl = slice(p * V7X_LANES, (p + 1) * V7X_LANES)
        o = _group_norm_gate(y[:, sl], bonus_ref[:, sl], g_ref[:, sl], lng_ref[p], lnb_ref[p], ones_bd)
        or_ref[p] = o.astype(or_ref.dtype)


def sample_rwkv_out(yt, bonus, g, lnx_g, lnx_b, o_r, row0):
    n = bonus.shape[0]
    full2 = lambda a, b: pl.BlockSpec((a, b), lambda i: (0, 0))
    pair_row = pl.BlockSpec((HEAD_PAIRS, 1, V7X_LANES), lambda i: (0, 0, 0))
    return pl.pallas_call(
        _sample_rwkv_out_kernel,
        grid=(1,),
        in_specs=[full2(RWKV_D, n), full2(n, RWKV_D), full2(n, RWKV_D), pair_row, pair_row,
                  pl.BlockSpec(memory_space=pl.ANY)],
        out_specs=pl.BlockSpec((HEAD_PAIRS, n, V7X_LANES), lambda i: (0, row0 // n, 0)),
        out_shape=jax.ShapeDtypeStruct(o_r.shape, o_r.dtype),
        input_output_aliases={5: 0},
        compiler_params=_params(("arbitrary",)),
        name="sample_rwkv_out",
    )(yt, bonus, g, lnx_g.reshape(HEAD_PAIRS, 1, V7X_LANES), lnx_b.reshape(HEAD_PAIRS, 1, V7X_LANES), o_r)


def _xattn_sample_kernel(q_ref, ck_ref, cv_ref, ox_any, ox_ref, *, sb):
    del ox_any
    q = q_ref[...]
    rows = N_MEM * XA_HEADS
    hrow = lax.broadcasted_iota(jnp.int32, (8, rows), 0) % XA_HEADS
    hcol = lax.broadcasted_iota(jnp.int32, (8, rows), 1) % XA_HEADS
    own = hrow == hcol
    h8 = lax.broadcasted_iota(jnp.int32, (8, XA_HD), 0) % XA_HEADS
    nt = (((1,), (1,)), ((), ()))
    outs = []
    for s in range(sb):
        qm = jnp.zeros((8, XA_HD), F32)
        for h in range(XA_HEADS):
            qm = jnp.where(h8 == h, q[s:s + 1, h * XA_HD:(h + 1) * XA_HD], qm)
        sc = lax.dot_general(qm.astype(BF16), ck_ref[s].astype(BF16), nt,
                             preferred_element_type=F32) * (XA_HD ** -0.5)
        pr = _softmax_rows(jnp.where(own, sc, NEG_INF))
        res = jnp.dot(pr.astype(BF16), cv_ref[s].astype(BF16), preferred_element_type=F32)
        outs.append(jnp.concatenate([res[h:h + 1, :] for h in range(XA_HEADS)], axis=1))
    ox_ref[...] = jnp.concatenate(outs, axis=0).astype(ox_ref.dtype)


def xattn_sample(z_q, ck, cv, o_x, row0, *, sb=8):
    n = ck.shape[0]
    off = row0 // sb
    tok = pl.BlockSpec((sb, XA_D), lambda i: (i + off, 0))
    cache = pl.BlockSpec((sb, N_MEM * XA_HEADS, XA_HD), lambda i: (i, 0, 0))
    return pl.pallas_call(
        functools.partial(_xattn_sample_kernel, sb=sb),
        grid=(n // sb,),
        in_specs=[tok, cache, cache, pl.BlockSpec(memory_space=pl.ANY)],
        out_specs=tok,
        out_shape=jax.ShapeDtypeStruct(o_x.shape, o_x.dtype),
        input_output_aliases={3: 0},
        compiler_params=_params(("arbitrary",)),
        name="xattn_sample",
    )(z_q, ck, cv, o_x)


def _merge_kernel(og_ref, or_ref, ox_ref, g0_ref, g1_ref, g2_ref, wg_ref, wr_ref, wx_ref, o_ref,
                  wg_bf, wr_bf, wx_bf):
    @pl.when(pl.program_id(1) == 0)
    def _():
        wg_bf[...] = wg_ref[...].astype(BF16)
        wr_bf[...] = wr_ref[...].astype(BF16)
        wx_bf[...] = wx_ref[...].astype(BF16)

    up_g = jnp.dot(og_ref[...], wg_bf[...], preferred_element_type=F32)
    up_r = jnp.dot(or_ref[0], wr_bf[0:V7X_LANES, :], preferred_element_type=F32)
    for p in range(1, HEAD_PAIRS):
        up_r = up_r + jnp.dot(or_ref[p], wr_bf[p * V7X_LANES:(p + 1) * V7X_LANES, :], preferred_element_type=F32)
    up_x = jnp.dot(ox_ref[...], wx_bf[...], preferred_element_type=F32)
    gate = lambda ref: _sigmoid(ref[...].astype(F32))
    merged = gate(g0_ref) * up_g + gate(g1_ref) * up_r + gate(g2_ref) * up_x
    o_ref[...] = merged.astype(o_ref.dtype)


def merge(o_g, o_r, o_x, z_gate, w_up_g, w_up_r, w_up_x, *, tb, nb=512):
    m = o_g.shape[0]
    nblk = D_MODEL // nb
    gate = lambda b: pl.BlockSpec((tb, nb), lambda j, i: (i, b * nblk + j))
    wspec = lambda k: pl.BlockSpec((k, nb), lambda j, i: (0, j))
    return pl.pallas_call(
        _merge_kernel,
        grid=(nblk, m // tb),
        in_specs=[pl.BlockSpec((tb, GMLP_D), lambda j, i: (i, 0)),
                  pl.BlockSpec((HEAD_PAIRS, tb, V7X_LANES), lambda j, i: (0, i, 0)),
                  pl.BlockSpec((tb, XA_D), lambda j, i: (i, 0)),
                  gate(0), gate(1), gate(2), wspec(GMLP_D), wspec(RWKV_D), wspec(XA_D)],
        out_specs=pl.BlockSpec((tb, nb), lambda j, i: (i, j)),
        out_shape=jax.ShapeDtypeStruct((m, D_MODEL), BF16),
        scratch_shapes=[pltpu.VMEM((GMLP_D, nb), BF16), pltpu.VMEM((RWKV_D, nb), BF16),
                        pltpu.VMEM((XA_D, nb), BF16)],
        compiler_params=_params(("arbitrary", "arbitrary")),
        name="merge",
    )(o_g, o_r, o_x, z_gate, z_gate, z_gate, w_up_g, w_up_r, w_up_x)


def _extract_top(src_ref, work_ref, rank_ref, vals_ref, n_rows):
    width = work_ref.shape[1]
    riota = lax.broadcasted_iota(jnp.int32, (n_rows, V7X_LANES), 0)
    kiota = lax.broadcasted_iota(jnp.int32, (TOPK, V7X_LANES), 0)

    def run(break_ties):
        work_ref[...] = src_ref[...]
        rank_ref[...] = jnp.full(rank_ref.shape, TOPK, jnp.int32)
        vals_ref[...] = jnp.zeros_like(vals_ref)

        def body(p, carry):
            for c in range(width // V7X_LANES):
                sl = slice(c * V7X_LANES, (c + 1) * V7X_LANES)
                w = work_ref[:, sl]
                m = jnp.max(w, axis=0, keepdims=True)
                if break_ties:
                    idx = jnp.min(jnp.where(w == m, riota, n_rows), axis=0, keepdims=True)
                    hit = riota == idx
                else:
                    hit = w == m
                rank_ref[:, sl] = jnp.where(hit, p, rank_ref[:, sl])
                work_ref[:, sl] = jnp.where(hit, NEG_INF, w)
                vals_ref[:, sl] = jnp.where(kiota == p, m, vals_ref[:, sl])
            return carry

        lax.fori_loop(0, TOPK, body, 0)

    run(False)
    picked = jnp.sum(jnp.where(rank_ref[...] < TOPK, 1.0, 0.0), axis=0, keepdims=True)
    tied = jnp.max(picked) > TOPK

    @pl.when(tied)
    def _():
        run(True)


_CAND_COUNT = tuple(TOPK // (a + 1) for a in range(TOPK))
_CAND_START = tuple(sum(_CAND_COUNT[:a]) for a in range(TOPK))
_CAND_ROWS = -(-sum(_CAND_COUNT) // 8) * 8


def _peer_topk_kernel(q_ref, keys_ref, r2_ref, lim_ref, e1_ref, e2_ref,
                      s_scr, work_scr, rank_scr, vals_scr, cand_scr, cwork_scr, crank_scr, cvals_scr, *, tbk):
    nt = (((1,), (1,)), ((), ()))
    for h in range(PEER_HEADS):
        for c in range(2):
            qcol = (2 * h + c) * PEER_DH
            col = (2 * h + c) * tbk
            k_hi, k_lo = _split_bf16(keys_ref[h, c])
            q_hi, q_lo = _split_bf16(q_ref[:, qcol:qcol + PEER_DH])
            sc = (lax.dot_general(k_hi, q_hi, nt, preferred_element_type=F32)
                  + lax.dot_general(k_hi, q_lo, nt, preferred_element_type=F32)
                  + lax.dot_general(k_lo, q_hi, nt, preferred_element_type=F32))
            s_scr[:, col:col + tbk] = sc
    _extract_top(s_scr, work_scr, rank_scr, vals_scr, N_KEYS)

    crow = lax.broadcasted_iota(jnp.int32, (_CAND_ROWS, tbk), 0)
    seg = jnp.full((_CAND_ROWS, tbk), TOPK, jnp.int32)
    for a in reversed(range(TOPK)):
        seg = jnp.where(crow < _CAND_START[a] + _CAND_COUNT[a], jnp.minimum(seg, a), seg)
    pad = jnp.zeros((_CAND_ROWS - TOPK, tbk), F32)
    for h in range(PEER_HEADS):
        v1 = vals_scr[:, (2 * h) * tbk:(2 * h + 1) * tbk]
        v2 = jnp.concatenate([vals_scr[:, (2 * h + 1) * tbk:(2 * h + 2) * tbk], pad], axis=0)
        cand = jnp.full((_CAND_ROWS, tbk), NEG_INF, F32)
        for a in range(TOPK):
            shifted = v2 if _CAND_START[a] == 0 else pltpu.roll(v2, _CAND_START[a], axis=0)
            cand = jnp.where(seg == a, v1[a:a + 1, :] + shifted, cand)
        cand_scr[:, h * tbk:(h + 1) * tbk] = cand
    _extract_top(cand_scr, cwork_scr, crank_scr, cvals_scr, _CAND_ROWS)

    for h in range(PEER_HEADS):
        hs = slice(h * tbk, (h + 1) * tbk)
        s1 = slice((2 * h) * tbk, (2 * h + 1) * tbk)
        s2 = slice((2 * h + 1) * tbk, (2 * h + 2) * tbk)
        cvals = cvals_scr[:, hs]
        z = jnp.sum(jnp.exp(cvals - cvals[0:1, :]), axis=0, keepdims=True)
        chosen = crank_scr[:, hs] < TOPK
        rank1 = rank_scr[:, s1]
        lim = jnp.zeros((N_KEYS, tbk), F32)
        for a in range(TOPK):
            count = jnp.sum(jnp.where(jnp.logical_and(chosen, seg == a), 1.0, 0.0), axis=0, keepdims=True)
            lim = jnp.where(rank1 == a, count, lim)
        lim_ref[h] = lim
        r2_ref[h] = rank_scr[:, s2].astype(F32).astype(r2_ref.dtype)
        e1_ref[h] = jnp.exp(s_scr[:, s1] - vals_scr[0:1, s1]) / z
        e2_ref[h] = jnp.exp(s_scr[:, s2] - vals_scr[0:1, s2]).astype(e2_ref.dtype)


def peer_topk(q, keys, *, tbk=128):
    m = q.shape[0]
    out = pl.BlockSpec((PEER_HEADS, N_KEYS, tbk), lambda i: (0, 0, i))
    shape = lambda dt: jax.ShapeDtypeStruct((PEER_HEADS, N_KEYS, m), dt)
    wide = 2 * PEER_HEADS * tbk
    return pl.pallas_call(
        functools.partial(_peer_topk_kernel, tbk=tbk),
        grid=(m // tbk,),
        in_specs=[pl.BlockSpec((tbk, PEER_HEADS * 2 * PEER_DH), lambda i: (i, 0)),
                  pl.BlockSpec((PEER_HEADS, 2, N_KEYS, PEER_DH), lambda i: (0, 0, 0, 0))],
        out_specs=[out] * 4,
        out_shape=[shape(BF16), shape(F32), shape(F32), shape(BF16)],
        scratch_shapes=[pltpu.VMEM((N_KEYS, wide), F32), pltpu.VMEM((N_KEYS, wide), F32),
                        pltpu.VMEM((N_KEYS, wide), jnp.int32), pltpu.VMEM((TOPK, wide), F32),
                        pltpu.VMEM((_CAND_ROWS, PEER_HEADS * tbk), F32),
                        pltpu.VMEM((_CAND_ROWS, PEER_HEADS * tbk), F32),
                        pltpu.VMEM((_CAND_ROWS, PEER_HEADS * tbk), jnp.int32),
                        pltpu.VMEM((TOPK, PEER_HEADS * tbk), F32)],
        compiler_params=_params(("parallel",)),
        name="peer_topk",
    )(q, keys)


PEER_SUB = 2 * N_KEYS


def _peer_dense_kernel(xt_ref, r2_ref, lim_ref, e1_ref, e2_ref, u_ref, v_ref, o_ref, coef_a, coef_b, ht_scr,
                       *, eb, n_blocks):
    e = pl.program_id(1)

    @pl.when(e == 0)
    def _():
        o_ref[...] = jnp.zeros_like(o_ref)
        coef_b[...] = jnp.zeros_like(coef_b)

    blk = jnp.minimum(e, n_blocks - 1)

    n_sub = eb // PEER_SUB
    d_sub = o_ref.shape[1] // n_sub

    def hidden(s):
        ht_scr[s % 2] = jnp.dot(u_ref[s * PEER_SUB:(s + 1) * PEER_SUB, :], xt_ref[...],
                                preferred_element_type=F32)

    def step(prev_ref, next_ref):
        hidden(0)
        for s in range(n_sub):
            ht = ht_scr.at[s % 2]
            cols = slice(s * d_sub, (s + 1) * d_sub)
            o_ref[:, cols] += lax.dot_general(prev_ref[...], v_ref[:, cols], (((0,), (0,)), ((), ())),
                                              preferred_element_type=F32)
            if s + 1 < n_sub:
                hidden(s + 1)
            for ii in range(PEER_SUB // N_KEYS):
                i = blk * (eb // N_KEYS) + s * (PEER_SUB // N_KEYS) + ii
                row = s * PEER_SUB + ii * N_KEYS
                gate = None
                for h in range(PEER_HEADS):
                    lim = lim_ref[h, pl.ds(i, 1), :].astype(BF16)
                    e1 = e1_ref[h, pl.ds(i, 1), :].astype(BF16)
                    term = jnp.where(r2_ref[h] < lim, e2_ref[h] * e1, jnp.zeros((), BF16))
                    gate = term if gate is None else gate + term
                next_ref[row:row + N_KEYS, :] = gate * _gelu_to_bf16(ht[ii * N_KEYS:(ii + 1) * N_KEYS, :])

    @pl.when(e % 2 == 0)
    def _():
        step(coef_b, coef_a)

    @pl.when(e % 2 == 1)
    def _():
        step(coef_a, coef_b)


def peer_dense(xt, r2, lim, e1, e2, u_bf, v_bf, *, tbl, eb=1024):
    d, m = xt.shape
    n_blocks = u_bf.shape[0] // eb
    head = pl.BlockSpec((PEER_HEADS, N_KEYS, tbl), lambda t, e: (0, 0, t))
    return pl.pallas_call(
        functools.partial(_peer_dense_kernel, eb=eb, n_blocks=n_blocks),
        grid=(m // tbl, n_blocks + 1),
        in_specs=[pl.BlockSpec((d, tbl), lambda t, e: (0, t)), head, head, head, head,
                  pl.BlockSpec((eb, d), lambda t, e: (jnp.minimum(e, n_blocks - 1), 0)),
                  pl.BlockSpec((eb, d), lambda t, e: (jnp.maximum(e - 1, 0), 0))],
        out_specs=pl.BlockSpec((tbl, d), lambda t, e: (t, 0)),
        out_shape=jax.ShapeDtypeStruct((m, d), F32),
        scratch_shapes=[pltpu.VMEM((eb, tbl), BF16), pltpu.VMEM((eb, tbl), BF16),
                        pltpu.VMEM((2, PEER_SUB, tbl), F32)],
        compiler_params=_params(("parallel", "arbitrary")),
        name="peer_dense",
    )(xt, r2, lim, e1, e2, u_bf, v_bf)


def _final_kernel(h_ref, p_ref, g_ref, o_ref):
    x = h_ref[...] + p_ref[...]
    o_ref[...] = x * lax.rsqrt(jnp.mean(x * x, axis=-1, keepdims=True) + RMS_EPS) * g_ref[...]


def final_norm(h, peer, g, row0, n_rows, *, tb):
    d = h.shape[1]
    off = row0 // tb
    rows = pl.BlockSpec((tb, d), lambda i: (i + off, 0))
    return pl.pallas_call(
        _final_kernel,
        grid=(n_rows // tb,),
        in_specs=[rows, rows, pl.BlockSpec((1, d), lambda i: (0, 0))],
        out_specs=pl.BlockSpec((tb, d), lambda i: (i, 0)),
        out_shape=jax.ShapeDtypeStruct((n_rows, d), F32),
        compiler_params=_params(("parallel",)),
        name="final_norm",
    )(h, peer, g.reshape(1, d))


def kernel(x_prompt, x_sample, mem_prompt, state_shift, state_wkv, cache_mem_k, cache_mem_v, ln1_g, w_in, gmlp_ln_g, gmlp_ln_b, gmlp_ws, gmlp_bs, rwkv_mu, rwkv_w0, rwkv_w2, rwkv_a0, rwkv_a2, rwkv_g2, rwkv_kk, rwkv_ka, rwkv_rk, rwkv_lnx_g, rwkv_lnx_b, mem_norm_g, w_mem_kv, w_up_g, w_up_r, w_up_x, w_out, ln2_g, peer_wq, peer_keys, peer_u, peer_v, final_g):
    depth = w_in.shape[0]
    assert depth == 1, "single-layer step"
    l = 0
    n_batch, seq_len, d = x_prompt.shape
    n_dec = x_sample.shape[0]
    n_prompt = n_batch * seq_len
    m = n_prompt + n_dec
    tb_small = _largest_divisor(m, (640, 128))
    tb_big = _largest_divisor(m, (1664, 640, 128))

    x = jnp.concatenate([x_prompt.reshape(n_prompt, d), x_sample.reshape(n_dec, d)], axis=0)
    xn = rmsnorm(x, ln1_g[l], tb=tb_small, out_dtype=BF16)
    proj = functools.partial(matmul, xn, w_in[l], tb=tb_big, nb=512)
    z_g = proj(col_off=0, n_cols=GMLP_COLS, name="proj_gmlp")
    z_r = proj(col_off=GMLP_COLS, n_cols=RWKV_COLS, name="proj_rwkv")
    z_q = proj(col_off=GMLP_COLS + RWKV_COLS, n_cols=XA_D, name="proj_xattn")
    z_gate = proj(col_off=GMLP_COLS + RWKV_COLS + XA_D, n_cols=GATE_COLS, out_dtype=BF16, name="proj_gate")

    memn = rmsnorm(mem_prompt.reshape(n_batch * N_MEM, d), mem_norm_g[l], tb=256, out_dtype=BF16)
    kv = matmul(memn, w_mem_kv[l], tb=N_MEM, nb=512, name="proj_mem_kv")
    p_mk = kv[:, :XA_D].reshape(1, n_batch, N_MEM, XA_HEADS, XA_HD)
    p_mv = kv[:, XA_D:].reshape(1, n_batch, N_MEM, XA_HEADS, XA_HD)

    row = lambda a: a.reshape(1, -1)
    zeros_lora = jnp.zeros((LORA_W, RWKV_D), F32)
    w2p = jnp.concatenate([rwkv_w2[l], zeros_lora], axis=0).astype(BF16)
    a2p = jnp.concatenate([zeros_lora, rwkv_a2[l]], axis=0).astype(BF16)
    rw = (row(rwkv_mu[l]), row(rwkv_w0[l]), w2p, row(rwkv_a0[l]), a2p, rwkv_g2[l].astype(BF16),
          row(rwkv_kk[l]), row(rwkv_ka[l]), row(rwkv_rk[l]))

    o_g = gmlp_prompt(z_g, jnp.zeros((m, GMLP_D), BF16), n_prompt, gmlp_ln_g[l], gmlp_ln_b[l], gmlp_ws[l],
                      gmlp_bs[l])
    prep = rwkv_prep(z_r, n_prompt, seq_len, *rw)
    o_r, st = rwkv_chunks(prep, jnp.zeros((HEAD_PAIRS, m, V7X_LANES), BF16), n_batch, seq_len,
                          rwkv_lnx_g[l], rwkv_lnx_b[l])
    o_x = xattn_prompt(z_q, kv, jnp.zeros((m, XA_D), BF16), n_batch, seq_len)

    gw = jnp.repeat(gmlp_ws[l][:, 0, 0], CHUNK).reshape(1, GMLP_D)
    gb = jnp.repeat(gmlp_bs[l][:, 0], CHUNK).reshape(1, GMLP_D)
    o_g, s_gv, wt, kkt, bt, kt, rt, vt, bonus_s, g_s = sample_mix(
        z_g, z_r, n_prompt, state_shift[l], o_g, gmlp_ln_g[l], gmlp_ln_b[l], gw, gb, *rw)
    s_wkv, y_s = sample_state(state_wkv[l], wt, kkt, bt, kt, rt, vt)
    o_r = sample_rwkv_out(y_s.reshape(RWKV_D, n_dec), bonus_s, g_s, rwkv_lnx_g[l], rwkv_lnx_b[l], o_r, n_prompt)
    o_x = xattn_sample(z_q, cache_mem_k[l].reshape(n_dec, N_MEM * XA_HEADS, XA_HD),
                       cache_mem_v[l].reshape(n_dec, N_MEM * XA_HEADS, XA_HD), o_x, n_prompt)

    merged = merge(o_g, o_r, o_x, z_gate, w_up_g[l], w_up_r[l], w_up_x[l], tb=tb_small)
    h = matmul(merged, w_out[l], tb=tb_big, nb=512, residual=x, name="proj_out")

    hn, hn_t = rmsnorm(h, ln2_g[l], tb=tb_small, out_dtype=BF16, transposed=True)
    q = matmul(hn, peer_wq[l], tb=tb_big, nb=512, name="proj_peer_q")
    r2, lim, e1, e2 = peer_topk(q, peer_keys[l])
    peer = peer_dense(hn_t, r2, lim, e1, e2, peer_u[l].astype(BF16), peer_v[l].astype(BF16), tbl=tb_small)
    y_prompt = final_norm(h, peer, final_g, 0, n_prompt, tb=512)
    y_sample = final_norm(h, peer, final_g, n_prompt, n_dec, tb=n_dec)

    st = st.reshape(n_batch, HEAD_PAIRS, 2, RWKV_HD, 2, RWKV_HD)
    p_wkv = jnp.stack([st[:, :, 0, :, 0, :], st[:, :, 1, :, 1, :]], axis=2)
    p_wkv = jnp.swapaxes(p_wkv, -1, -2).reshape(1, n_batch, RWKV_HEADS, RWKV_HD, RWKV_HD)
    p_shift = z_r[seq_len - 1:n_prompt:seq_len][None]
    s_shift = z_r[n_prompt:][None]
    s_wkv = s_wkv.reshape(1, n_dec, RWKV_HEADS, RWKV_HD, RWKV_HD)
    return (y_prompt.reshape(n_batch, seq_len, d), y_sample.reshape(n_dec, 1, d), p_mk, p_mv,
            p_shift, p_wkv, s_shift, s_wkv, s_gv.reshape(1, n_dec, 1, GMLP_D))
```

```python
import functools

import jax
import jax.numpy as jnp
from jax import lax
from jax.experimental import pallas as pl
from jax.experimental.pallas import tpu as pltpu

F32 = jnp.float32
BF16 = jnp.bfloat16

D_MODEL = 2048
CHUNK = 128
GMLP_GROUPS = 6
GMLP_D = 768
RWKV_HEADS = 12
RWKV_HD = 64
RWKV_D = 768
LORA_W = 64
LORA_A = 64
LORA_G = 128
RWKV_COLS = 2560
XA_HEADS = 4
XA_HD = 128
XA_D = 512
N_MEM = 256
GMLP_COLS = 2 * GMLP_D
GATE_COLS = 3 * D_MODEL
PEER_HEADS = 8
N_KEYS = 128
PEER_DH = 128
TOPK = 16
RMS_EPS = 1e-6
LN_EPS = 1e-5
GN_EPS = 64e-5

V7X_LANES = 128
V7X_VMEM_LIMIT_BYTES = 56 * 1024 * 1024

HEAD_PAIRS = RWKV_HEADS // 2
RWKV_CHUNK = 64
RWKV_CHUNKS_PER_STEP = 32
NEG_INF = float("-inf")
CAND_PAD = -(2.0 ** 100)
TAKEN_BELOW = -(2.0 ** 110)
TAKEN_BASE = -(2.0 ** 120)
TAKEN_STEP = 2.0 ** 115


def _params(semantics):
    return pltpu.CompilerParams(dimension_semantics=semantics, vmem_limit_bytes=V7X_VMEM_LIMIT_BYTES)


def _largest_divisor(m, candidates):
    return next(c for c in candidates if m % c == 0)


def _gelu(x):
    return 0.5 * x * (1.0 + jnp.tanh(0.7978845608028654 * (x + 0.044715 * (x * x * x))))


def _gelu_to_bf16(x):
    inner = (0.7978845608028654 * x) * (1.0 + 0.044715 * (x * x))
    xb = x.astype(BF16)
    return (0.5 * xb) * (1.0 + jnp.tanh(inner.astype(BF16)))


def _split_bf16(x):
    hi = x.astype(BF16)
    return hi, (x - hi.astype(F32)).astype(BF16)


def _split3_bf16(x):
    hi = x.astype(BF16)
    rest = x - hi.astype(F32)
    mid = rest.astype(BF16)
    return hi, mid, (rest - mid.astype(F32)).astype(BF16)


def _sigmoid(x):
    return 0.5 + 0.5 * jnp.tanh(0.5 * x)


def _rmsnorm_kernel(x_ref, g_ref, *o_refs, transposed):
    x = x_ref[...]
    y = x * lax.rsqrt(jnp.mean(x * x, axis=-1, keepdims=True) + RMS_EPS) * g_ref[...]
    o_refs[0][...] = y.astype(o_refs[0].dtype)
    if transposed:
        o_refs[1][...] = y.T.astype(o_refs[1].dtype)


def rmsnorm(x, g, *, tb, out_dtype, transposed=False):
    m, d = x.shape
    out_shape = [jax.ShapeDtypeStruct((m, d), out_dtype)]
    out_specs = [pl.BlockSpec((tb, d), lambda i: (i, 0))]
    if transposed:
        out_shape.append(jax.ShapeDtypeStruct((d, m), out_dtype))
        out_specs.append(pl.BlockSpec((d, tb), lambda i: (0, i)))
    res = pl.pallas_call(
        functools.partial(_rmsnorm_kernel, transposed=transposed),
        grid=(m // tb,),
        in_specs=[pl.BlockSpec((tb, d), lambda i: (i, 0)), pl.BlockSpec((1, d), lambda i: (0, 0))],
        out_specs=out_specs,
        out_shape=out_shape,
        compiler_params=_params(("parallel",)),
        name="rmsnorm_t" if transposed else "rmsnorm",
    )(x, g.reshape(1, d))
    return res if transposed else res[0]


def _matmul_kernel(a_ref, w_ref, *rest, has_residual):
    if has_residual:
        r_ref, o_ref, wbf_ref = rest
    else:
        o_ref, wbf_ref = rest

    @pl.when(pl.program_id(1) == 0)
    def _():
        wbf_ref[...] = w_ref[...].astype(BF16)

    acc = jnp.dot(a_ref[...], wbf_ref[...], preferred_element_type=F32)
    if has_residual:
        acc = acc + r_ref[...]
    o_ref[...] = acc.astype(o_ref.dtype)


def matmul(a, w, *, tb, nb, col_off=0, n_cols=None, residual=None, out_dtype=F32, name="matmul"):
    m, k = a.shape
    n = w.shape[1] if n_cols is None else n_cols
    off = col_off // nb
    in_specs = [pl.BlockSpec((tb, k), lambda j, i: (i, 0)),
                pl.BlockSpec((k, nb), lambda j, i: (0, j + off))]
    args = [a, w]
    if residual is not None:
        in_specs.append(pl.BlockSpec((tb, nb), lambda j, i: (i, j)))
        args.append(residual)
    return pl.pallas_call(
        functools.partial(_matmul_kernel, has_residual=residual is not None),
        grid=(n // nb, m // tb),
        in_specs=in_specs,
        out_specs=pl.BlockSpec((tb, nb), lambda j, i: (i, j)),
        out_shape=jax.ShapeDtypeStruct((m, n), out_dtype),
        scratch_shapes=[pltpu.VMEM((k, nb), BF16)],
        compiler_params=_params(("arbitrary", "arbitrary")),
        name=name,
    )(*args)


def _layernorm(v, g, b):
    vc = v - jnp.mean(v, axis=-1, keepdims=True)
    var = jnp.mean(vc * vc, axis=-1, keepdims=True)
    return vc * lax.rsqrt(var + LN_EPS) * g + b


def _gmlp_kernel(z_ref, lng_ref, lnb_ref, ws_ref, bst_ref, buf_any, o_ref):
    del buf_any
    ge = _gelu(z_ref[...])
    u = ge[:, :GMLP_D]
    v = _layernorm(ge[:, GMLP_D:], lng_ref[...], lnb_ref[...])
    row = lax.broadcasted_iota(jnp.int32, (CHUNK, CHUNK), 0)
    col = lax.broadcasted_iota(jnp.int32, (CHUNK, CHUNK), 1)
    causal = col <= row
    for g in range(GMLP_GROUPS):
        sl = slice(g * CHUNK, (g + 1) * CHUNK)
        wm = jnp.where(causal, ws_ref[g], 0.0).astype(BF16)
        mixed = jnp.dot(wm, v[:, sl].astype(BF16), preferred_element_type=F32) + bst_ref[:, g:g + 1]
        o_ref[:, sl] = (u[:, sl] * mixed).astype(o_ref.dtype)


def gmlp_prompt(z_g, buf, n_tokens, ln_g, ln_b, ws, bs):
    return pl.pallas_call(
        _gmlp_kernel,
        grid=(n_tokens // CHUNK,),
        in_specs=[pl.BlockSpec((CHUNK, GMLP_COLS), lambda i: (i, 0)),
                  pl.BlockSpec((1, GMLP_D), lambda i: (0, 0)),
                  pl.BlockSpec((1, GMLP_D), lambda i: (0, 0)),
                  pl.BlockSpec((GMLP_GROUPS, CHUNK, CHUNK), lambda i: (0, 0, 0)),
                  pl.BlockSpec((CHUNK, GMLP_GROUPS), lambda i: (0, 0)),
                  pl.BlockSpec(memory_space=pl.ANY)],
        out_specs=pl.BlockSpec((CHUNK, GMLP_D), lambda i: (i, 0)),
        out_shape=jax.ShapeDtypeStruct(buf.shape, buf.dtype),
        input_output_aliases={5: 0},
        compiler_params=_params(("parallel",)),
        name="gmlp_prompt",
    )(z_g, ln_g.reshape(1, GMLP_D), ln_b.reshape(1, GMLP_D), ws, bs.T, buf)


def _head_block_ones():
    r = lax.broadcasted_iota(jnp.int32, (V7X_LANES, V7X_LANES), 0) // RWKV_HD
    c = lax.broadcasted_iota(jnp.int32, (V7X_LANES, V7X_LANES), 1) // RWKV_HD
    return (r == c).astype(F32)


def _head_sum(x, ones_bd):
    ones = ones_bd.astype(BF16)
    return sum(jnp.dot(t, ones, preferred_element_type=F32) for t in _split3_bf16(x))


def _rwkv_mix(zs, w0, w2p, a0, a2p, g2, kkp, ka, rk, ones_bd):
    r = zs[:, 0:RWKV_D]
    k = zs[:, RWKV_D:2 * RWKV_D]
    v = zs[:, 2 * RWKV_D:3 * RWKV_D]
    lwa = zs[:, 3 * RWKV_D:3 * RWKV_D + LORA_W + LORA_A]
    lg = zs[:, 3 * RWKV_D + LORA_W + LORA_A:]
    lora_w = jnp.dot(jnp.tanh(lwa).astype(BF16), w2p, preferred_element_type=F32)
    lora_a = jnp.dot(lwa.astype(BF16), a2p, preferred_element_type=F32)
    x = -(w0 + lora_w)
    softplus = jnp.maximum(x, 0.0) + jnp.log(1.0 + jnp.exp(-jnp.abs(x)))
    logw = -jnp.exp(-softplus - 0.5)
    a = _sigmoid(a0 + lora_a)
    g = jnp.dot(_sigmoid(lg).astype(BF16), g2, preferred_element_type=F32)
    kk = k * kkp
    k2 = k * (1.0 + (a - 1.0) * ka)
    rkk = r * k2 * rk
    out = []
    for p in range(HEAD_PAIRS):
        sl = slice(p * V7X_LANES, (p + 1) * V7X_LANES)
        kk_p = kk[:, sl]
        norm = jnp.sqrt(_head_sum(kk_p * kk_p, ones_bd))
        kkn = kk_p / jnp.maximum(norm, 1e-12)
        bonus = _head_sum(rkk[:, sl], ones_bd) * v[:, sl]
        out.append((r[:, sl], k2[:, sl], v[:, sl], logw[:, sl], kkn, a[:, sl], g[:, sl], bonus))
    return out


def _rwkv_prep_kernel(z_ref, zp_ref, mu_ref, w0_ref, w2_ref, a0_ref, a2_ref, g2_ref, kkp_ref, ka_ref, rk_ref,
                      at_ref, btp_ref, ktp_ref, rt_ref, v_ref, g_ref, bonus_ref, w_ref, pc_ref, cs_scr,
                      *, tb, blocks_per_seq):
    i = pl.program_id(0)
    z = z_ref[...]
    first = (i % blocks_per_seq) == 0
    prev_row = jnp.where(first, 0.0, zp_ref[7:8, :])
    row = lax.broadcasted_iota(jnp.int32, (tb, 1), 0)
    zprev = jnp.where(row == 0, prev_row, pltpu.roll(z, 1, axis=0))
    zs = z + (zprev - z) * mu_ref[...]
    ones_bd = _head_block_ones()
    mixed = _rwkv_mix(zs, w0_ref[...], w2_ref[...], a0_ref[...], a2_ref[...], g2_ref[...], kkp_ref[...],
                      ka_ref[...], rk_ref[...], ones_bd)
    n_chunks = tb // RWKV_CHUNK
    shape3 = (n_chunks, RWKV_CHUNK, V7X_LANES)
    tr = lax.broadcasted_iota(jnp.int32, (n_chunks, RWKV_CHUNK, RWKV_CHUNK), 1)
    tc = lax.broadcasted_iota(jnp.int32, (n_chunks, RWKV_CHUNK, RWKV_CHUNK), 2)
    tri = (tc <= tr).astype(BF16)
    bdot = lambda x: lax.dot_general(tri, x, (((2,), (1,)), ((0,), (0,))), preferred_element_type=F32)
    for p, (r, k2, v, logw, kkn, a, g, bonus) in enumerate(mixed):
        hi, mid, lo = _split3_bf16(logw.reshape(shape3))
        cs3 = bdot(hi) + bdot(mid) + bdot(lo)
        total = cs3[:, RWKV_CHUNK - 1:RWKV_CHUNK, :]
        cs = cs3.reshape(tb, V7X_LANES)
        tail = jnp.exp(total - cs3).reshape(tb, V7X_LANES)
        cs_scr[...] = cs
        cs_end = cs_scr[pl.ds(RWKV_CHUNK - 1, n_chunks, stride=RWKV_CHUNK), :]
        inv_p = jnp.exp(-cs)
        at_ref[p] = (jnp.exp(cs - logw) * kkn).astype(BF16)
        btp_ref[p] = (kkn * a * tail).astype(BF16)
        ktp_ref[p] = (k2 * tail).astype(BF16)
        rt_ref[p] = (jnp.exp(cs) * r).astype(BF16)
        v_ref[p] = v.astype(BF16)
        g_ref[p] = g
        bonus_ref[p] = bonus
        w_ref[p] = jnp.concatenate([kkn * a * inv_p, k2 * inv_p], axis=-1).astype(BF16)
        pc_ref[p] = jnp.exp(cs_end)


def rwkv_prep(z_r, n_tokens, seq_len, mu, w0, w2p, a0, a2p, g2, kkp, ka, rk, *, tb=512):
    n_chunks = n_tokens // RWKV_CHUNK
    row_spec = lambda w: pl.BlockSpec((1, w), lambda i: (0, 0))
    pair_out = lambda w: pl.BlockSpec((HEAD_PAIRS, tb, w), lambda i: (0, i, 0))
    pair_shape = lambda w, dt=BF16: jax.ShapeDtypeStruct((HEAD_PAIRS, n_tokens, w), dt)
    return pl.pallas_call(
        functools.partial(_rwkv_prep_kernel, tb=tb, blocks_per_seq=seq_len // tb),
        grid=(n_tokens // tb,),
        in_specs=[pl.BlockSpec((tb, RWKV_COLS), lambda i: (i, 0)),
                  pl.BlockSpec((8, RWKV_COLS), lambda i: (jnp.maximum(i * (tb // 8) - 1, 0), 0)),
                  row_spec(RWKV_COLS), row_spec(RWKV_D),
                  pl.BlockSpec((LORA_W + LORA_A, RWKV_D), lambda i: (0, 0)),
                  row_spec(RWKV_D),
                  pl.BlockSpec((LORA_W + LORA_A, RWKV_D), lambda i: (0, 0)),
                  pl.BlockSpec((LORA_G, RWKV_D), lambda i: (0, 0)),
                  row_spec(RWKV_D), row_spec(RWKV_D), row_spec(RWKV_D)],
        out_specs=[pair_out(V7X_LANES)] * 7 + [pair_out(2 * V7X_LANES),
                   pl.BlockSpec((HEAD_PAIRS, tb // RWKV_CHUNK, V7X_LANES), lambda i: (0, i, 0))],
        out_shape=[pair_shape(V7X_LANES)] * 5 + [pair_shape(V7X_LANES, F32)] * 2 + [pair_shape(2 * V7X_LANES),
                   jax.ShapeDtypeStruct((HEAD_PAIRS, n_chunks, V7X_LANES), F32)],
        scratch_shapes=[pltpu.VMEM((tb, V7X_LANES), F32)],
        compiler_params=_params(("parallel",)),
        name="rwkv_prep",
    )(z_r, z_r, mu, w0, w2p, a0, a2p, g2, kkp, ka, rk)


def _bmm(x, y):
    return lax.dot_general(x.astype(BF16), y.astype(BF16), (((2,), (1,)), ((0,), (0,))),
                           preferred_element_type=F32)


def _bmm_nt(x, y):
    return lax.dot_general(x.astype(BF16), y.astype(BF16), (((2,), (2,)), ((0,), (0,))),
                           preferred_element_type=F32)


def _mm_tn(x, y, precision=None):
    return lax.dot_general(x, y, (((0,), (0,)), ((), ())), preferred_element_type=F32, precision=precision)


def _group_norm_gate(y, bonus, g, lng, lnb, ones_bd):
    mean = _head_sum(y, ones_bd) * (1.0 / RWKV_HD)
    yc = y - mean
    var = _head_sum(yc * yc, ones_bd) * (1.0 / RWKV_HD)
    return (yc * lax.rsqrt(var + GN_EPS) * lng + lnb + bonus) * g


def _rwkv_chunk_kernel(at_ref, btp_ref, ktp_ref, rt_ref, v_ref, g_ref, bonus_ref, w_ref, pc_ref, lng_ref,
                       lnb_ref, buf_any, o_ref, st_ref, st_scr):
    del buf_any
    nc, c, ln = RWKV_CHUNKS_PER_STEP, RWKV_CHUNK, V7X_LANES
    t = pl.program_id(1)

    @pl.when(t == 0)
    def _():
        st_scr[...] = jnp.zeros_like(st_scr)

    shape3 = (nc, c, ln)
    at = at_ref[0].reshape(shape3)
    rt = rt_ref[0].reshape(shape3)
    v = v_ref[0].reshape(shape3)
    btk = w_ref[0].reshape(nc, c, 2 * ln)
    bt, kt = btk[..., :ln], btk[..., ln:]
    lane = lax.broadcasted_iota(jnp.int32, (1, 1, ln), 2)
    head0 = lane < RWKV_HD
    head0_2 = jnp.concatenate([head0, head0], axis=-1)
    zero = jnp.zeros(shape3, F32)

    lhs = jnp.concatenate([jnp.where(head0, at, 0.0), jnp.where(head0, rt, 0.0),
                           jnp.where(head0, 0.0, at), jnp.where(head0, 0.0, rt)], axis=1)
    gram = _bmm_nt(lhs, jnp.concatenate([bt, kt], axis=1))
    gr = lax.broadcasted_iota(jnp.int32, (1, 4 * c, 2 * c), 1)
    gc = lax.broadcasted_iota(jnp.int32, (1, 4 * c, 2 * c), 2) % c
    causal = gr % c + (gr // c) % 2 > gc
    gram = jnp.where(causal, gram, 0.0)
    zv = jnp.concatenate([jnp.zeros(shape3, BF16), v], axis=1)
    first = lax.broadcasted_iota(jnp.int32, (1, 1, 2 * c), 2) < c

    xs, mvs, mwus = [], [], []
    for h in range(2):
        g_h = gram[:, 2 * h * c:(2 * h + 2) * c, :]
        lmv = _bmm(g_h, zv)
        x = jnp.concatenate([at.astype(F32), lmv[:, :c, :]], axis=-1)
        lp = jnp.where(first, -g_h[:, :c, :], 0.0)
        pad2 = jnp.zeros((nc, c, 2 * ln), F32)
        x = x + _bmm(lp, jnp.concatenate([x, pad2], axis=1))
        n = 2
        while n < c:
            lp = _bmm(lp, jnp.concatenate([lp, zero], axis=1))
            x = x + _bmm(lp, jnp.concatenate([x, pad2], axis=1))
            n *= 2
        xs.append(x)
        mvs.append(lmv[:, c:, :])
        mrb = jnp.where(first, g_h[:, c:, :], 0.0)
        mwus.append(_bmm(mrb, jnp.concatenate([x, pad2], axis=1)))
    x = jnp.where(head0_2, xs[0], xs[1])
    mv = jnp.where(head0, mvs[0], mvs[1])
    mwu = jnp.where(head0_2, mwus[0], mwus[1])
    qe = (rt.astype(F32) - mwu[..., :ln]).astype(BF16)
    yl = mv - mwu[..., ln:]

    rr = lax.broadcasted_iota(jnp.int32, (ln, ln), 0)
    cc = lax.broadcasted_iota(jnp.int32, (ln, ln), 1)
    same_head = (rr // RWKV_HD) == (cc // RWKV_HD)
    eye = rr == cc
    btp = btp_ref[0].reshape(shape3).astype(BF16)
    ktp = ktp_ref[0].reshape(shape3).astype(BF16)
    x_bf = x.astype(BF16)
    v_bf = v.astype(BF16)
    pc = pc_ref[0]
    st = st_scr[...]
    ys = []
    for j in range(nc):
        bwu = _mm_tn(btp[j], x_bf[j])
        kv = _mm_tn(ktp[j], v_bf[j])
        tr = jnp.where(same_head, jnp.where(eye, pc[j:j + 1, :], 0.0) - bwu[:, :ln], 0.0)
        ad = jnp.where(same_head, kv - bwu[:, ln:], 0.0)
        st_hi = st.astype(BF16)
        st_lo = (st - st_hi.astype(F32)).astype(BF16)
        ys.append(jnp.dot(qe[j], st_hi, preferred_element_type=F32) + yl[j])
        tr_bf = tr.astype(BF16)
        st = (jnp.dot(tr_bf, st_hi, preferred_element_type=F32)
              + jnp.dot(tr_bf, st_lo, preferred_element_type=F32) + ad)
    st_scr[...] = st
    st_ref[0] = st
    y = jnp.concatenate(ys, axis=0)
    o = _group_norm_gate(y, bonus_ref[0], g_ref[0], lng_ref[0], lnb_ref[0], _head_block_ones())
    o_ref[0] = o.astype(o_ref.dtype)


def rwkv_chunks(prep, buf, n_batch, seq_len, lnx_g, lnx_b):
    at, btp, ktp, rt, v, g, bonus, wbk, pc = prep
    rows = RWKV_CHUNK * RWKV_CHUNKS_PER_STEP
    steps = seq_len // rows
    n_tokens = n_batch * seq_len
    tok = lambda w: pl.BlockSpec((1, rows, w), lambda bp, t: (bp % HEAD_PAIRS, (bp // HEAD_PAIRS) * steps + t, 0))
    pair_row = pl.BlockSpec((1, 1, V7X_LANES), lambda bp, t: (bp % HEAD_PAIRS, 0, 0))
    return pl.pallas_call(
        _rwkv_chunk_kernel,
        grid=(n_batch * HEAD_PAIRS, steps),
        in_specs=[tok(V7X_LANES)] * 7 + [tok(2 * V7X_LANES),
                  pl.BlockSpec((1, RWKV_CHUNKS_PER_STEP, V7X_LANES),
                               lambda bp, t: (bp % HEAD_PAIRS, (bp // HEAD_PAIRS) * steps + t, 0)),
                  pair_row, pair_row, pl.BlockSpec(memory_space=pl.ANY)],
        out_specs=[tok(V7X_LANES),
                   pl.BlockSpec((1, V7X_LANES, V7X_LANES), lambda bp, t: (bp, 0, 0))],
        out_shape=[jax.ShapeDtypeStruct(buf.shape, buf.dtype),
                   jax.ShapeDtypeStruct((n_batch * HEAD_PAIRS, V7X_LANES, V7X_LANES), F32)],
        input_output_aliases={11: 0},
        scratch_shapes=[pltpu.VMEM((V7X_LANES, V7X_LANES), F32)],
        compiler_params=_params(("parallel", "arbitrary")),
        name="rwkv_chunks",
    )(at, btp, ktp, rt, v, g, bonus, wbk, pc,
      lnx_g.reshape(HEAD_PAIRS, 1, V7X_LANES), lnx_b.reshape(HEAD_PAIRS, 1, V7X_LANES), buf)


def _softmax_rows(s):
    e = jnp.exp(s - jnp.max(s, axis=-1, keepdims=True))
    return e / jnp.sum(e, axis=-1, keepdims=True)


def _xattn_kernel(q_ref, k_ref, v_ref, buf_any, o_ref):
    del buf_any
    for h in range(XA_HEADS):
        sl = slice(h * XA_HD, (h + 1) * XA_HD)
        s = lax.dot_general(q_ref[:, sl].astype(BF16), k_ref[:, sl].astype(BF16), (((1,), (1,)), ((), ())),
                            preferred_element_type=F32) * (XA_HD ** -0.5)
        pr = _softmax_rows(s)
        o_ref[:, sl] = jnp.dot(pr.astype(BF16), v_ref[:, sl].astype(BF16),
                               preferred_element_type=F32).astype(o_ref.dtype)


def xattn_prompt(z_q, kv, buf, n_batch, seq_len, *, tq=512):
    steps = seq_len // tq
    return pl.pallas_call(
        _xattn_kernel,
        grid=(n_batch, steps),
        in_specs=[pl.BlockSpec((tq, XA_D), lambda b, t: (b * steps + t, 0)),
                  pl.BlockSpec((N_MEM, XA_D), lambda b, t: (b, 0)),
                  pl.BlockSpec((N_MEM, XA_D), lambda b, t: (b, 1)),
                  pl.BlockSpec(memory_space=pl.ANY)],
        out_specs=pl.BlockSpec((tq, XA_D), lambda b, t: (b * steps + t, 0)),
        out_shape=jax.ShapeDtypeStruct(buf.shape, buf.dtype),
        input_output_aliases={3: 0},
        compiler_params=_params(("parallel", "parallel")),
        name="xattn_prompt",
    )(z_q, kv, kv, buf)


def _sample_mix_kernel(zg_ref, zr_ref, shift_ref, lng_ref, lnb_ref, gw_ref, gb_ref,
                       mu_ref, w0_ref, w2_ref, a0_ref, a2_ref, g2_ref, kkp_ref, ka_ref, rk_ref, og_any,
                       og_ref, gv_ref, wt_ref, kkt_ref, bt_ref, kt_ref, rt_ref, vt_ref, bonus_ref, g_ref):
    del og_any
    ge = _gelu(zg_ref[...])
    v = _layernorm(ge[:, GMLP_D:], lng_ref[...], lnb_ref[...])
    gv_ref[...] = v
    og_ref[...] = (ge[:, :GMLP_D] * (v * gw_ref[...] + gb_ref[...])).astype(og_ref.dtype)

    z = zr_ref[...]
    zs = z + (shift_ref[...] - z) * mu_ref[...]
    mixed = _rwkv_mix(zs, w0_ref[...], w2_ref[...], a0_ref[...], a2_ref[...], g2_ref[...], kkp_ref[...],
                      ka_ref[...], rk_ref[...], _head_block_ones())
    for p, (r, k2, vv, logw, kkn, a, g, bonus) in enumerate(mixed):
        sl = slice(p * V7X_LANES, (p + 1) * V7X_LANES)
        wt_ref[sl, :] = jnp.exp(logw).T
        kkt_ref[sl, :] = kkn.T
        bt_ref[sl, :] = (kkn * a).T
        kt_ref[sl, :] = k2.T
        rt_ref[sl, :] = r.T
        vt_ref[sl, :] = vv.T
        bonus_ref[:, sl] = bonus
        g_ref[:, sl] = g


def sample_mix(z_g, z_r, row0, shift, o_g, ln_g, ln_b, gw, gb, mu, w0, w2p, a0, a2p, g2, kkp, ka, rk):
    n = shift.shape[0]
    blk = row0 // n
    tok = lambda w: pl.BlockSpec((n, w), lambda i: (blk, 0))
    loc = lambda w: pl.BlockSpec((n, w), lambda i: (0, 0))
    row = lambda w: pl.BlockSpec((1, w), lambda i: (0, 0))
    full2 = lambda a, b: pl.BlockSpec((a, b), lambda i: (0, 0))
    chan = jax.ShapeDtypeStruct((RWKV_D, n), F32)
    return pl.pallas_call(
        _sample_mix_kernel,
        grid=(1,),
        in_specs=[tok(GMLP_COLS), tok(RWKV_COLS), loc(RWKV_COLS),
                  row(GMLP_D), row(GMLP_D), row(GMLP_D), row(GMLP_D),
                  row(RWKV_COLS), row(RWKV_D), full2(LORA_W + LORA_A, RWKV_D), row(RWKV_D),
                  full2(LORA_W + LORA_A, RWKV_D), full2(LORA_G, RWKV_D), row(RWKV_D), row(RWKV_D), row(RWKV_D),
                  pl.BlockSpec(memory_space=pl.ANY)],
        out_specs=[tok(GMLP_D), loc(GMLP_D)] + [full2(RWKV_D, n)] * 6 + [loc(RWKV_D), loc(RWKV_D)],
        out_shape=[jax.ShapeDtypeStruct(o_g.shape, o_g.dtype), jax.ShapeDtypeStruct((n, GMLP_D), F32)]
                  + [chan] * 6 + [jax.ShapeDtypeStruct((n, RWKV_D), F32)] * 2,
        input_output_aliases={16: 0},
        compiler_params=_params(("arbitrary",)),
        name="sample_mix",
    )(z_g, z_r, shift, ln_g.reshape(1, -1), ln_b.reshape(1, -1), gw, gb, mu, w0, w2p, a0, a2p, g2, kkp, ka, rk, o_g)


def _sample_state_kernel(s_ref, w_ref, kk_ref, b_ref, k_ref, r_ref, v_ref, snew_ref, y_ref):
    n = s_ref.shape[0]
    st = s_ref[...].T.reshape(RWKV_HD, RWKV_HD, n)
    sa = jnp.sum(st * (-kk_ref[...])[None], axis=1, keepdims=True)
    st_new = st * w_ref[...][None] + sa * b_ref[...][None] + v_ref[...] * k_ref[...][None]
    y_ref[...] = jnp.sum(st_new * r_ref[...][None], axis=1, keepdims=True)
    snew_ref[...] = st_new.reshape(RWKV_HD * RWKV_HD, n).T


def sample_state(s0, wt, kkt, bt, kt, rt, vt):
    n = s0.shape[0]
    hd2 = RWKV_HD * RWKV_HD
    state = pl.BlockSpec((n, hd2), lambda h: (0, h))
    keyvec = pl.BlockSpec((RWKV_HD, n), lambda h: (h, 0))
    valvec = pl.BlockSpec((RWKV_HD, 1, n), lambda h: (h, 0, 0))
    return pl.pallas_call(
        _sample_state_kernel,
        grid=(RWKV_HEADS,),
        in_specs=[state, keyvec, keyvec, keyvec, keyvec, keyvec, valvec],
        out_specs=[state, valvec],
        out_shape=[jax.ShapeDtypeStruct((n, RWKV_HEADS * hd2), F32),
                   jax.ShapeDtypeStruct((RWKV_D, 1, n), F32)],
        compiler_params=_params(("parallel",)),
        name="sample_state",
    )(s0.reshape(n, RWKV_HEADS * hd2), wt, kkt, bt, kt, rt, vt.reshape(RWKV_D, 1, n))


def _sample_rwkv_out_kernel(yt_ref, bonus_ref, g_ref, lng_ref, lnb_ref, or_any, or_ref):
    del or_any
    y = yt_ref[...].T
    ones_bd = _head_block_ones()
    for p in range(HEAD_PAIRS):
        sl = slice(p * V7X_LANES, (p + 1) * V7X_LANES)
        o = _group_norm_gate(y[:, sl], bonus_ref[:, sl], g_ref[:, sl], lng_ref[p], lnb_ref[p], ones_bd)
        or_ref[p] = o.astype(or_ref.dtype)


def sample_rwkv_out(yt, bonus, g, lnx_g, lnx_b, o_r, row0):
    n = bonus.shape[0]
    full2 = lambda a, b: pl.BlockSpec((a, b), lambda i: (0, 0))
    pair_row = pl.BlockSpec((HEAD_PAIRS, 1, V7X_LANES), lambda i: (0, 0, 0))
    return pl.pallas_call(
        _sample_rwkv_out_kernel,
        grid=(1,),
        in_specs=[full2(RWKV_D, n), full2(n, RWKV_D), full2(n, RWKV_D), pair_row, pair_row,
                  pl.BlockSpec(memory_space=pl.ANY)],
        out_specs=pl.BlockSpec((HEAD_PAIRS, n, V7X_LANES), lambda i: (0, row0 // n, 0)),
        out_shape=jax.ShapeDtypeStruct(o_r.shape, o_r.dtype),
        input_output_aliases={5: 0},
        compiler_params=_params(("arbitrary",)),
        name="sample_rwkv_out",
    )(yt, bonus, g, lnx_g.reshape(HEAD_PAIRS, 1, V7X_LANES), lnx_b.reshape(HEAD_PAIRS, 1, V7X_LANES), o_r)


def _xattn_sample_kernel(q_ref, ck_ref, cv_ref, ox_any, ox_ref, *, sb):
    del ox_any
    q = q_ref[...]
    rows = N_MEM * XA_HEADS
    hrow = lax.broadcasted_iota(jnp.int32, (8, rows), 0) % XA_HEADS
    hcol = lax.broadcasted_iota(jnp.int32, (8, rows), 1) % XA_HEADS
    own = hrow == hcol
    h8 = lax.broadcasted_iota(jnp.int32, (8, XA_HD), 0) % XA_HEADS
    nt = (((1,), (1,)), ((), ()))
    outs = []
    for s in range(sb):
        qm = jnp.zeros((8, XA_HD), F32)
        for h in range(XA_HEADS):
            qm = jnp.where(h8 == h, q[s:s + 1, h * XA_HD:(h + 1) * XA_HD], qm)
        sc = lax.dot_general(qm.astype(BF16), ck_ref[s].astype(BF16), nt,
                             preferred_element_type=F32) * (XA_HD ** -0.5)
        pr = _softmax_rows(jnp.where(own, sc, NEG_INF))
        res = jnp.dot(pr.astype(BF16), cv_ref[s].astype(BF16), preferred_element_type=F32)
        outs.append(jnp.concatenate([res[h:h + 1, :] for h in range(XA_HEADS)], axis=1))
    ox_ref[...] = jnp.concatenate(outs, axis=0).astype(ox_ref.dtype)


def xattn_sample(z_q, ck, cv, o_x, row0, *, sb=8):
    n = ck.shape[0]
    off = row0 // sb
    tok = pl.BlockSpec((sb, XA_D), lambda i: (i + off, 0))
    cache = pl.BlockSpec((sb, N_MEM * XA_HEADS, XA_HD), lambda i: (i, 0, 0))
    return pl.pallas_call(
        functools.partial(_xattn_sample_kernel, sb=sb),
        grid=(n // sb,),
        in_specs=[tok, cache, cache, pl.BlockSpec(memory_space=pl.ANY)],
        out_specs=tok,
        out_shape=jax.ShapeDtypeStruct(o_x.shape, o_x.dtype),
        input_output_aliases={3: 0},
        compiler_params=_params(("arbitrary",)),
        name="xattn_sample",
    )(z_q, ck, cv, o_x)


def _merge_kernel(og_ref, or_ref, ox_ref, g0_ref, g1_ref, g2_ref, wg_ref, wr_ref, wx_ref, o_ref,
                  wg_bf, wr_bf, wx_bf):
    @pl.when(pl.program_id(1) == 0)
    def _():
        wg_bf[...] = wg_ref[...].astype(BF16)
        wr_bf[...] = wr_ref[...].astype(BF16)
        wx_bf[...] = wx_ref[...].astype(BF16)

    up_g = jnp.dot(og_ref[...], wg_bf[...], preferred_element_type=F32)
    up_r = jnp.dot(or_ref[0], wr_bf[0:V7X_LANES, :], preferred_element_type=F32)
    for p in range(1, HEAD_PAIRS):
        up_r = up_r + jnp.dot(or_ref[p], wr_bf[p * V7X_LANES:(p + 1) * V7X_LANES, :], preferred_element_type=F32)
    up_x = jnp.dot(ox_ref[...], wx_bf[...], preferred_element_type=F32)
    gate = lambda ref: _sigmoid(ref[...].astype(F32))
    merged = gate(g0_ref) * up_g + gate(g1_ref) * up_r + gate(g2_ref) * up_x
    o_ref[...] = merged.astype(o_ref.dtype)


def merge(o_g, o_r, o_x, z_gate, w_up_g, w_up_r, w_up_x, *, tb, nb=512):
    m = o_g.shape[0]
    nblk = D_MODEL // nb
    gate = lambda b: pl.BlockSpec((tb, nb), lambda j, i: (i, b * nblk + j))
    wspec = lambda k: pl.BlockSpec((k, nb), lambda j, i: (0, j))
    return pl.pallas_call(
        _merge_kernel,
        grid=(nblk, m // tb),
        in_specs=[pl.BlockSpec((tb, GMLP_D), lambda j, i: (i, 0)),
                  pl.BlockSpec((HEAD_PAIRS, tb, V7X_LANES), lambda j, i: (0, i, 0)),
                  pl.BlockSpec((tb, XA_D), lambda j, i: (i, 0)),
                  gate(0), gate(1), gate(2), wspec(GMLP_D), wspec(RWKV_D), wspec(XA_D)],
        out_specs=pl.BlockSpec((tb, nb), lambda j, i: (i, j)),
        out_shape=jax.ShapeDtypeStruct((m, D_MODEL), BF16),
        scratch_shapes=[pltpu.VMEM((GMLP_D, nb), BF16), pltpu.VMEM((RWKV_D, nb), BF16),
                        pltpu.VMEM((XA_D, nb), BF16)],
        compiler_params=_params(("arbitrary", "arbitrary")),
        name="merge",
    )(o_g, o_r, o_x, z_gate, z_gate, z_gate, w_up_g, w_up_r, w_up_x)


def _extract_top(src_ref, work_ref, rank_ref, vals_ref, n_rows):
    width = work_ref.shape[1]
    riota = lax.broadcasted_iota(jnp.int32, (n_rows, V7X_LANES), 0)
    kiota = lax.broadcasted_iota(jnp.int32, (TOPK, V7X_LANES), 0)

    def run(break_ties):
        work_ref[...] = src_ref[...]
        vals_ref[...] = jnp.zeros_like(vals_ref)

        def body(p, carry):
            taken = TAKEN_BASE - p.astype(F32) * TAKEN_STEP
            for c in range(width // V7X_LANES):
                sl = slice(c * V7X_LANES, (c + 1) * V7X_LANES)
                w = work_ref[:, sl]
                m = jnp.max(w, axis=0, keepdims=True)
                if break_ties:
                    idx = jnp.min(jnp.where(w == m, riota, n_rows), axis=0, keepdims=True)
                    hit = riota == idx
                else:
                    hit = w == m
                work_ref[:, sl] = jnp.where(hit, taken, w)
                vals_ref[:, sl] = jnp.where(kiota == p, m, vals_ref[:, sl])
            return carry

        lax.fori_loop(0, TOPK, body, 0)

    run(False)
    picked = jnp.sum(jnp.where(work_ref[...] < TAKEN_BELOW, 1.0, 0.0), axis=0, keepdims=True)
    tied = jnp.max(picked) > TOPK

    @pl.when(tied)
    def _():
        run(True)

    w = work_ref[...]
    rank = jnp.where(w < TAKEN_BELOW, w * (-1.0 / TAKEN_STEP) + (TAKEN_BASE / TAKEN_STEP), float(TOPK))
    rank_ref[...] = rank.astype(jnp.int32)


_CAND_COUNT = tuple(TOPK // (a + 1) for a in range(TOPK))
_CAND_START = tuple(sum(_CAND_COUNT[:a]) for a in range(TOPK))
_CAND_ROWS = -(-sum(_CAND_COUNT) // 8) * 8


def _peer_topk_kernel(q_ref, keys_ref, r2_ref, lim_ref, e1_ref, e2_ref,
                      s_scr, work_scr, rank_scr, vals_scr, cand_scr, cwork_scr, crank_scr, cvals_scr, *, tbk):
    nt = (((1,), (1,)), ((), ()))
    for h in range(PEER_HEADS):
        for c in range(2):
            qcol = (2 * h + c) * PEER_DH
            col = (2 * h + c) * tbk
            k_hi, k_lo = _split_bf16(keys_ref[h, c])
            q_hi, q_lo = _split_bf16(q_ref[:, qcol:qcol + PEER_DH])
            sc = (lax.dot_general(k_hi, q_hi, nt, preferred_element_type=F32)
                  + lax.dot_general(k_hi, q_lo, nt, preferred_element_type=F32)
                  + lax.dot_general(k_lo, q_hi, nt, preferred_element_type=F32))
            s_scr[:, col:col + tbk] = sc
    _extract_top(s_scr, work_scr, rank_scr, vals_scr, N_KEYS)

    crow = lax.broadcasted_iota(jnp.int32, (_CAND_ROWS, tbk), 0)
    seg = jnp.full((_CAND_ROWS, tbk), TOPK, jnp.int32)
    for a in reversed(range(TOPK)):
        seg = jnp.where(crow < _CAND_START[a] + _CAND_COUNT[a], jnp.minimum(seg, a), seg)
    pad = jnp.zeros((_CAND_ROWS - TOPK, tbk), F32)
    for h in range(PEER_HEADS):
        v1 = vals_scr[:, (2 * h) * tbk:(2 * h + 1) * tbk]
        v2 = jnp.concatenate([vals_scr[:, (2 * h + 1) * tbk:(2 * h + 2) * tbk], pad], axis=0)
        cand = jnp.full((_CAND_ROWS, tbk), CAND_PAD, F32)
        for a in range(TOPK):
            shifted = v2 if _CAND_START[a] == 0 else pltpu.roll(v2, _CAND_START[a], axis=0)
            cand = jnp.where(seg == a, v1[a:a + 1, :] + shifted, cand)
        cand_scr[:, h * tbk:(h + 1) * tbk] = cand
    _extract_top(cand_scr, cwork_scr, crank_scr, cvals_scr, _CAND_ROWS)

    for h in range(PEER_HEADS):
        hs = slice(h * tbk, (h + 1) * tbk)
        s1 = slice((2 * h) * tbk, (2 * h + 1) * tbk)
        s2 = slice((2 * h + 1) * tbk, (2 * h + 2) * tbk)
        cvals = cvals_scr[:, hs]
        z = jnp.sum(jnp.exp(cvals - cvals[0:1, :]), axis=0, keepdims=True)
        chosen = crank_scr[:, hs] < TOPK
        rank1 = rank_scr[:, s1]
        lim = jnp.zeros((N_KEYS, tbk), F32)
        for a in range(TOPK):
            count = jnp.sum(jnp.where(jnp.logical_and(chosen, seg == a), 1.0, 0.0), axis=0, keepdims=True)
            lim = jnp.where(rank1 == a, count, lim)
        lim_ref[h] = lim
        r2_ref[h] = rank_scr[:, s2].astype(F32).astype(r2_ref.dtype)
        e1_ref[h] = jnp.exp(s_scr[:, s1] - vals_scr[0:1, s1]) / z
        e2_ref[h] = jnp.exp(s_scr[:, s2] - vals_scr[0:1, s2]).astype(e2_ref.dtype)


def peer_topk(q, keys, *, tbk=128):
    m = q.shape[0]
    out = pl.BlockSpec((PEER_HEADS, N_KEYS, tbk), lambda i: (0, 0, i))
    shape = lambda dt: jax.ShapeDtypeStruct((PEER_HEADS, N_KEYS, m), dt)
    wide = 2 * PEER_HEADS * tbk
    return pl.pallas_call(
        functools.partial(_peer_topk_kernel, tbk=tbk),
        grid=(m // tbk,),
        in_specs=[pl.BlockSpec((tbk, PEER_HEADS * 2 * PEER_DH), lambda i: (i, 0)),
                  pl.BlockSpec((PEER_HEADS, 2, N_KEYS, PEER_DH), lambda i: (0, 0, 0, 0))],
        out_specs=[out] * 4,
        out_shape=[shape(BF16), shape(F32), shape(F32), shape(BF16)],
        scratch_shapes=[pltpu.VMEM((N_KEYS, wide), F32), pltpu.VMEM((N_KEYS, wide), F32),
                        pltpu.VMEM((N_KEYS, wide), jnp.int32), pltpu.VMEM((TOPK, wide), F32),
                        pltpu.VMEM((_CAND_ROWS, PEER_HEADS * tbk), F32),
                        pltpu.VMEM((_CAND_ROWS, PEER_HEADS * tbk), F32),
                        pltpu.VMEM((_CAND_ROWS, PEER_HEADS * tbk), jnp.int32),
                        pltpu.VMEM((TOPK, PEER_HEADS * tbk), F32)],
        compiler_params=_params(("parallel",)),
        name="peer_topk",
    )(q, keys)


PEER_SUB = 2 * N_KEYS


def _peer_dense_kernel(xt_ref, r2_ref, lim_ref, e1_ref, e2_ref, u_ref, v_ref, o_ref, coef_a, coef_b, ht_scr,
                       *, eb, n_blocks):
    e = pl.program_id(1)

    @pl.when(e == 0)
    def _():
        o_ref[...] = jnp.zeros_like(o_ref)
        coef_b[...] = jnp.zeros_like(coef_b)

    blk = jnp.minimum(e, n_blocks - 1)

    n_sub = eb // PEER_SUB
    d_sub = o_ref.shape[1] // n_sub

    def hidden(s):
        ht_scr[s % 2] = jnp.dot(u_ref[s * PEER_SUB:(s + 1) * PEER_SUB, :], xt_ref[...],
                                preferred_element_type=F32)

    def step(prev_ref, next_ref):
        hidden(0)
        for s in range(n_sub):
            ht = ht_scr.at[s % 2]
            cols = slice(s * d_sub, (s + 1) * d_sub)
            o_ref[:, cols] += lax.dot_general(prev_ref[...], v_ref[:, cols], (((0,), (0,)), ((), ())),
                                              preferred_element_type=F32)
            if s + 1 < n_sub:
                hidden(s + 1)
            for ii in range(PEER_SUB // N_KEYS):
                i = blk * (eb // N_KEYS) + s * (PEER_SUB // N_KEYS) + ii
                row = s * PEER_SUB + ii * N_KEYS
                gate = None
                for h in range(PEER_HEADS):
                    lim = lim_ref[h, pl.ds(i, 1), :].astype(BF16)
                    e1 = e1_ref[h, pl.ds(i, 1), :].astype(BF16)
                    term = jnp.where(r2_ref[h] < lim, e2_ref[h] * e1, jnp.zeros((), BF16))
                    gate = term if gate is None else gate + term
                next_ref[row:row + N_KEYS, :] = gate * _gelu_to_bf16(ht[ii * N_KEYS:(ii + 1) * N_KEYS, :])

    @pl.when(e % 2 == 0)
    def _():
        step(coef_b, coef_a)

    @pl.when(e % 2 == 1)
    def _():
        step(coef_a, coef_b)


def peer_dense(xt, r2, lim, e1, e2, u_bf, v_bf, *, tbl, eb=1024):
    d, m = xt.shape
    n_blocks = u_bf.shape[0] // eb
    head = pl.BlockSpec((PEER_HEADS, N_KEYS, tbl), lambda t, e: (0, 0, t))
    return pl.pallas_call(
        functools.partial(_peer_dense_kernel, eb=eb, n_blocks=n_blocks),
        grid=(m // tbl, n_blocks + 1),
        in_specs=[pl.BlockSpec((d, tbl), lambda t, e: (0, t)), head, head, head, head,
                  pl.BlockSpec((eb, d), lambda t, e: (jnp.minimum(e, n_blocks - 1), 0)),
                  pl.BlockSpec((eb, d), lambda t, e: (jnp.maximum(e - 1, 0), 0))],
        out_specs=pl.BlockSpec((tbl, d), lambda t, e: (t, 0)),
        out_shape=jax.ShapeDtypeStruct((m, d), F32),
        scratch_shapes=[pltpu.VMEM((eb, tbl), BF16), pltpu.VMEM((eb, tbl), BF16),
                        pltpu.VMEM((2, PEER_SUB, tbl), F32)],
        compiler_params=_params(("parallel", "arbitrary")),
        name="peer_dense",
    )(xt, r2, lim, e1, e2, u_bf, v_bf)


def _final_kernel(h_ref, p_ref, g_ref, o_ref):
    x = h_ref[...] + p_ref[...]
    o_ref[...] = x * lax.rsqrt(jnp.mean(x * x, axis=-1, keepdims=True) + RMS_EPS) * g_ref[...]


def final_norm(h, peer, g, row0, n_rows, *, tb):
    d = h.shape[1]
    off = row0 // tb
    rows = pl.BlockSpec((tb, d), lambda i: (i + off, 0))
    return pl.pallas_call(
        _final_kernel,
        grid=(n_rows // tb,),
        in_specs=[rows, rows, pl.BlockSpec((1, d), lambda i: (0, 0))],
        out_specs=pl.BlockSpec((tb, d), lambda i: (i, 0)),
        out_shape=jax.ShapeDtypeStruct((n_rows, d), F32),
        compiler_params=_params(("parallel",)),
        name="final_norm",
    )(h, peer, g.reshape(1, d))


def kernel(x_prompt, x_sample, mem_prompt, state_shift, state_wkv, cache_mem_k, cache_mem_v, ln1_g, w_in, gmlp_ln_g, gmlp_ln_b, gmlp_ws, gmlp_bs, rwkv_mu, rwkv_w0, rwkv_w2, rwkv_a0, rwkv_a2, rwkv_g2, rwkv_kk, rwkv_ka, rwkv_rk, rwkv_lnx_g, rwkv_lnx_b, mem_norm_g, w_mem_kv, w_up_g, w_up_r, w_up_x, w_out, ln2_g, peer_wq, peer_keys, peer_u, peer_v, final_g):
    depth = w_in.shape[0]
    assert depth == 1, "single-layer step"
    l = 0
    n_batch, seq_len, d = x_prompt.shape
    n_dec = x_sample.shape[0]
    n_prompt = n_batch * seq_len
    m = n_prompt + n_dec
    tb_small = _largest_divisor(m, (640, 128))
    tb_big = _largest_divisor(m, (1664, 640, 128))

    x = jnp.concatenate([x_prompt.reshape(n_prompt, d), x_sample.reshape(n_dec, d)], axis=0)
    xn = rmsnorm(x, ln1_g[l], tb=tb_small, out_dtype=BF16)
    proj = functools.partial(matmul, xn, w_in[l], tb=tb_big, nb=512)
    z_g = proj(col_off=0, n_cols=GMLP_COLS, name="proj_gmlp")
    z_r = proj(col_off=GMLP_COLS, n_cols=RWKV_COLS, name="proj_rwkv")
    z_q = proj(col_off=GMLP_COLS + RWKV_COLS, n_cols=XA_D, name="proj_xattn")
    z_gate = proj(col_off=GMLP_COLS + RWKV_COLS + XA_D, n_cols=GATE_COLS, out_dtype=BF16, name="proj_gate")

    memn = rmsnorm(mem_prompt.reshape(n_batch * N_MEM, d), mem_norm_g[l], tb=256, out_dtype=BF16)
    kv = matmul(memn, w_mem_kv[l], tb=N_MEM, nb=512, name="proj_mem_kv")
    p_mk = kv[:, :XA_D].reshape(1, n_batch, N_MEM, XA_HEADS, XA_HD)
    p_mv = kv[:, XA_D:].reshape(1, n_batch, N_MEM, XA_HEADS, XA_HD)

    row = lambda a: a.reshape(1, -1)
    zeros_lora = jnp.zeros((LORA_W, RWKV_D), F32)
    w2p = jnp.concatenate([rwkv_w2[l], zeros_lora], axis=0).astype(BF16)
    a2p = jnp.concatenate([zeros_lora, rwkv_a2[l]], axis=0).astype(BF16)
    rw = (row(rwkv_mu[l]), row(rwkv_w0[l]), w2p, row(rwkv_a0[l]), a2p, rwkv_g2[l].astype(BF16),
          row(rwkv_kk[l]), row(rwkv_ka[l]), row(rwkv_rk[l]))

    o_g = gmlp_prompt(z_g, jnp.zeros((m, GMLP_D), BF16), n_prompt, gmlp_ln_g[l], gmlp_ln_b[l], gmlp_ws[l],
                      gmlp_bs[l])
    prep = rwkv_prep(z_r, n_prompt, seq_len, *rw)
    o_r, st = rwkv_chunks(prep, jnp.zeros((HEAD_PAIRS, m, V7X_LANES), BF16), n_batch, seq_len,
                          rwkv_lnx_g[l], rwkv_lnx_b[l])
    o_x = xattn_prompt(z_q, kv, jnp.zeros((m, XA_D), BF16), n_batch, seq_len)

    gw = jnp.repeat(gmlp_ws[l][:, 0, 0], CHUNK).reshape(1, GMLP_D)
    gb = jnp.repeat(gmlp_bs[l][:, 0], CHUNK).reshape(1, GMLP_D)
    o_g, s_gv, wt, kkt, bt, kt, rt, vt, bonus_s, g_s = sample_mix(
        z_g, z_r, n_prompt, state_shift[l], o_g, gmlp_ln_g[l], gmlp_ln_b[l], gw, gb, *rw)
    s_wkv, y_s = sample_state(state_wkv[l], wt, kkt, bt, kt, rt, vt)
    o_r = sample_rwkv_out(y_s.reshape(RWKV_D, n_dec), bonus_s, g_s, rwkv_lnx_g[l], rwkv_lnx_b[l], o_r, n_prompt)
    o_x = xattn_sample(z_q, cache_mem_k[l].reshape(n_dec, N_MEM * XA_HEADS, XA_HD),
                       cache_mem_v[l].reshape(n_dec, N_MEM * XA_HEADS, XA_HD), o_x, n_prompt)

    merged = merge(o_g, o_r, o_x, z_gate, w_up_g[l], w_up_r[l], w_up_x[l], tb=tb_small)
    h = matmul(merged, w_out[l], tb=tb_big, nb=512, residual=x, name="proj_out")

    hn, hn_t = rmsnorm(h, ln2_g[l], tb=tb_small, out_dtype=BF16, transposed=True)
    q = matmul(hn, peer_wq[l], tb=tb_big, nb=512, name="proj_peer_q")
    r2, lim, e1, e2 = peer_topk(q, peer_keys[l])
    peer = peer_dense(hn_t, r2, lim, e1, e2, peer_u[l].astype(BF16), peer_v[l].astype(BF16), tbl=tb_small)
    y_prompt = final_norm(h, peer, final_g, 0, n_prompt, tb=512)
    y_sample = final_norm(h, peer, final_g, n_prompt, n_dec, tb=n_dec)

    st = st.reshape(n_batch, HEAD_PAIRS, 2, RWKV_HD, 2, RWKV_HD)
    p_wkv = jnp.stack([st[:, :, 0, :, 0, :], st[:, :, 1, :, 1, :]], axis=2)
    p_wkv = jnp.swapaxes(p_wkv, -1, -2).reshape(1, n_batch, RWKV_HEADS, RWKV_HD, RWKV_HD)
    p_shift = z_r[seq_len - 1:n_prompt:seq_len][None]
    s_shift = z_r[n_prompt:][None]
    s_wkv = s_wkv.reshape(1, n_dec, RWKV_HEADS, RWKV_HD, RWKV_HD)
    return (y_prompt.reshape(n_batch, seq_len, d), y_sample.reshape(n_dec, 1, d), p_mk, p_mv,
            p_shift, p_wkv, s_shift, s_wkv, s_gv.reshape(1, n_dec, 1, GMLP_D))
```

```python
import functools

import jax
import jax.numpy as jnp
from jax import lax
from jax.experimental import pallas as pl
from jax.experimental.pallas import tpu as pltpu

F32 = jnp.float32
BF16 = jnp.bfloat16

D_MODEL = 2048
CHUNK = 128
GMLP_GROUPS = 6
GMLP_D = 768
RWKV_HEADS = 12
RWKV_HD = 64
RWKV_D = 768
LORA_W = 64
LORA_A = 64
LORA_G = 128
RWKV_COLS = 2560
XA_HEADS = 4
XA_HD = 128
XA_D = 512
N_MEM = 256
GMLP_COLS = 2 * GMLP_D
GATE_COLS = 3 * D_MODEL
PEER_HEADS = 8
N_KEYS = 128
PEER_DH = 128
TOPK = 16
RMS_EPS = 1e-6
LN_EPS = 1e-5
GN_EPS = 64e-5

V7X_LANES = 128
V7X_VMEM_LIMIT_BYTES = 56 * 1024 * 1024

HEAD_PAIRS = RWKV_HEADS // 2
RWKV_CHUNK = 64
RWKV_CHUNKS_PER_STEP = 32
NEG_INF = float("-inf")
CAND_PAD = -(2.0 ** 100)
TAKEN_BELOW = -(2.0 ** 110)
TAKEN_BASE = -(2.0 ** 120)
TAKEN_STEP = 2.0 ** 115


def _params(semantics):
    return pltpu.CompilerParams(dimension_semantics=semantics, vmem_limit_bytes=V7X_VMEM_LIMIT_BYTES)


def _largest_divisor(m, candidates):
    return next(c for c in candidates if m % c == 0)


def _gelu(x):
    return 0.5 * x * (1.0 + jnp.tanh(0.7978845608028654 * (x + 0.044715 * (x * x * x))))


def _gelu_to_bf16(x):
    inner = (0.7978845608028654 * x) * (1.0 + 0.044715 * (x * x))
    xb = x.astype(BF16)
    return (0.5 * xb) * (1.0 + jnp.tanh(inner.astype(BF16)))


def _split_bf16(x):
    hi = x.astype(BF16)
    return hi, (x - hi.astype(F32)).astype(BF16)


def _split3_bf16(x):
    hi = x.astype(BF16)
    rest = x - hi.astype(F32)
    mid = rest.astype(BF16)
    return hi, mid, (rest - mid.astype(F32)).astype(BF16)


def _sigmoid(x):
    return 0.5 + 0.5 * jnp.tanh(0.5 * x)


def _rmsnorm_kernel(x_ref, g_ref, *o_refs, transposed):
    x = x_ref[...]
    y = x * lax.rsqrt(jnp.mean(x * x, axis=-1, keepdims=True) + RMS_EPS) * g_ref[...]
    o_refs[0][...] = y.astype(o_refs[0].dtype)
    if transposed:
        o_refs[1][...] = y.T.astype(o_refs[1].dtype)


def rmsnorm(x, g, *, tb, out_dtype, transposed=False):
    m, d = x.shape
    out_shape = [jax.ShapeDtypeStruct((m, d), out_dtype)]
    out_specs = [pl.BlockSpec((tb, d), lambda i: (i, 0))]
    if transposed:
        out_shape.append(jax.ShapeDtypeStruct((d, m), out_dtype))
        out_specs.append(pl.BlockSpec((d, tb), lambda i: (0, i)))
    res = pl.pallas_call(
        functools.partial(_rmsnorm_kernel, transposed=transposed),
        grid=(m // tb,),
        in_specs=[pl.BlockSpec((tb, d), lambda i: (i, 0)), pl.BlockSpec((1, d), lambda i: (0, 0))],
        out_specs=out_specs,
        out_shape=out_shape,
        compiler_params=_params(("parallel",)),
        name="rmsnorm_t" if transposed else "rmsnorm",
    )(x, g.reshape(1, d))
    return res if transposed else res[0]


def _matmul_kernel(a_ref, w_ref, *rest, has_residual):
    if has_residual:
        r_ref, o_ref, wbf_ref = rest
    else:
        o_ref, wbf_ref = rest

    @pl.when(pl.program_id(1) == 0)
    def _():
        wbf_ref[...] = w_ref[...].astype(BF16)

    acc = jnp.dot(a_ref[...], wbf_ref[...], preferred_element_type=F32)
    if has_residual:
        acc = acc + r_ref[...]
    o_ref[...] = acc.astype(o_ref.dtype)


def matmul(a, w, *, tb, nb, col_off=0, n_cols=None, residual=None, out_dtype=F32, name="matmul"):
    m, k = a.shape
    n = w.shape[1] if n_cols is None else n_cols
    off = col_off // nb
    in_specs = [pl.BlockSpec((tb, k), lambda j, i: (i, 0)),
                pl.BlockSpec((k, nb), lambda j, i: (0, j + off))]
    args = [a, w]
    if residual is not None:
        in_specs.append(pl.BlockSpec((tb, nb), lambda j, i: (i, j)))
        args.append(residual)
    return pl.pallas_call(
        functools.partial(_matmul_kernel, has_residual=residual is not None),
        grid=(n // nb, m // tb),
        in_specs=in_specs,
        out_specs=pl.BlockSpec((tb, nb), lambda j, i: (i, j)),
        out_shape=jax.ShapeDtypeStruct((m, n), out_dtype),
        scratch_shapes=[pltpu.VMEM((k, nb), BF16)],
        compiler_params=_params(("arbitrary", "arbitrary")),
        name=name,
    )(*args)


def _layernorm(v, g, b):
    vc = v - jnp.mean(v, axis=-1, keepdims=True)
    var = jnp.mean(vc * vc, axis=-1, keepdims=True)
    return vc * lax.rsqrt(var + LN_EPS) * g + b


def _gmlp_kernel(z_ref, lng_ref, lnb_ref, ws_ref, bst_ref, buf_any, o_ref):
    del buf_any
    ge = _gelu(z_ref[...])
    u = ge[:, :GMLP_D]
    v = _layernorm(ge[:, GMLP_D:], lng_ref[...], lnb_ref[...])
    row = lax.broadcasted_iota(jnp.int32, (CHUNK, CHUNK), 0)
    col = lax.broadcasted_iota(jnp.int32, (CHUNK, CHUNK), 1)
    causal = col <= row
    for g in range(GMLP_GROUPS):
        sl = slice(g * CHUNK, (g + 1) * CHUNK)
        wm = jnp.where(causal, ws_ref[g], 0.0).astype(BF16)
        mixed = jnp.dot(wm, v[:, sl].astype(BF16), preferred_element_type=F32) + bst_ref[:, g:g + 1]
        o_ref[:, sl] = (u[:, sl] * mixed).astype(o_ref.dtype)


def gmlp_prompt(z_g, buf, n_tokens, ln_g, ln_b, ws, bs):
    return pl.pallas_call(
        _gmlp_kernel,
        grid=(n_tokens // CHUNK,),
        in_specs=[pl.BlockSpec((CHUNK, GMLP_COLS), lambda i: (i, 0)),
                  pl.BlockSpec((1, GMLP_D), lambda i: (0, 0)),
                  pl.BlockSpec((1, GMLP_D), lambda i: (0, 0)),
                  pl.BlockSpec((GMLP_GROUPS, CHUNK, CHUNK), lambda i: (0, 0, 0)),
                  pl.BlockSpec((CHUNK, GMLP_GROUPS), lambda i: (0, 0)),
                  pl.BlockSpec(memory_space=pl.ANY)],
        out_specs=pl.BlockSpec((CHUNK, GMLP_D), lambda i: (i, 0)),
        out_shape=jax.ShapeDtypeStruct(buf.shape, buf.dtype),
        input_output_aliases={5: 0},
        compiler_params=_params(("parallel",)),
        name="gmlp_prompt",
    )(z_g, ln_g.reshape(1, GMLP_D), ln_b.reshape(1, GMLP_D), ws, bs.T, buf)


def _head_block_ones():
    r = lax.broadcasted_iota(jnp.int32, (V7X_LANES, V7X_LANES), 0) // RWKV_HD
    c = lax.broadcasted_iota(jnp.int32, (V7X_LANES, V7X_LANES), 1) // RWKV_HD
    return (r == c).astype(F32)


def _head_sum(x, ones_bd):
    ones = ones_bd.astype(BF16)
    return sum(jnp.dot(t, ones, preferred_element_type=F32) for t in _split3_bf16(x))


def _rwkv_mix(zs, w0, w2p, a0, a2p, g2, kkp, ka, rk, ones_bd):
    r = zs[:, 0:RWKV_D]
    k = zs[:, RWKV_D:2 * RWKV_D]
    v = zs[:, 2 * RWKV_D:3 * RWKV_D]
    lwa = zs[:, 3 * RWKV_D:3 * RWKV_D + LORA_W + LORA_A]
    lg = zs[:, 3 * RWKV_D + LORA_W + LORA_A:]
    lora_w = jnp.dot(jnp.tanh(lwa).astype(BF16), w2p, preferred_element_type=F32)
    lora_a = jnp.dot(lwa.astype(BF16), a2p, preferred_element_type=F32)
    x = -(w0 + lora_w)
    softplus = jnp.maximum(x, 0.0) + jnp.log(1.0 + jnp.exp(-jnp.abs(x)))
    logw = -jnp.exp(-softplus - 0.5)
    a = _sigmoid(a0 + lora_a)
    g = jnp.dot(_sigmoid(lg).astype(BF16), g2, preferred_element_type=F32)
    kk = k * kkp
    k2 = k * (1.0 + (a - 1.0) * ka)
    rkk = r * k2 * rk
    out = []
    for p in range(HEAD_PAIRS):
        sl = slice(p * V7X_LANES, (p + 1) * V7X_LANES)
        kk_p = kk[:, sl]
        norm = jnp.sqrt(_head_sum(kk_p * kk_p, ones_bd))
        kkn = kk_p / jnp.maximum(norm, 1e-12)
        bonus = _head_sum(rkk[:, sl], ones_bd) * v[:, sl]
        out.append((r[:, sl], k2[:, sl], v[:, sl], logw[:, sl], kkn, a[:, sl], g[:, sl], bonus))
    return out


def _rwkv_prep_kernel(z_ref, zp_ref, mu_ref, w0_ref, w2_ref, a0_ref, a2_ref, g2_ref, kkp_ref, ka_ref, rk_ref,
                      at_ref, btp_ref, ktp_ref, rt_ref, v_ref, g_ref, bonus_ref, w_ref, pc_ref, cs_scr,
                      *, tb, blocks_per_seq):
    i = pl.program_id(0)
    z = z_ref[...]
    first = (i % blocks_per_seq) == 0
    prev_row = jnp.where(first, 0.0, zp_ref[7:8, :])
    row = lax.broadcasted_iota(jnp.int32, (tb, 1), 0)
    zprev = jnp.where(row == 0, prev_row, pltpu.roll(z, 1, axis=0))
    zs = z + (zprev - z) * mu_ref[...]
    ones_bd = _head_block_ones()
    mixed = _rwkv_mix(zs, w0_ref[...], w2_ref[...], a0_ref[...], a2_ref[...], g2_ref[...], kkp_ref[...],
                      ka_ref[...], rk_ref[...], ones_bd)
    n_chunks = tb // RWKV_CHUNK
    shape3 = (n_chunks, RWKV_CHUNK, V7X_LANES)
    tr = lax.broadcasted_iota(jnp.int32, (n_chunks, RWKV_CHUNK, RWKV_CHUNK), 1)
    tc = lax.broadcasted_iota(jnp.int32, (n_chunks, RWKV_CHUNK, RWKV_CHUNK), 2)
    tri = (tc <= tr).astype(BF16)
    bdot = lambda x: lax.dot_general(tri, x, (((2,), (1,)), ((0,), (0,))), preferred_element_type=F32)
    for p, (r, k2, v, logw, kkn, a, g, bonus) in enumerate(mixed):
        hi, mid, lo = _split3_bf16(logw.reshape(shape3))
        cs3 = bdot(hi) + bdot(mid) + bdot(lo)
        total = cs3[:, RWKV_CHUNK - 1:RWKV_CHUNK, :]
        cs = cs3.reshape(tb, V7X_LANES)
        tail = jnp.exp(total - cs3).reshape(tb, V7X_LANES)
        cs_scr[...] = cs
        cs_end = cs_scr[pl.ds(RWKV_CHUNK - 1, n_chunks, stride=RWKV_CHUNK), :]
        inv_p = jnp.exp(-cs)
        at_ref[p] = (jnp.exp(cs - logw) * kkn).astype(BF16)
        btp_ref[p] = (kkn * a * tail).astype(BF16)
        ktp_ref[p] = (k2 * tail).astype(BF16)
        rt_ref[p] = (jnp.exp(cs) * r).astype(BF16)
        v_ref[p] = v.astype(BF16)
        g_ref[p] = g
        bonus_ref[p] = bonus
        w_ref[p] = jnp.concatenate([kkn * a * inv_p, k2 * inv_p], axis=-1).astype(BF16)
        pc_ref[p] = jnp.exp(cs_end)


def rwkv_prep(z_r, n_tokens, seq_len, mu, w0, w2p, a0, a2p, g2, kkp, ka, rk, *, tb=512):
    n_chunks = n_tokens // RWKV_CHUNK
    row_spec = lambda w: pl.BlockSpec((1, w), lambda i: (0, 0))
    pair_out = lambda w: pl.BlockSpec((HEAD_PAIRS, tb, w), lambda i: (0, i, 0))
    pair_shape = lambda w, dt=BF16: jax.ShapeDtypeStruct((HEAD_PAIRS, n_tokens, w), dt)
    return pl.pallas_call(
        functools.partial(_rwkv_prep_kernel, tb=tb, blocks_per_seq=seq_len // tb),
        grid=(n_tokens // tb,),
        in_specs=[pl.BlockSpec((tb, RWKV_COLS), lambda i: (i, 0)),
                  pl.BlockSpec((8, RWKV_COLS), lambda i: (jnp.maximum(i * (tb // 8) - 1, 0), 0)),
                  row_spec(RWKV_COLS), row_spec(RWKV_D),
                  pl.BlockSpec((LORA_W + LORA_A, RWKV_D), lambda i: (0, 0)),
                  row_spec(RWKV_D),
                  pl.BlockSpec((LORA_W + LORA_A, RWKV_D), lambda i: (0, 0)),
                  pl.BlockSpec((LORA_G, RWKV_D), lambda i: (0, 0)),
                  row_spec(RWKV_D), row_spec(RWKV_D), row_spec(RWKV_D)],
        out_specs=[pair_out(V7X_LANES)] * 7 + [pair_out(2 * V7X_LANES),
                   pl.BlockSpec((HEAD_PAIRS, tb // RWKV_CHUNK, V7X_LANES), lambda i: (0, i, 0))],
        out_shape=[pair_shape(V7X_LANES)] * 5 + [pair_shape(V7X_LANES, F32)] * 2 + [pair_shape(2 * V7X_LANES),
                   jax.ShapeDtypeStruct((HEAD_PAIRS, n_chunks, V7X_LANES), F32)],
        scratch_shapes=[pltpu.VMEM((tb, V7X_LANES), F32)],
        compiler_params=_params(("parallel",)),
        name="rwkv_prep",
    )(z_r, z_r, mu, w0, w2p, a0, a2p, g2, kkp, ka, rk)


def _bmm(x, y):
    return lax.dot_general(x.astype(BF16), y.astype(BF16), (((2,), (1,)), ((0,), (0,))),
                           preferred_element_type=F32)


def _bmm_nt(x, y):
    return lax.dot_general(x.astype(BF16), y.astype(BF16), (((2,), (2,)), ((0,), (0,))),
                           preferred_element_type=F32)


def _mm_tn(x, y, precision=None):
    return lax.dot_general(x, y, (((0,), (0,)), ((), ())), preferred_element_type=F32, precision=precision)


def _group_norm_gate(y, bonus, g, lng, lnb, ones_bd):
    mean = _head_sum(y, ones_bd) * (1.0 / RWKV_HD)
    yc = y - mean
    var = _head_sum(yc * yc, ones_bd) * (1.0 / RWKV_HD)
    return (yc * lax.rsqrt(var + GN_EPS) * lng + lnb + bonus) * g


def _rwkv_chunk_kernel(at_ref, btp_ref, ktp_ref, rt_ref, v_ref, g_ref, bonus_ref, w_ref, pc_ref, lng_ref,
                       lnb_ref, buf_any, o_ref, st_ref, st_scr):
    del buf_any
    nc, c, ln = RWKV_CHUNKS_PER_STEP, RWKV_CHUNK, V7X_LANES
    t = pl.program_id(1)

    @pl.when(t == 0)
    def _():
        st_scr[...] = jnp.zeros_like(st_scr)

    shape3 = (nc, c, ln)
    at = at_ref[0].reshape(shape3)
    rt = rt_ref[0].reshape(shape3)
    v = v_ref[0].reshape(shape3)
    btk = w_ref[0].reshape(nc, c, 2 * ln)
    bt, kt = btk[..., :ln], btk[..., ln:]
    lane = lax.broadcasted_iota(jnp.int32, (1, 1, ln), 2)
    head0 = lane < RWKV_HD
    head0_2 = jnp.concatenate([head0, head0], axis=-1)
    zero = jnp.zeros(shape3, F32)

    lhs = jnp.concatenate([jnp.where(head0, at, 0.0), jnp.where(head0, rt, 0.0),
                           jnp.where(head0, 0.0, at), jnp.where(head0, 0.0, rt)], axis=1)
    gram = _bmm_nt(lhs, jnp.concatenate([bt, kt], axis=1))
    gr = lax.broadcasted_iota(jnp.int32, (1, 4 * c, 2 * c), 1)
    gc = lax.broadcasted_iota(jnp.int32, (1, 4 * c, 2 * c), 2) % c
    causal = gr % c + (gr // c) % 2 > gc
    gram = jnp.where(causal, gram, 0.0)
    zv = jnp.concatenate([jnp.zeros(shape3, BF16), v], axis=1)
    first = lax.broadcasted_iota(jnp.int32, (1, 1, 2 * c), 2) < c

    xs, mvs, mwus = [], [], []
    for h in range(2):
        g_h = gram[:, 2 * h * c:(2 * h + 2) * c, :]
        lmv = _bmm(g_h, zv)
        x = jnp.concatenate([at.astype(F32), lmv[:, :c, :]], axis=-1)
        lp = jnp.where(first, -g_h[:, :c, :], 0.0)
        pad2 = jnp.zeros((nc, c, 2 * ln), F32)
        x = x + _bmm(lp, jnp.concatenate([x, pad2], axis=1))
        n = 2
        while n < c:
            lp = _bmm(lp, jnp.concatenate([lp, zero], axis=1))
            x = x + _bmm(lp, jnp.concatenate([x, pad2], axis=1))
            n *= 2
        xs.append(x)
        mvs.append(lmv[:, c:, :])
        mrb = jnp.where(first, g_h[:, c:, :], 0.0)
        mwus.append(_bmm(mrb, jnp.concatenate([x, pad2], axis=1)))
    x = jnp.where(head0_2, xs[0], xs[1])
    mv = jnp.where(head0, mvs[0], mvs[1])
    mwu = jnp.where(head0_2, mwus[0], mwus[1])
    qe = (rt.astype(F32) - mwu[..., :ln]).astype(BF16)
    yl = mv - mwu[..., ln:]

    rr = lax.broadcasted_iota(jnp.int32, (ln, ln), 0)
    cc = lax.broadcasted_iota(jnp.int32, (ln, ln), 1)
    same_head = (rr // RWKV_HD) == (cc // RWKV_HD)
    eye = rr == cc
    btp = btp_ref[0].reshape(shape3).astype(BF16)
    ktp = ktp_ref[0].reshape(shape3).astype(BF16)
    x_bf = x.astype(BF16)
    v_bf = v.astype(BF16)
    pc = pc_ref[0]
    st = st_scr[...]
    ys = []
    for j in range(nc):
        bwu = _mm_tn(btp[j], x_bf[j])
        kv = _mm_tn(ktp[j], v_bf[j])
        tr = jnp.where(same_head, jnp.where(eye, pc[j:j + 1, :], 0.0) - bwu[:, :ln], 0.0)
        ad = jnp.where(same_head, kv - bwu[:, ln:], 0.0)
        st_hi = st.astype(BF16)
        st_lo = (st - st_hi.astype(F32)).astype(BF16)
        ys.append(jnp.dot(qe[j], st_hi, preferred_element_type=F32) + yl[j])
        tr_bf = tr.astype(BF16)
        st = (jnp.dot(tr_bf, st_hi, preferred_element_type=F32)
              + jnp.dot(tr_bf, st_lo, preferred_element_type=F32) + ad)
    st_scr[...] = st
    st_ref[0] = st
    y = jnp.concatenate(ys, axis=0)
    o = _group_norm_gate(y, bonus_ref[0], g_ref[0], lng_ref[0], lnb_ref[0], _head_block_ones())
    o_ref[0] = o.astype(o_ref.dtype)


def rwkv_chunks(prep, buf, n_batch, seq_len, lnx_g, lnx_b):
    at, btp, ktp, rt, v, g, bonus, wbk, pc = prep
    rows = RWKV_CHUNK * RWKV_CHUNKS_PER_STEP
    steps = seq_len // rows
    n_tokens = n_batch * seq_len
    tok = lambda w: pl.BlockSpec((1, rows, w), lambda bp, t: (bp % HEAD_PAIRS, (bp // HEAD_PAIRS) * steps + t, 0))
    pair_row = pl.BlockSpec((1, 1, V7X_LANES), lambda bp, t: (bp % HEAD_PAIRS, 0, 0))
    return pl.pallas_call(
        _rwkv_chunk_kernel,
        grid=(n_batch * HEAD_PAIRS, steps),
        in_specs=[tok(V7X_LANES)] * 7 + [tok(2 * V7X_LANES),
                  pl.BlockSpec((1, RWKV_CHUNKS_PER_STEP, V7X_LANES),
                               lambda bp, t: (bp % HEAD_PAIRS, (bp // HEAD_PAIRS) * steps + t, 0)),
                  pair_row, pair_row, pl.BlockSpec(memory_space=pl.ANY)],
        out_specs=[tok(V7X_LANES),
                   pl.BlockSpec((1, V7X_LANES, V7X_LANES), lambda bp, t: (bp, 0, 0))],
        out_shape=[jax.ShapeDtypeStruct(buf.shape, buf.dtype),
                   jax.ShapeDtypeStruct((n_batch * HEAD_PAIRS, V7X_LANES, V7X_LANES), F32)],
        input_output_aliases={11: 0},
        scratch_shapes=[pltpu.VMEM((V7X_LANES, V7X_LANES), F32)],
        compiler_params=_params(("parallel", "arbitrary")),
        name="rwkv_chunks",
    )(at, btp, ktp, rt, v, g, bonus, wbk, pc,
      lnx_g.reshape(HEAD_PAIRS, 1, V7X_LANES), lnx_b.reshape(HEAD_PAIRS, 1, V7X_LANES), buf)


def _softmax_rows(s):
    e = jnp.exp(s - jnp.max(s, axis=-1, keepdims=True))
    return e / jnp.sum(e, axis=-1, keepdims=True)


def _xattn_kernel(q_ref, k_ref, v_ref, buf_any, o_ref):
    del buf_any
    for h in range(XA_HEADS):
        sl = slice(h * XA_HD, (h + 1) * XA_HD)
        s = lax.dot_general(q_ref[:, sl].astype(BF16), k_ref[:, sl].astype(BF16), (((1,), (1,)), ((), ())),
                            preferred_element_type=F32) * (XA_HD ** -0.5)
        pr = _softmax_rows(s)
        o_ref[:, sl] = jnp.dot(pr.astype(BF16), v_ref[:, sl].astype(BF16),
                               preferred_element_type=F32).astype(o_ref.dtype)


def xattn_prompt(z_q, kv, buf, n_batch, seq_len, *, tq=512):
    steps = seq_len // tq
    return pl.pallas_call(
        _xattn_kernel,
        grid=(n_batch, steps),
        in_specs=[pl.BlockSpec((tq, XA_D), lambda b, t: (b * steps + t, 0)),
                  pl.BlockSpec((N_MEM, XA_D), lambda b, t: (b, 0)),
                  pl.BlockSpec((N_MEM, XA_D), lambda b, t: (b, 1)),
                  pl.BlockSpec(memory_space=pl.ANY)],
        out_specs=pl.BlockSpec((tq, XA_D), lambda b, t: (b * steps + t, 0)),
        out_shape=jax.ShapeDtypeStruct(buf.shape, buf.dtype),
        input_output_aliases={3: 0},
        compiler_params=_params(("parallel", "parallel")),
        name="xattn_prompt",
    )(z_q, kv, kv, buf)


def _sample_mix_kernel(zg_ref, zr_ref, shift_ref, lng_ref, lnb_ref, gw_ref, gb_ref,
                       mu_ref, w0_ref, w2_ref, a0_ref, a2_ref, g2_ref, kkp_ref, ka_ref, rk_ref, og_any,
                       og_ref, gv_ref, wt_ref, kkt_ref, bt_ref, kt_ref, rt_ref, vt_ref, bonus_ref, g_ref):
    del og_any
    ge = _gelu(zg_ref[...])
    v = _layernorm(ge[:, GMLP_D:], lng_ref[...], lnb_ref[...])
    gv_ref[...] = v
    og_ref[...] = (ge[:, :GMLP_D] * (v * gw_ref[...] + gb_ref[...])).astype(og_ref.dtype)

    z = zr_ref[...]
    zs = z + (shift_ref[...] - z) * mu_ref[...]
    mixed = _rwkv_mix(zs, w0_ref[...], w2_ref[...], a0_ref[...], a2_ref[...], g2_ref[...], kkp_ref[...],
                      ka_ref[...], rk_ref[...], _head_block_ones())
    for p, (r, k2, vv, logw, kkn, a, g, bonus) in enumerate(mixed):
        sl = slice(p * V7X_LANES, (p + 1) * V7X_LANES)
        wt_ref[sl, :] = jnp.exp(logw).T
        kkt_ref[sl, :] = kkn.T
        bt_ref[sl, :] = (kkn * a).T
        kt_ref[sl, :] = k2.T
        rt_ref[sl, :] = r.T
        vt_ref[sl, :] = vv.T
        bonus_ref[:, sl] = bonus
        g_ref[:, sl] = g


def sample_mix(z_g, z_r, row0, shift, o_g, ln_g, ln_b, gw, gb, mu, w0, w2p, a0, a2p, g2, kkp, ka, rk):
    n = shift.shape[0]
    blk = row0 // n
    tok = lambda w: pl.BlockSpec((n, w), lambda i: (blk, 0))
    loc = lambda w: pl.BlockSpec((n, w), lambda i: (0, 0))
    row = lambda w: pl.BlockSpec((1, w), lambda i: (0, 0))
    full2 = lambda a, b: pl.BlockSpec((a, b), lambda i: (0, 0))
    chan = jax.ShapeDtypeStruct((RWKV_D, n), F32)
    return pl.pallas_call(
        _sample_mix_kernel,
        grid=(1,),
        in_specs=[tok(GMLP_COLS), tok(RWKV_COLS), loc(RWKV_COLS),
                  row(GMLP_D), row(GMLP_D), row(GMLP_D), row(GMLP_D),
                  row(RWKV_COLS), row(RWKV_D), full2(LORA_W + LORA_A, RWKV_D), row(RWKV_D),
                  full2(LORA_W + LORA_A, RWKV_D), full2(LORA_G, RWKV_D), row(RWKV_D), row(RWKV_D), row(RWKV_D),
                  pl.BlockSpec(memory_space=pl.ANY)],
        out_specs=[tok(GMLP_D), loc(GMLP_D)] + [full2(RWKV_D, n)] * 6 + [loc(RWKV_D), loc(RWKV_D)],
        out_shape=[jax.ShapeDtypeStruct(o_g.shape, o_g.dtype), jax.ShapeDtypeStruct((n, GMLP_D), F32)]
                  + [chan] * 6 + [jax.ShapeDtypeStruct((n, RWKV_D), F32)] * 2,
        input_output_aliases={16: 0},
        compiler_params=_params(("arbitrary",)),
        name="sample_mix",
    )(z_g, z_r, shift, ln_g.reshape(1, -1), ln_b.reshape(1, -1), gw, gb, mu, w0, w2p, a0, a2p, g2, kkp, ka, rk, o_g)


def _sample_state_kernel(s_ref, w_ref, kk_ref, b_ref, k_ref, r_ref, v_ref, snew_ref, y_ref):
    n = s_ref.shape[0]
    st = s_ref[...].T.reshape(RWKV_HD, RWKV_HD, n)
    sa = jnp.sum(st * (-kk_ref[...])[None], axis=1, keepdims=True)
    st_new = st * w_ref[...][None] + sa * b_ref[...][None] + v_ref[...] * k_ref[...][None]
    y_ref[...] = jnp.sum(st_new * r_ref[...][None], axis=1, keepdims=True)
    snew_ref[...] = st_new.reshape(RWKV_HD * RWKV_HD, n).T


def sample_state(s0, wt, kkt, bt, kt, rt, vt):
    n = s0.shape[0]
    hd2 = RWKV_HD * RWKV_HD
    state = pl.BlockSpec((n, hd2), lambda h: (0, h))
    keyvec = pl.BlockSpec((RWKV_HD, n), lambda h: (h, 0))
    valvec = pl.BlockSpec((RWKV_HD, 1, n), lambda h: (h, 0, 0))
    return pl.pallas_call(
        _sample_state_kernel,
        grid=(RWKV_HEADS,),
        in_specs=[state, keyvec, keyvec, keyvec, keyvec, keyvec, valvec],
        out_specs=[state, valvec],
        out_shape=[jax.ShapeDtypeStruct((n, RWKV_HEADS * hd2), F32),
                   jax.ShapeDtypeStruct((RWKV_D, 1, n), F32)],
        compiler_params=_params(("parallel",)),
        name="sample_state",
    )(s0.reshape(n, RWKV_HEADS * hd2), wt, kkt, bt, kt, rt, vt.reshape(RWKV_D, 1, n))


def _sample_rwkv_out_kernel(yt_ref, bonus_ref, g_ref, lng_ref, lnb_ref, or_any, or_ref):
    del or_any
    y = yt_ref[...].T
    ones_bd = _head_block_ones()
    for p in range(HEAD_PAIRS):
        sl = slice(p * V7X_LANES, (p + 1) * V7X_LANES)
        o = _group_norm_gate(y[:, sl], bonus_ref[:, sl], g_ref[:, sl], lng_ref[p], lnb_ref[p], ones_bd)
        or_ref[p] = o.astype(or_ref.dtype)


def sample_rwkv_out(yt, bonus, g, lnx_g, lnx_b, o_r, row0):
    n = bonus.shape[0]
    full2 = lambda a, b: pl.BlockSpec((a, b), lambda i: (0, 0))
    pair_row = pl.BlockSpec((HEAD_PAIRS, 1, V7X_LANES), lambda i: (0, 0, 0))
    return pl.pallas_call(
        _sample_rwkv_out_kernel,
        grid=(1,),
        in_specs=[full2(RWKV_D, n), full2(n, RWKV_D), full2(n, RWKV_D), pair_row, pair_row,
                  pl.BlockSpec(memory_space=pl.ANY)],
        out_specs=pl.BlockSpec((HEAD_PAIRS, n, V7X_LANES), lambda i: (0, row0 // n, 0)),
        out_shape=jax.ShapeDtypeStruct(o_r.shape, o_r.dtype),
        input_output_aliases={5: 0},
        compiler_params=_params(("arbitrary",)),
        name="sample_rwkv_out",
    )(yt, bonus, g, lnx_g.reshape(HEAD_PAIRS, 1, V7X_LANES), lnx_b.reshape(HEAD_PAIRS, 1, V7X_LANES), o_r)


def _xattn_sample_kernel(q_ref, ck_ref, cv_ref, ox_any, ox_ref, *, sb):
    del ox_any
    q = q_ref[...]
    rows = N_MEM * XA_HEADS
    hrow = lax.broadcasted_iota(jnp.int32, (8, rows), 0) % XA_HEADS
    hcol = lax.broadcasted_iota(jnp.int32, (8, rows), 1) % XA_HEADS
    own = hrow == hcol
    h8 = lax.broadcasted_iota(jnp.int32, (8, XA_HD), 0) % XA_HEADS
    nt = (((1,), (1,)), ((), ()))
    outs = []
    for s in range(sb):
        qm = jnp.zeros((8, XA_HD), F32)
        for h in range(XA_HEADS):
            qm = jnp.where(h8 == h, q[s:s + 1, h * XA_HD:(h + 1) * XA_HD], qm)
        sc = lax.dot_general(qm.astype(BF16), ck_ref[s].astype(BF16), nt,
                             preferred_element_type=F32) * (XA_HD ** -0.5)
        pr = _softmax_rows(jnp.where(own, sc, NEG_INF))
        res = jnp.dot(pr.astype(BF16), cv_ref[s].astype(BF16), preferred_element_type=F32)
        outs.append(jnp.concatenate([res[h:h + 1, :] for h in range(XA_HEADS)], axis=1))
    ox_ref[...] = jnp.concatenate(outs, axis=0).astype(ox_ref.dtype)


def xattn_sample(z_q, ck, cv, o_x, row0, *, sb=8):
    n = ck.shape[0]
    off = row0 // sb
    tok = pl.BlockSpec((sb, XA_D), lambda i: (i + off, 0))
    cache = pl.BlockSpec((sb, N_MEM * XA_HEADS, XA_HD), lambda i: (i, 0, 0))
    return pl.pallas_call(
        functools.partial(_xattn_sample_kernel, sb=sb),
        grid=(n // sb,),
        in_specs=[tok, cache, cache, pl.BlockSpec(memory_space=pl.ANY)],
        out_specs=tok,
        out_shape=jax.ShapeDtypeStruct(o_x.shape, o_x.dtype),
        input_output_aliases={3: 0},
        compiler_params=_params(("arbitrary",)),
        name="xattn_sample",
    )(z_q, ck, cv, o_x)


def _merge_kernel(og_ref, or_ref, ox_ref, g0_ref, g1_ref, g2_ref, wg_ref, wr_ref, wx_ref, o_ref,
                  wg_bf, wr_bf, wx_bf):
    @pl.when(pl.program_id(1) == 0)
    def _():
        wg_bf[...] = wg_ref[...].astype(BF16)
        wr_bf[...] = wr_ref[...].astype(BF16)
        wx_bf[...] = wx_ref[...].astype(BF16)

    up_g = jnp.dot(og_ref[...], wg_bf[...], preferred_element_type=F32)
    up_r = jnp.dot(or_ref[0], wr_bf[0:V7X_LANES, :], preferred_element_type=F32)
    for p in range(1, HEAD_PAIRS):
        up_r = up_r + jnp.dot(or_ref[p], wr_bf[p * V7X_LANES:(p + 1) * V7X_LANES, :], preferred_element_type=F32)
    up_x = jnp.dot(ox_ref[...], wx_bf[...], preferred_element_type=F32)
    gate = lambda ref: _sigmoid(ref[...].astype(F32))
    merged = gate(g0_ref) * up_g + gate(g1_ref) * up_r + gate(g2_ref) * up_x
    o_ref[...] = merged.astype(o_ref.dtype)


def merge(o_g, o_r, o_x, z_gate, w_up_g, w_up_r, w_up_x, *, tb, nb=512):
    m = o_g.shape[0]
    nblk = D_MODEL // nb
    gate = lambda b: pl.BlockSpec((tb, nb), lambda j, i: (i, b * nblk + j))
    wspec = lambda k: pl.BlockSpec((k, nb), lambda j, i: (0, j))
    return pl.pallas_call(
        _merge_kernel,
        grid=(nblk, m // tb),
        in_specs=[pl.BlockSpec((tb, GMLP_D), lambda j, i: (i, 0)),
                  pl.BlockSpec((HEAD_PAIRS, tb, V7X_LANES), lambda j, i: (0, i, 0)),
                  pl.BlockSpec((tb, XA_D), lambda j, i: (i, 0)),
                  gate(0), gate(1), gate(2), wspec(GMLP_D), wspec(RWKV_D), wspec(XA_D)],
        out_specs=pl.BlockSpec((tb, nb), lambda j, i: (i, j)),
        out_shape=jax.ShapeDtypeStruct((m, D_MODEL), BF16),
        scratch_shapes=[pltpu.VMEM((GMLP_D, nb), BF16), pltpu.VMEM((RWKV_D, nb), BF16),
                        pltpu.VMEM((XA_D, nb), BF16)],
        compiler_params=_params(("arbitrary", "arbitrary")),
        name="merge",
    )(o_g, o_r, o_x, z_gate, z_gate, z_gate, w_up_g, w_up_r, w_up_x)


def _extract_top(src_ref, work_ref, rank_ref, vals_ref, n_rows):
    width = work_ref.shape[1]
    riota = lax.broadcasted_iota(jnp.int32, (n_rows, V7X_LANES), 0)
    kiota = lax.broadcasted_iota(jnp.int32, (TOPK, V7X_LANES), 0)

    def run(break_ties):
        work_ref[...] = src_ref[...]
        vals_ref[...] = jnp.zeros_like(vals_ref)

        def body(p, carry):
            taken = TAKEN_BASE - jnp.asarray(p, F32) * TAKEN_STEP
            for c in range(width // V7X_LANES):
                sl = slice(c * V7X_LANES, (c + 1) * V7X_LANES)
                w = work_ref[:, sl]
                m = jnp.max(w, axis=0, keepdims=True)
                if break_ties:
                    idx = jnp.min(jnp.where(w == m, riota, n_rows), axis=0, keepdims=True)
                    hit = riota == idx
                else:
                    hit = w == m
                work_ref[:, sl] = jnp.where(hit, taken, w)
                vals_ref[:, sl] = jnp.where(kiota == p, m, vals_ref[:, sl])
            return carry

        lax.fori_loop(0, TOPK, body, 0)

    run(False)
    picked = jnp.sum(jnp.where(work_ref[...] < TAKEN_BELOW, 1.0, 0.0), axis=0, keepdims=True)
    tied = jnp.max(picked) > TOPK

    @pl.when(tied)
    def _():
        run(True)

    w = work_ref[...]
    rank = jnp.where(w < TAKEN_BELOW, w * (-1.0 / TAKEN_STEP) + (TAKEN_BASE / TAKEN_STEP), float(TOPK))
    rank_ref[...] = rank.astype(jnp.int32)


_CAND_COUNT = tuple(TOPK // (a + 1) for a in range(TOPK))
_CAND_START = tuple(sum(_CAND_COUNT[:a]) for a in range(TOPK))
_CAND_ROWS = -(-sum(_CAND_COUNT) // 8) * 8


def _peer_topk_kernel(q_ref, keys_ref, r2_ref, lim_ref, e1_ref, e2_ref,
                      s_scr, work_scr, rank_scr, vals_scr, cand_scr, cwork_scr, crank_scr, cvals_scr, *, tbk):
    nt = (((1,), (1,)), ((), ()))
    for h in range(PEER_HEADS):
        for c in range(2):
            qcol = (2 * h + c) * PEER_DH
            col = (2 * h + c) * tbk
            k_hi, k_lo = _split_bf16(keys_ref[h, c])
            q_hi, q_lo = _split_bf16(q_ref[:, qcol:qcol + PEER_DH])
            sc = (lax.dot_general(k_hi, q_hi, nt, preferred_element_type=F32)
                  + lax.dot_general(k_hi, q_lo, nt, preferred_element_type=F32)
                  + lax.dot_general(k_lo, q_hi, nt, preferred_element_type=F32))
            s_scr[:, col:col + tbk] = sc
    _extract_top(s_scr, work_scr, rank_scr, vals_scr, N_KEYS)

    crow = lax.broadcasted_iota(jnp.int32, (_CAND_ROWS, tbk), 0)
    seg = jnp.full((_CAND_ROWS, tbk), TOPK, jnp.int32)
    for a in reversed(range(TOPK)):
        seg = jnp.where(crow < _CAND_START[a] + _CAND_COUNT[a], jnp.minimum(seg, a), seg)
    pad = jnp.zeros((_CAND_ROWS - TOPK, tbk), F32)
    for h in range(PEER_HEADS):
        v1 = vals_scr[:, (2 * h) * tbk:(2 * h + 1) * tbk]
        v2 = jnp.concatenate([vals_scr[:, (2 * h + 1) * tbk:(2 * h + 2) * tbk], pad], axis=0)
        cand = jnp.full((_CAND_ROWS, tbk), CAND_PAD, F32)
        for a in range(TOPK):
            shifted = v2 if _CAND_START[a] == 0 else pltpu.roll(v2, _CAND_START[a], axis=0)
            cand = jnp.where(seg == a, v1[a:a + 1, :] + shifted, cand)
        cand_scr[:, h * tbk:(h + 1) * tbk] = cand
    _extract_top(cand_scr, cwork_scr, crank_scr, cvals_scr, _CAND_ROWS)

    for h in range(PEER_HEADS):
        hs = slice(h * tbk, (h + 1) * tbk)
        s1 = slice((2 * h) * tbk, (2 * h + 1) * tbk)
        s2 = slice((2 * h + 1) * tbk, (2 * h + 2) * tbk)
        cvals = cvals_scr[:, hs]
        z = jnp.sum(jnp.exp(cvals - cvals[0:1, :]), axis=0, keepdims=True)
        chosen = crank_scr[:, hs] < TOPK
        rank1 = rank_scr[:, s1]
        lim = jnp.zeros((N_KEYS, tbk), F32)
        for a in range(TOPK):
            count = jnp.sum(jnp.where(jnp.logical_and(chosen, seg == a), 1.0, 0.0), axis=0, keepdims=True)
            lim = jnp.where(rank1 == a, count, lim)
        lim_ref[h] = lim
        r2_ref[h] = rank_scr[:, s2].astype(F32).astype(r2_ref.dtype)
        e1_ref[h] = jnp.exp(s_scr[:, s1] - vals_scr[0:1, s1]) / z
        e2_ref[h] = jnp.exp(s_scr[:, s2] - vals_scr[0:1, s2]).astype(e2_ref.dtype)


def peer_topk(q, keys, *, tbk=128):
    m = q.shape[0]
    out = pl.BlockSpec((PEER_HEADS, N_KEYS, tbk), lambda i: (0, 0, i))
    shape = lambda dt: jax.ShapeDtypeStruct((PEER_HEADS, N_KEYS, m), dt)
    wide = 2 * PEER_HEADS * tbk
    return pl.pallas_call(
        functools.partial(_peer_topk_kernel, tbk=tbk),
        grid=(m // tbk,),
        in_specs=[pl.BlockSpec((tbk, PEER_HEADS * 2 * PEER_DH), lambda i: (i, 0)),
                  pl.BlockSpec((PEER_HEADS, 2, N_KEYS, PEER_DH), lambda i: (0, 0, 0, 0))],
        out_specs=[out] * 4,
        out_shape=[shape(BF16), shape(F32), shape(F32), shape(BF16)],
        scratch_shapes=[pltpu.VMEM((N_KEYS, wide), F32), pltpu.VMEM((N_KEYS, wide), F32),
                        pltpu.VMEM((N_KEYS, wide), jnp.int32), pltpu.VMEM((TOPK, wide), F32),
                        pltpu.VMEM((_CAND_ROWS, PEER_HEADS * tbk), F32),
                        pltpu.VMEM((_CAND_ROWS, PEER_HEADS * tbk), F32),
                        pltpu.VMEM((_CAND_ROWS, PEER_HEADS * tbk), jnp.int32),
                        pltpu.VMEM((TOPK, PEER_HEADS * tbk), F32)],
        compiler_params=_params(("parallel",)),
        name="peer_topk",
    )(q, keys)


PEER_SUB = 2 * N_KEYS


def _peer_dense_kernel(xt_ref, r2_ref, lim_ref, e1_ref, e2_ref, u_ref, v_ref, o_ref, coef_a, coef_b, ht_scr,
                       *, eb, n_blocks):
    e = pl.program_id(1)
    blk = jnp.minimum(e, n_blocks - 1)
    n_sub = eb // PEER_SUB
    d_sub = o_ref.shape[1] // n_sub

    def hidden(s):
        ht_scr[s % 2] = jnp.dot(u_ref[s * PEER_SUB:(s + 1) * PEER_SUB, :], xt_ref[...],
                                preferred_element_type=F32)

    def step(prev_ref, next_ref):
        hidden(0)
        for s in range(n_sub):
            ht = ht_scr.at[s % 2]
            cols = slice(s * d_sub, (s + 1) * d_sub)
            if prev_ref is not None:
                o_ref[:, cols] += lax.dot_general(prev_ref[...], v_ref[:, cols], (((0,), (0,)), ((), ())),
                                                  preferred_element_type=F32)
            if s + 1 < n_sub:
                hidden(s + 1)
            for ii in range(PEER_SUB // N_KEYS):
                i = blk * (eb // N_KEYS) + s * (PEER_SUB // N_KEYS) + ii
                row = s * PEER_SUB + ii * N_KEYS
                gate = None
                for h in range(PEER_HEADS):
                    lim = lim_ref[h, pl.ds(i, 1), :].astype(BF16)
                    e1 = e1_ref[h, pl.ds(i, 1), :].astype(BF16)
                    term = jnp.where(r2_ref[h] < lim, e2_ref[h] * e1, jnp.zeros((), BF16))
                    gate = term if gate is None else gate + term
                next_ref[row:row + N_KEYS, :] = gate * _gelu_to_bf16(ht[ii * N_KEYS:(ii + 1) * N_KEYS, :])

    @pl.when(e == 0)
    def _():
        o_ref[...] = jnp.zeros_like(o_ref)
        step(None, coef_a)

    @pl.when(jnp.logical_and(e % 2 == 0, jnp.logical_and(e > 0, e < n_blocks)))
    def _():
        step(coef_b, coef_a)

    @pl.when(e % 2 == 1)
    def _():
        step(coef_a, coef_b)

    @pl.when(e == n_blocks)
    def _():
        o_ref[...] += lax.dot_general(coef_b[...], v_ref[...], (((0,), (0,)), ((), ())),
                                      preferred_element_type=F32)


def peer_dense(xt, r2, lim, e1, e2, u_bf, v_bf, *, tbl, eb=1024):
    d, m = xt.shape
    n_blocks = u_bf.shape[0] // eb
    assert n_blocks % 2 == 0
    head = pl.BlockSpec((PEER_HEADS, N_KEYS, tbl), lambda t, e: (0, 0, t))
    return pl.pallas_call(
        functools.partial(_peer_dense_kernel, eb=eb, n_blocks=n_blocks),
        grid=(m // tbl, n_blocks + 1),
        in_specs=[pl.BlockSpec((d, tbl), lambda t, e: (0, t)), head, head, head, head,
                  pl.BlockSpec((eb, d), lambda t, e: (jnp.minimum(e, n_blocks - 1), 0)),
                  pl.BlockSpec((eb, d), lambda t, e: (jnp.maximum(e - 1, 0), 0))],
        out_specs=pl.BlockSpec((tbl, d), lambda t, e: (t, 0)),
        out_shape=jax.ShapeDtypeStruct((m, d), F32),
        scratch_shapes=[pltpu.VMEM((eb, tbl), BF16), pltpu.VMEM((eb, tbl), BF16),
                        pltpu.VMEM((2, PEER_SUB, tbl), F32)],
        compiler_params=_params(("parallel", "arbitrary")),
        name="peer_dense",
    )(xt, r2, lim, e1, e2, u_bf, v_bf)


def _final_kernel(h_ref, p_ref, g_ref, o_ref):
    x = h_ref[...] + p_ref[...]
    o_ref[...] = x * lax.rsqrt(jnp.mean(x * x, axis=-1, keepdims=True) + RMS_EPS) * g_ref[...]


def final_norm(h, peer, g, row0, n_rows, *, tb):
    d = h.shape[1]
    off = row0 // tb
    rows = pl.BlockSpec((tb, d), lambda i: (i + off, 0))
    return pl.pallas_call(
        _final_kernel,
        grid=(n_rows // tb,),
        in_specs=[rows, rows, pl.BlockSpec((1, d), lambda i: (0, 0))],
        out_specs=pl.BlockSpec((tb, d), lambda i: (i, 0)),
        out_shape=jax.ShapeDtypeStruct((n_rows, d), F32),
        compiler_params=_params(("parallel",)),
        name="final_norm",
    )(h, peer, g.reshape(1, d))


def kernel(x_prompt, x_sample, mem_prompt, state_shift, state_wkv, cache_mem_k, cache_mem_v, ln1_g, w_in, gmlp_ln_g, gmlp_ln_b, gmlp_ws, gmlp_bs, rwkv_mu, rwkv_w0, rwkv_w2, rwkv_a0, rwkv_a2, rwkv_g2, rwkv_kk, rwkv_ka, rwkv_rk, rwkv_lnx_g, rwkv_lnx_b, mem_norm_g, w_mem_kv, w_up_g, w_up_r, w_up_x, w_out, ln2_g, peer_wq, peer_keys, peer_u, peer_v, final_g):
    depth = w_in.shape[0]
    assert depth == 1, "single-layer step"
    l = 0
    n_batch, seq_len, d = x_prompt.shape
    n_dec = x_sample.shape[0]
    n_prompt = n_batch * seq_len
    m = n_prompt + n_dec
    tb_small = _largest_divisor(m, (640, 128))
    tb_big = _largest_divisor(m, (1664, 640, 128))

    x = jnp.concatenate([x_prompt.reshape(n_prompt, d), x_sample.reshape(n_dec, d)], axis=0)
    xn = rmsnorm(x, ln1_g[l], tb=tb_small, out_dtype=BF16)
    proj = functools.partial(matmul, xn, w_in[l], tb=tb_big, nb=512)
    z_g = proj(col_off=0, n_cols=GMLP_COLS, name="proj_gmlp")
    z_r = proj(col_off=GMLP_COLS, n_cols=RWKV_COLS, name="proj_rwkv")
    z_q = proj(col_off=GMLP_COLS + RWKV_COLS, n_cols=XA_D, name="proj_xattn")
    z_gate = proj(col_off=GMLP_COLS + RWKV_COLS + XA_D, n_cols=GATE_COLS, out_dtype=BF16, name="proj_gate")

    memn = rmsnorm(mem_prompt.reshape(n_batch * N_MEM, d), mem_norm_g[l], tb=256, out_dtype=BF16)
    kv = matmul(memn, w_mem_kv[l], tb=N_MEM, nb=512, name="proj_mem_kv")
    p_mk = kv[:, :XA_D].reshape(1, n_batch, N_MEM, XA_HEADS, XA_HD)
    p_mv = kv[:, XA_D:].reshape(1, n_batch, N_MEM, XA_HEADS, XA_HD)

    row = lambda a: a.reshape(1, -1)
    zeros_lora = jnp.zeros((LORA_W, RWKV_D), F32)
    w2p = jnp.concatenate([rwkv_w2[l], zeros_lora], axis=0).astype(BF16)
    a2p = jnp.concatenate([zeros_lora, rwkv_a2[l]], axis=0).astype(BF16)
    rw = (row(rwkv_mu[l]), row(rwkv_w0[l]), w2p, row(rwkv_a0[l]), a2p, rwkv_g2[l].astype(BF16),
          row(rwkv_kk[l]), row(rwkv_ka[l]), row(rwkv_rk[l]))

    o_g = gmlp_prompt(z_g, jnp.zeros((m, GMLP_D), BF16), n_prompt, gmlp_ln_g[l], gmlp_ln_b[l], gmlp_ws[l],
                      gmlp_bs[l])
    prep = rwkv_prep(z_r, n_prompt, seq_len, *rw)
    o_r, st = rwkv_chunks(prep, jnp.zeros((HEAD_PAIRS, m, V7X_LANES), BF16), n_batch, seq_len,
                          rwkv_lnx_g[l], rwkv_lnx_b[l])
    o_x = xattn_prompt(z_q, kv, jnp.zeros((m, XA_D), BF16), n_batch, seq_len)

    gw = jnp.repeat(gmlp_ws[l][:, 0, 0], CHUNK).reshape(1, GMLP_D)
    gb = jnp.repeat(gmlp_bs[l][:, 0], CHUNK).reshape(1, GMLP_D)
    o_g, s_gv, wt, kkt, bt, kt, rt, vt, bonus_s, g_s = sample_mix(
        z_g, z_r, n_prompt, state_shift[l], o_g, gmlp_ln_g[l], gmlp_ln_b[l], gw, gb, *rw)
    s_wkv, y_s = sample_state(state_wkv[l], wt, kkt, bt, kt, rt, vt)
    o_r = sample_rwkv_out(y_s.reshape(RWKV_D, n_dec), bonus_s, g_s, rwkv_lnx_g[l], rwkv_lnx_b[l], o_r, n_prompt)
    o_x = xattn_sample(z_q, cache_mem_k[l].reshape(n_dec, N_MEM * XA_HEADS, XA_HD),
                       cache_mem_v[l].reshape(n_dec, N_MEM * XA_HEADS, XA_HD), o_x, n_prompt)

    merged = merge(o_g, o_r, o_x, z_gate, w_up_g[l], w_up_r[l], w_up_x[l], tb=tb_small)
    h = matmul(merged, w_out[l], tb=tb_big, nb=512, residual=x, name="proj_out")

    hn, hn_t = rmsnorm(h, ln2_g[l], tb=tb_small, out_dtype=BF16, transposed=True)
    q = matmul(hn, peer_wq[l], tb=tb_big, nb=512, name="proj_peer_q")
    r2, lim, e1, e2 = peer_topk(q, peer_keys[l])
    peer = peer_dense(hn_t, r2, lim, e1, e2, peer_u[l].astype(BF16), peer_v[l].astype(BF16), tbl=tb_small)
    y_prompt = final_norm(h, peer, final_g, 0, n_prompt, tb=512)
    y_sample = final_norm(h, peer, final_g, n_prompt, n_dec, tb=n_dec)

    st = st.reshape(n_batch, HEAD_PAIRS, 2, RWKV_HD, 2, RWKV_HD)
    p_wkv = jnp.stack([st[:, :, 0, :, 0, :], st[:, :, 1, :, 1, :]], axis=2)
    p_wkv = jnp.swapaxes(p_wkv, -1, -2).reshape(1, n_batch, RWKV_HEADS, RWKV_HD, RWKV_HD)
    p_shift = z_r[seq_len - 1:n_prompt:seq_len][None]
    s_shift = z_r[n_prompt:][None]
    s_wkv = s_wkv.reshape(1, n_dec, RWKV_HEADS, RWKV_HD, RWKV_HD)
    return (y_prompt.reshape(n_batch, seq_len, d), y_sample.reshape(n_dec, 1, d), p_mk, p_mv,
            p_shift, p_wkv, s_shift, s_wkv, s_gv.reshape(1, n_dec, 1, GMLP_D))
```

```python
import functools

import jax
import jax.numpy as jnp
from jax import lax
from jax.experimental import pallas as pl
from jax.experimental.pallas import tpu as pltpu

F32 = jnp.float32
BF16 = jnp.bfloat16

D_MODEL = 2048
CHUNK = 128
GMLP_GROUPS = 6
GMLP_D = 768
RWKV_HEADS = 12
RWKV_HD = 64
RWKV_D = 768
LORA_W = 64
LORA_A = 64
LORA_G = 128
RWKV_COLS = 2560
XA_HEADS = 4
XA_HD = 128
XA_D = 512
N_MEM = 256
GMLP_COLS = 2 * GMLP_D
GATE_COLS = 3 * D_MODEL
PEER_HEADS = 8
N_KEYS = 128
PEER_DH = 128
TOPK = 16
RMS_EPS = 1e-6
LN_EPS = 1e-5
GN_EPS = 64e-5

V7X_LANES = 128
V7X_VMEM_LIMIT_BYTES = 56 * 1024 * 1024

HEAD_PAIRS = RWKV_HEADS // 2
RWKV_CHUNK = 64
RWKV_CHUNKS_PER_STEP = 32
NEG_INF = float("-inf")
CAND_PAD = -(2.0 ** 100)
TAKEN_BELOW = -(2.0 ** 110)
TAKEN_BASE = -(2.0 ** 120)
TAKEN_STEP = 2.0 ** 115


def _params(semantics):
    return pltpu.CompilerParams(dimension_semantics=semantics, vmem_limit_bytes=V7X_VMEM_LIMIT_BYTES)


def _largest_divisor(m, candidates):
    return next(c for c in candidates if m % c == 0)


def _gelu(x):
    return 0.5 * x * (1.0 + jnp.tanh(0.7978845608028654 * (x + 0.044715 * (x * x * x))))


def _gelu_to_bf16(x):
    inner = (0.7978845608028654 * x) * (1.0 + 0.044715 * (x * x))
    xb = x.astype(BF16)
    return (0.5 * xb) * (1.0 + jnp.tanh(inner.astype(BF16)))


def _split_bf16(x):
    hi = x.astype(BF16)
    return hi, (x - hi.astype(F32)).astype(BF16)


def _split3_bf16(x):
    hi = x.astype(BF16)
    rest = x - hi.astype(F32)
    mid = rest.astype(BF16)
    return hi, mid, (rest - mid.astype(F32)).astype(BF16)


def _sigmoid(x):
    return 0.5 + 0.5 * jnp.tanh(0.5 * x)


def _rmsnorm_kernel(x_ref, g_ref, *o_refs, transposed):
    x = x_ref[...]
    y = x * lax.rsqrt(jnp.mean(x * x, axis=-1, keepdims=True) + RMS_EPS) * g_ref[...]
    o_refs[0][...] = y.astype(o_refs[0].dtype)
    if transposed:
        o_refs[1][...] = y.T.astype(o_refs[1].dtype)


def rmsnorm(x, g, *, tb, out_dtype, transposed=False):
    m, d = x.shape
    out_shape = [jax.ShapeDtypeStruct((m, d), out_dtype)]
    out_specs = [pl.BlockSpec((tb, d), lambda i: (i, 0))]
    if transposed:
        out_shape.append(jax.ShapeDtypeStruct((d, m), out_dtype))
        out_specs.append(pl.BlockSpec((d, tb), lambda i: (0, i)))
    res = pl.pallas_call(
        functools.partial(_rmsnorm_kernel, transposed=transposed),
        grid=(m // tb,),
        in_specs=[pl.BlockSpec((tb, d), lambda i: (i, 0)), pl.BlockSpec((1, d), lambda i: (0, 0))],
        out_specs=out_specs,
        out_shape=out_shape,
        compiler_params=_params(("parallel",)),
        name="rmsnorm_t" if transposed else "rmsnorm",
    )(x, g.reshape(1, d))
    return res if transposed else res[0]


def _matmul_kernel(a_ref, w_ref, *rest, has_residual):
    if has_residual:
        r_ref, o_ref, wbf_ref = rest
    else:
        o_ref, wbf_ref = rest

    @pl.when(pl.program_id(1) == 0)
    def _():
        wbf_ref[...] = w_ref[...].astype(BF16)

    acc = jnp.dot(a_ref[...], wbf_ref[...], preferred_element_type=F32)
    if has_residual:
        acc = acc + r_ref[...]
    o_ref[...] = acc.astype(o_ref.dtype)


def matmul(a, w, *, tb, nb, col_off=0, n_cols=None, residual=None, out_dtype=F32, name="matmul"):
    m, k = a.shape
    n = w.shape[1] if n_cols is None else n_cols
    off = col_off // nb
    in_specs = [pl.BlockSpec((tb, k), lambda j, i: (i, 0)),
                pl.BlockSpec((k, nb), lambda j, i: (0, j + off))]
    args = [a, w]
    if residual is not None:
        in_specs.append(pl.BlockSpec((tb, nb), lambda j, i: (i, j)))
        args.append(residual)
    return pl.pallas_call(
        functools.partial(_matmul_kernel, has_residual=residual is not None),
        grid=(n // nb, m // tb),
        in_specs=in_specs,
        out_specs=pl.BlockSpec((tb, nb), lambda j, i: (i, j)),
        out_shape=jax.ShapeDtypeStruct((m, n), out_dtype),
        scratch_shapes=[pltpu.VMEM((k, nb), BF16)],
        compiler_params=_params(("arbitrary", "arbitrary")),
        name=name,
    )(*args)


def _layernorm(v, g, b):
    vc = v - jnp.mean(v, axis=-1, keepdims=True)
    var = jnp.mean(vc * vc, axis=-1, keepdims=True)
    return vc * lax.rsqrt(var + LN_EPS) * g + b


def _gmlp_kernel(z_ref, lng_ref, lnb_ref, ws_ref, bst_ref, buf_any, o_ref):
    del buf_any
    ge = _gelu(z_ref[...])
    u = ge[:, :GMLP_D]
    v = _layernorm(ge[:, GMLP_D:], lng_ref[...], lnb_ref[...])
    row = lax.broadcasted_iota(jnp.int32, (CHUNK, CHUNK), 0)
    col = lax.broadcasted_iota(jnp.int32, (CHUNK, CHUNK), 1)
    causal = col <= row
    for g in range(GMLP_GROUPS):
        sl = slice(g * CHUNK, (g + 1) * CHUNK)
        wm = jnp.where(causal, ws_ref[g], 0.0).astype(BF16)
        mixed = jnp.dot(wm, v[:, sl].astype(BF16), preferred_element_type=F32) + bst_ref[:, g:g + 1]
        o_ref[:, sl] = (u[:, sl] * mixed).astype(o_ref.dtype)


def gmlp_prompt(z_g, buf, n_tokens, ln_g, ln_b, ws, bs):
    return pl.pallas_call(
        _gmlp_kernel,
        grid=(n_tokens // CHUNK,),
        in_specs=[pl.BlockSpec((CHUNK, GMLP_COLS), lambda i: (i, 0)),
                  pl.BlockSpec((1, GMLP_D), lambda i: (0, 0)),
                  pl.BlockSpec((1, GMLP_D), lambda i: (0, 0)),
                  pl.BlockSpec((GMLP_GROUPS, CHUNK, CHUNK), lambda i: (0, 0, 0)),
                  pl.BlockSpec((CHUNK, GMLP_GROUPS), lambda i: (0, 0)),
                  pl.BlockSpec(memory_space=pl.ANY)],
        out_specs=pl.BlockSpec((CHUNK, GMLP_D), lambda i: (i, 0)),
        out_shape=jax.ShapeDtypeStruct(buf.shape, buf.dtype),
        input_output_aliases={5: 0},
        compiler_params=_params(("parallel",)),
        name="gmlp_prompt",
    )(z_g, ln_g.reshape(1, GMLP_D), ln_b.reshape(1, GMLP_D), ws, bs.T, buf)


def _head_block_ones():
    r = lax.broadcasted_iota(jnp.int32, (V7X_LANES, V7X_LANES), 0) // RWKV_HD
    c = lax.broadcasted_iota(jnp.int32, (V7X_LANES, V7X_LANES), 1) // RWKV_HD
    return (r == c).astype(F32)


def _head_sum(x, ones_bd):
    ones = ones_bd.astype(BF16)
    return sum(jnp.dot(t, ones, preferred_element_type=F32) for t in _split3_bf16(x))


def _rwkv_mix(zs, w0, w2p, a0, a2p, g2, kkp, ka, rk, ones_bd):
    r = zs[:, 0:RWKV_D]
    k = zs[:, RWKV_D:2 * RWKV_D]
    v = zs[:, 2 * RWKV_D:3 * RWKV_D]
    lwa = zs[:, 3 * RWKV_D:3 * RWKV_D + LORA_W + LORA_A]
    lg = zs[:, 3 * RWKV_D + LORA_W + LORA_A:]
    lora_w = jnp.dot(jnp.tanh(lwa).astype(BF16), w2p, preferred_element_type=F32)
    lora_a = jnp.dot(lwa.astype(BF16), a2p, preferred_element_type=F32)
    x = -(w0 + lora_w)
    softplus = jnp.maximum(x, 0.0) + jnp.log(1.0 + jnp.exp(-jnp.abs(x)))
    logw = -jnp.exp(-softplus - 0.5)
    a = _sigmoid(a0 + lora_a)
    g = jnp.dot(_sigmoid(lg).astype(BF16), g2, preferred_element_type=F32)
    kk = k * kkp
    k2 = k * (1.0 + (a - 1.0) * ka)
    rkk = r * k2 * rk
    out = []
    for p in range(HEAD_PAIRS):
        sl = slice(p * V7X_LANES, (p + 1) * V7X_LANES)
        kk_p = kk[:, sl]
        norm = jnp.sqrt(_head_sum(kk_p * kk_p, ones_bd))
        kkn = kk_p / jnp.maximum(norm, 1e-12)
        bonus = _head_sum(rkk[:, sl], ones_bd) * v[:, sl]
        out.append((r[:, sl], k2[:, sl], v[:, sl], logw[:, sl], kkn, a[:, sl], g[:, sl], bonus))
    return out


def _rwkv_prep_kernel(z_ref, zp_ref, mu_ref, w0_ref, w2_ref, a0_ref, a2_ref, g2_ref, kkp_ref, ka_ref, rk_ref,
                      at_ref, btp_ref, ktp_ref, rt_ref, v_ref, g_ref, bonus_ref, w_ref, pc_ref, cs_scr,
                      *, tb, blocks_per_seq):
    i = pl.program_id(0)
    z = z_ref[...]
    first = (i % blocks_per_seq) == 0
    prev_row = jnp.where(first, 0.0, zp_ref[7:8, :])
    row = lax.broadcasted_iota(jnp.int32, (tb, 1), 0)
    zprev = jnp.where(row == 0, prev_row, pltpu.roll(z, 1, axis=0))
    zs = z + (zprev - z) * mu_ref[...]
    ones_bd = _head_block_ones()
    mixed = _rwkv_mix(zs, w0_ref[...], w2_ref[...], a0_ref[...], a2_ref[...], g2_ref[...], kkp_ref[...],
                      ka_ref[...], rk_ref[...], ones_bd)
    n_chunks = tb // RWKV_CHUNK
    shape3 = (n_chunks, RWKV_CHUNK, V7X_LANES)
    tr = lax.broadcasted_iota(jnp.int32, (n_chunks, RWKV_CHUNK, RWKV_CHUNK), 1)
    tc = lax.broadcasted_iota(jnp.int32, (n_chunks, RWKV_CHUNK, RWKV_CHUNK), 2)
    tri = (tc <= tr).astype(BF16)
    bdot = lambda x: lax.dot_general(tri, x, (((2,), (1,)), ((0,), (0,))), preferred_element_type=F32)
    for p, (r, k2, v, logw, kkn, a, g, bonus) in enumerate(mixed):
        hi, mid, lo = _split3_bf16(logw.reshape(shape3))
        cs3 = bdot(hi) + bdot(mid) + bdot(lo)
        total = cs3[:, RWKV_CHUNK - 1:RWKV_CHUNK, :]
        cs = cs3.reshape(tb, V7X_LANES)
        tail = jnp.exp(total - cs3).reshape(tb, V7X_LANES)
        cs_scr[...] = cs
        cs_end = cs_scr[pl.ds(RWKV_CHUNK - 1, n_chunks, stride=RWKV_CHUNK), :]
        inv_p = jnp.exp(-cs)
        at_ref[p] = (jnp.exp(cs - logw) * kkn).astype(BF16)
        btp_ref[p] = (kkn * a * tail).astype(BF16)
        ktp_ref[p] = (k2 * tail).astype(BF16)
        rt_ref[p] = (jnp.exp(cs) * r).astype(BF16)
        v_ref[p] = v.astype(BF16)
        g_ref[p] = g
        bonus_ref[p] = bonus
        w_ref[p] = jnp.concatenate([kkn * a * inv_p, k2 * inv_p], axis=-1).astype(BF16)
        pc_ref[p] = jnp.exp(cs_end)


def rwkv_prep(z_r, n_tokens, seq_len, mu, w0, w2p, a0, a2p, g2, kkp, ka, rk, *, tb=512):
    n_chunks = n_tokens // RWKV_CHUNK
    row_spec = lambda w: pl.BlockSpec((1, w), lambda i: (0, 0))
    pair_out = lambda w: pl.BlockSpec((HEAD_PAIRS, tb, w), lambda i: (0, i, 0))
    pair_shape = lambda w, dt=BF16: jax.ShapeDtypeStruct((HEAD_PAIRS, n_tokens, w), dt)
    return pl.pallas_call(
        functools.partial(_rwkv_prep_kernel, tb=tb, blocks_per_seq=seq_len // tb),
        grid=(n_tokens // tb,),
        in_specs=[pl.BlockSpec((tb, RWKV_COLS), lambda i: (i, 0)),
                  pl.BlockSpec((8, RWKV_COLS), lambda i: (jnp.maximum(i * (tb // 8) - 1, 0), 0)),
                  row_spec(RWKV_COLS), row_spec(RWKV_D),
                  pl.BlockSpec((LORA_W + LORA_A, RWKV_D), lambda i: (0, 0)),
                  row_spec(RWKV_D),
                  pl.BlockSpec((LORA_W + LORA_A, RWKV_D), lambda i: (0, 0)),
                  pl.BlockSpec((LORA_G, RWKV_D), lambda i: (0, 0)),
                  row_spec(RWKV_D), row_spec(RWKV_D), row_spec(RWKV_D)],
        out_specs=[pair_out(V7X_LANES)] * 7 + [pair_out(2 * V7X_LANES),
                   pl.BlockSpec((HEAD_PAIRS, tb // RWKV_CHUNK, V7X_LANES), lambda i: (0, i, 0))],
        out_shape=[pair_shape(V7X_LANES)] * 5 + [pair_shape(V7X_LANES, F32)] * 2 + [pair_shape(2 * V7X_LANES),
                   jax.ShapeDtypeStruct((HEAD_PAIRS, n_chunks, V7X_LANES), F32)],
        scratch_shapes=[pltpu.VMEM((tb, V7X_LANES), F32)],
        compiler_params=_params(("parallel",)),
        name="rwkv_prep",
    )(z_r, z_r, mu, w0, w2p, a0, a2p, g2, kkp, ka, rk)


def _bmm(x, y):
    return lax.dot_general(x.astype(BF16), y.astype(BF16), (((2,), (1,)), ((0,), (0,))),
                           preferred_element_type=F32)


def _bmm_nt(x, y):
    return lax.dot_general(x.astype(BF16), y.astype(BF16), (((2,), (2,)), ((0,), (0,))),
                           preferred_element_type=F32)


def _mm_tn(x, y, precision=None):
    return lax.dot_general(x, y, (((0,), (0,)), ((), ())), preferred_element_type=F32, precision=precision)


def _group_norm_gate(y, bonus, g, lng, lnb, ones_bd):
    mean = _head_sum(y, ones_bd) * (1.0 / RWKV_HD)
    yc = y - mean
    var = _head_sum(yc * yc, ones_bd) * (1.0 / RWKV_HD)
    return (yc * lax.rsqrt(var + GN_EPS) * lng + lnb + bonus) * g


def _rwkv_chunk_kernel(at_ref, btp_ref, ktp_ref, rt_ref, v_ref, g_ref, bonus_ref, w_ref, pc_ref, lng_ref,
                       lnb_ref, buf_any, o_ref, st_ref, st_scr):
    del buf_any
    nc, c, ln = RWKV_CHUNKS_PER_STEP, RWKV_CHUNK, V7X_LANES
    t = pl.program_id(1)

    @pl.when(t == 0)
    def _():
        st_scr[...] = jnp.zeros_like(st_scr)

    shape3 = (nc, c, ln)
    at = at_ref[0].reshape(shape3)
    rt = rt_ref[0].reshape(shape3)
    v = v_ref[0].reshape(shape3)
    btk = w_ref[0].reshape(nc, c, 2 * ln)
    bt, kt = btk[..., :ln], btk[..., ln:]
    lane = lax.broadcasted_iota(jnp.int32, (1, 1, ln), 2)
    head0 = lane < RWKV_HD
    head0_2 = jnp.concatenate([head0, head0], axis=-1)
    zero = jnp.zeros(shape3, F32)

    lhs = jnp.concatenate([jnp.where(head0, at, 0.0), jnp.where(head0, rt, 0.0),
                           jnp.where(head0, 0.0, at), jnp.where(head0, 0.0, rt)], axis=1)
    gram = _bmm_nt(lhs, jnp.concatenate([bt, kt], axis=1))
    gr = lax.broadcasted_iota(jnp.int32, (1, 4 * c, 2 * c), 1)
    gc = lax.broadcasted_iota(jnp.int32, (1, 4 * c, 2 * c), 2) % c
    causal = gr % c + (gr // c) % 2 > gc
    gram = jnp.where(causal, gram, 0.0)
    zv = jnp.concatenate([jnp.zeros(shape3, BF16), v], axis=1)
    first = lax.broadcasted_iota(jnp.int32, (1, 1, 2 * c), 2) < c

    xs, mvs, mwus = [], [], []
    for h in range(2):
        g_h = gram[:, 2 * h * c:(2 * h + 2) * c, :]
        lmv = _bmm(g_h, zv)
        x = jnp.concatenate([at.astype(F32), lmv[:, :c, :]], axis=-1)
        lp = jnp.where(first, -g_h[:, :c, :], 0.0)
        pad2 = jnp.zeros((nc, c, 2 * ln), F32)
        x = x + _bmm(lp, jnp.concatenate([x, pad2], axis=1))
        n = 2
        while n < c:
            lp = _bmm(lp, jnp.concatenate([lp, zero], axis=1))
            x = x + _bmm(lp, jnp.concatenate([x, pad2], axis=1))
            n *= 2
        xs.append(x)
        mvs.append(lmv[:, c:, :])
        mrb = jnp.where(first, g_h[:, c:, :], 0.0)
        mwus.append(_bmm(mrb, jnp.concatenate([x, pad2], axis=1)))
    x = jnp.where(head0_2, xs[0], xs[1])
    mv = jnp.where(head0, mvs[0], mvs[1])
    mwu = jnp.where(head0_2, mwus[0], mwus[1])
    qe = (rt.astype(F32) - mwu[..., :ln]).astype(BF16)
    yl = mv - mwu[..., ln:]

    rr = lax.broadcasted_iota(jnp.int32, (ln, ln), 0)
    cc = lax.broadcasted_iota(jnp.int32, (ln, ln), 1)
    same_head = (rr // RWKV_HD) == (cc // RWKV_HD)
    eye = rr == cc
    btp = btp_ref[0].reshape(shape3).astype(BF16)
    ktp = ktp_ref[0].reshape(shape3).astype(BF16)
    x_bf = x.astype(BF16)
    v_bf = v.astype(BF16)
    pc = pc_ref[0]
    st = st_scr[...]
    ys = []
    for j in range(nc):
        bwu = _mm_tn(btp[j], x_bf[j])
        kv = _mm_tn(ktp[j], v_bf[j])
        tr = jnp.where(same_head, jnp.where(eye, pc[j:j + 1, :], 0.0) - bwu[:, :ln], 0.0)
        ad = jnp.where(same_head, kv - bwu[:, ln:], 0.0)
        st_hi = st.astype(BF16)
        st_lo = (st - st_hi.astype(F32)).astype(BF16)
        ys.append(jnp.dot(qe[j], st_hi, preferred_element_type=F32) + yl[j])
        tr_bf = tr.astype(BF16)
        st = (jnp.dot(tr_bf, st_hi, preferred_element_type=F32)
              + jnp.dot(tr_bf, st_lo, preferred_element_type=F32) + ad)
    st_scr[...] = st
    st_ref[0] = st
    y = jnp.concatenate(ys, axis=0)
    o = _group_norm_gate(y, bonus_ref[0], g_ref[0], lng_ref[0], lnb_ref[0], _head_block_ones())
    o_ref[0] = o.astype(o_ref.dtype)


def rwkv_chunks(prep, buf, n_batch, seq_len, lnx_g, lnx_b):
    at, btp, ktp, rt, v, g, bonus, wbk, pc = prep
    rows = RWKV_CHUNK * RWKV_CHUNKS_PER_STEP
    steps = seq_len // rows
    n_tokens = n_batch * seq_len
    tok = lambda w: pl.BlockSpec((1, rows, w), lambda bp, t: (bp % HEAD_PAIRS, (bp // HEAD_PAIRS) * steps + t, 0))
    pair_row = pl.BlockSpec((1, 1, V7X_LANES), lambda bp, t: (bp % HEAD_PAIRS, 0, 0))
    return pl.pallas_call(
        _rwkv_chunk_kernel,
        grid=(n_batch * HEAD_PAIRS, steps),
        in_specs=[tok(V7X_LANES)] * 7 + [tok(2 * V7X_LANES),
                  pl.BlockSpec((1, RWKV_CHUNKS_PER_STEP, V7X_LANES),
                               lambda bp, t: (bp % HEAD_PAIRS, (bp // HEAD_PAIRS) * steps + t, 0)),
                  pair_row, pair_row, pl.BlockSpec(memory_space=pl.ANY)],
        out_specs=[tok(V7X_LANES),
                   pl.BlockSpec((1, V7X_LANES, V7X_LANES), lambda bp, t: (bp, 0, 0))],
        out_shape=[jax.ShapeDtypeStruct(buf.shape, buf.dtype),
                   jax.ShapeDtypeStruct((n_batch * HEAD_PAIRS, V7X_LANES, V7X_LANES), F32)],
        input_output_aliases={11: 0},
        scratch_shapes=[pltpu.VMEM((V7X_LANES, V7X_LANES), F32)],
        compiler_params=_params(("parallel", "arbitrary")),
        name="rwkv_chunks",
    )(at, btp, ktp, rt, v, g, bonus, wbk, pc,
      lnx_g.reshape(HEAD_PAIRS, 1, V7X_LANES), lnx_b.reshape(HEAD_PAIRS, 1, V7X_LANES), buf)


def _softmax_rows(s):
    e = jnp.exp(s - jnp.max(s, axis=-1, keepdims=True))
    return e / jnp.sum(e, axis=-1, keepdims=True)


def _xattn_kernel(q_ref, k_ref, v_ref, buf_any, o_ref):
    del buf_any
    for h in range(XA_HEADS):
        sl = slice(h * XA_HD, (h + 1) * XA_HD)
        s = lax.dot_general(q_ref[:, sl].astype(BF16), k_ref[:, sl].astype(BF16), (((1,), (1,)), ((), ())),
                            preferred_element_type=F32) * (XA_HD ** -0.5)
        pr = _softmax_rows(s)
        o_ref[:, sl] = jnp.dot(pr.astype(BF16), v_ref[:, sl].astype(BF16),
                               preferred_element_type=F32).astype(o_ref.dtype)


def xattn_prompt(z_q, kv, buf, n_batch, seq_len, *, tq=512):
    steps = seq_len // tq
    return pl.pallas_call(
        _xattn_kernel,
        grid=(n_batch, steps),
        in_specs=[pl.BlockSpec((tq, XA_D), lambda b, t: (b * steps + t, 0)),
                  pl.BlockSpec((N_MEM, XA_D), lambda b, t: (b, 0)),
                  pl.BlockSpec((N_MEM, XA_D), lambda b, t: (b, 1)),
                  pl.BlockSpec(memory_space=pl.ANY)],
        out_specs=pl.BlockSpec((tq, XA_D), lambda b, t: (b * steps + t, 0)),
        out_shape=jax.ShapeDtypeStruct(buf.shape, buf.dtype),
        input_output_aliases={3: 0},
        compiler_params=_params(("parallel", "parallel")),
        name="xattn_prompt",
    )(z_q, kv, kv, buf)


def _sample_mix_kernel(zg_ref, zr_ref, shift_ref, lng_ref, lnb_ref, gw_ref, gb_ref,
                       mu_ref, w0_ref, w2_ref, a0_ref, a2_ref, g2_ref, kkp_ref, ka_ref, rk_ref, og_any,
                       og_ref, gv_ref, wt_ref, kkt_ref, bt_ref, kt_ref, rt_ref, vt_ref, bonus_ref, g_ref):
    del og_any
    ge = _gelu(zg_ref[...])
    v = _layernorm(ge[:, GMLP_D:], lng_ref[...], lnb_ref[...])
    gv_ref[...] = v
    og_ref[...] = (ge[:, :GMLP_D] * (v * gw_ref[...] + gb_ref[...])).astype(og_ref.dtype)

    z = zr_ref[...]
    zs = z + (shift_ref[...] - z) * mu_ref[...]
    mixed = _rwkv_mix(zs, w0_ref[...], w2_ref[...], a0_ref[...], a2_ref[...], g2_ref[...], kkp_ref[...],
                      ka_ref[...], rk_ref[...], _head_block_ones())
    for p, (r, k2, vv, logw, kkn, a, g, bonus) in enumerate(mixed):
        sl = slice(p * V7X_LANES, (p + 1) * V7X_LANES)
        wt_ref[sl, :] = jnp.exp(logw).T
        kkt_ref[sl, :] = kkn.T
        bt_ref[sl, :] = (kkn * a).T
        kt_ref[sl, :] = k2.T
        rt_ref[sl, :] = r.T
        vt_ref[sl, :] = vv.T
        bonus_ref[:, sl] = bonus
        g_ref[:, sl] = g


def sample_mix(z_g, z_r, row0, shift, o_g, ln_g, ln_b, gw, gb, mu, w0, w2p, a0, a2p, g2, kkp, ka, rk):
    n = shift.shape[0]
    blk = row0 // n
    tok = lambda w: pl.BlockSpec((n, w), lambda i: (blk, 0))
    loc = lambda w: pl.BlockSpec((n, w), lambda i: (0, 0))
    row = lambda w: pl.BlockSpec((1, w), lambda i: (0, 0))
    full2 = lambda a, b: pl.BlockSpec((a, b), lambda i: (0, 0))
    chan = jax.ShapeDtypeStruct((RWKV_D, n), F32)
    return pl.pallas_call(
        _sample_mix_kernel,
        grid=(1,),
        in_specs=[tok(GMLP_COLS), tok(RWKV_COLS), loc(RWKV_COLS),
                  row(GMLP_D), row(GMLP_D), row(GMLP_D), row(GMLP_D),
                  row(RWKV_COLS), row(RWKV_D), full2(LORA_W + LORA_A, RWKV_D), row(RWKV_D),
                  full2(LORA_W + LORA_A, RWKV_D), full2(LORA_G, RWKV_D), row(RWKV_D), row(RWKV_D), row(RWKV_D),
                  pl.BlockSpec(memory_space=pl.ANY)],
        out_specs=[tok(GMLP_D), loc(GMLP_D)] + [full2(RWKV_D, n)] * 6 + [loc(RWKV_D), loc(RWKV_D)],
        out_shape=[jax.ShapeDtypeStruct(o_g.shape, o_g.dtype), jax.ShapeDtypeStruct((n, GMLP_D), F32)]
                  + [chan] * 6 + [jax.ShapeDtypeStruct((n, RWKV_D), F32)] * 2,
        input_output_aliases={16: 0},
        compiler_params=_params(("arbitrary",)),
        name="sample_mix",
    )(z_g, z_r, shift, ln_g.reshape(1, -1), ln_b.reshape(1, -1), gw, gb, mu, w0, w2p, a0, a2p, g2, kkp, ka, rk, o_g)


def _sample_state_kernel(s_ref, w_ref, kk_ref, b_ref, k_ref, r_ref, v_ref, snew_ref, y_ref):
    n = s_ref.shape[0]
    st = s_ref[...].T.reshape(RWKV_HD, RWKV_HD, n)
    sa = jnp.sum(st * (-kk_ref[...])[None], axis=1, keepdims=True)
    st_new = st * w_ref[...][None] + sa * b_ref[...][None] + v_ref[...] * k_ref[...][None]
    y_ref[...] = jnp.sum(st_new * r_ref[...][None], axis=1, keepdims=True)
    snew_ref[...] = st_new.reshape(RWKV_HD * RWKV_HD, n).T


def sample_state(s0, wt, kkt, bt, kt, rt, vt):
    n = s0.shape[0]
    hd2 = RWKV_HD * RWKV_HD
    state = pl.BlockSpec((n, hd2), lambda h: (0, h))
    keyvec = pl.BlockSpec((RWKV_HD, n), lambda h: (h, 0))
    valvec = pl.BlockSpec((RWKV_HD, 1, n), lambda h: (h, 0, 0))
    return pl.pallas_call(
        _sample_state_kernel,
        grid=(RWKV_HEADS,),
        in_specs=[state, keyvec, keyvec, keyvec, keyvec, keyvec, valvec],
        out_specs=[state, valvec],
        out_shape=[jax.ShapeDtypeStruct((n, RWKV_HEADS * hd2), F32),
                   jax.ShapeDtypeStruct((RWKV_D, 1, n), F32)],
        compiler_params=_params(("parallel",)),
        name="sample_state",
    )(s0.reshape(n, RWKV_HEADS * hd2), wt, kkt, bt, kt, rt, vt.reshape(RWKV_D, 1, n))


def _sample_rwkv_out_kernel(yt_ref, bonus_ref, g_ref, lng_ref, lnb_ref, or_any, or_ref):
    del or_any
    y = yt_ref[...].T
    ones_bd = _head_block_ones()
    for p in range(HEAD_PAIRS):
        sl = slice(p * V7X_LANES, (p + 1) * V7X_LANES)
        o = _group_norm_gate(y[:, sl], bonus_ref[:, sl], g_ref[:, sl], lng_ref[p], lnb_ref[p], ones_bd)
        or_ref[p] = o.astype(or_ref.dtype)


def sample_rwkv_out(yt, bonus, g, lnx_g, lnx_b, o_r, row0):
    n = bonus.shape[0]
    full2 = lambda a, b: pl.BlockSpec((a, b), lambda i: (0, 0))
    pair_row = pl.BlockSpec((HEAD_PAIRS, 1, V7X_LANES), lambda i: (0, 0, 0))
    return pl.pallas_call(
        _sample_rwkv_out_kernel,
        grid=(1,),
        in_specs=[full2(RWKV_D, n), full2(n, RWKV_D), full2(n, RWKV_D), pair_row, pair_row,
                  pl.BlockSpec(memory_space=pl.ANY)],
        out_specs=pl.BlockSpec((HEAD_PAIRS, n, V7X_LANES), lambda i: (0, row0 // n, 0)),
        out_shape=jax.ShapeDtypeStruct(o_r.shape, o_r.dtype),
        input_output_aliases={5: 0},
        compiler_params=_params(("arbitrary",)),
        name="sample_rwkv_out",
    )(yt, bonus, g, lnx_g.reshape(HEAD_PAIRS, 1, V7X_LANES), lnx_b.reshape(HEAD_PAIRS, 1, V7X_LANES), o_r)


def _xattn_sample_kernel(q_ref, ck_ref, cv_ref, ox_any, ox_ref, *, sb):
    del ox_any
    q = q_ref[...]
    rows = N_MEM * XA_HEADS
    hrow = lax.broadcasted_iota(jnp.int32, (8, rows), 0) % XA_HEADS
    hcol = lax.broadcasted_iota(jnp.int32, (8, rows), 1) % XA_HEADS
    own = hrow == hcol
    h8 = lax.broadcasted_iota(jnp.int32, (8, XA_HD), 0) % XA_HEADS
    nt = (((1,), (1,)), ((), ()))
    outs = []
    for s in range(sb):
        qm = jnp.zeros((8, XA_HD), F32)
        for h in range(XA_HEADS):
            qm = jnp.where(h8 == h, q[s:s + 1, h * XA_HD:(h + 1) * XA_HD], qm)
        sc = lax.dot_general(qm.astype(BF16), ck_ref[s].astype(BF16), nt,
                             preferred_element_type=F32) * (XA_HD ** -0.5)
        pr = _softmax_rows(jnp.where(own, sc, NEG_INF))
        res = jnp.dot(pr.astype(BF16), cv_ref[s].astype(BF16), preferred_element_type=F32)
        outs.append(jnp.concatenate([res[h:h + 1, :] for h in range(XA_HEADS)], axis=1))
    ox_ref[...] = jnp.concatenate(outs, axis=0).astype(ox_ref.dtype)


def xattn_sample(z_q, ck, cv, o_x, row0, *, sb=8):
    n = ck.shape[0]
    off = row0 // sb
    tok = pl.BlockSpec((sb, XA_D), lambda i: (i + off, 0))
    cache = pl.BlockSpec((sb, N_MEM * XA_HEADS, XA_HD), lambda i: (i, 0, 0))
    return pl.pallas_call(
        functools.partial(_xattn_sample_kernel, sb=sb),
        grid=(n // sb,),
        in_specs=[tok, cache, cache, pl.BlockSpec(memory_space=pl.ANY)],
        out_specs=tok,
        out_shape=jax.ShapeDtypeStruct(o_x.shape, o_x.dtype),
        input_output_aliases={3: 0},
        compiler_params=_params(("arbitrary",)),
        name="xattn_sample",
    )(z_q, ck, cv, o_x)


def _merge_kernel(og_ref, or_ref, ox_ref, g0_ref, g1_ref, g2_ref, wg_ref, wr_ref, wx_ref, o_ref,
                  wg_bf, wr_bf, wx_bf):
    @pl.when(pl.program_id(1) == 0)
    def _():
        wg_bf[...] = wg_ref[...].astype(BF16)
        wr_bf[...] = wr_ref[...].astype(BF16)
        wx_bf[...] = wx_ref[...].astype(BF16)

    up_g = jnp.dot(og_ref[...], wg_bf[...], preferred_element_type=F32)
    up_r = jnp.dot(or_ref[0], wr_bf[0:V7X_LANES, :], preferred_element_type=F32)
    for p in range(1, HEAD_PAIRS):
        up_r = up_r + jnp.dot(or_ref[p], wr_bf[p * V7X_LANES:(p + 1) * V7X_LANES, :], preferred_element_type=F32)
    up_x = jnp.dot(ox_ref[...], wx_bf[...], preferred_element_type=F32)
    gate = lambda ref: _sigmoid(ref[...].astype(F32))
    merged = gate(g0_ref) * up_g + gate(g1_ref) * up_r + gate(g2_ref) * up_x
    o_ref[...] = merged.astype(o_ref.dtype)


def merge(o_g, o_r, o_x, z_gate, w_up_g, w_up_r, w_up_x, *, tb, nb=512):
    m = o_g.shape[0]
    nblk = D_MODEL // nb
    gate = lambda b: pl.BlockSpec((tb, nb), lambda j, i: (i, b * nblk + j))
    wspec = lambda k: pl.BlockSpec((k, nb), lambda j, i: (0, j))
    return pl.pallas_call(
        _merge_kernel,
        grid=(nblk, m // tb),
        in_specs=[pl.BlockSpec((tb, GMLP_D), lambda j, i: (i, 0)),
                  pl.BlockSpec((HEAD_PAIRS, tb, V7X_LANES), lambda j, i: (0, i, 0)),
                  pl.BlockSpec((tb, XA_D), lambda j, i: (i, 0)),
                  gate(0), gate(1), gate(2), wspec(GMLP_D), wspec(RWKV_D), wspec(XA_D)],
        out_specs=pl.BlockSpec((tb, nb), lambda j, i: (i, j)),
        out_shape=jax.ShapeDtypeStruct((m, D_MODEL), BF16),
        scratch_shapes=[pltpu.VMEM((GMLP_D, nb), BF16), pltpu.VMEM((RWKV_D, nb), BF16),
                        pltpu.VMEM((XA_D, nb), BF16)],
        compiler_params=_params(("arbitrary", "arbitrary")),
        name="merge",
    )(o_g, o_r, o_x, z_gate, z_gate, z_gate, w_up_g, w_up_r, w_up_x)


def _extract_top(src_ref, work_ref, rank_ref, vals_ref, n_rows):
    width = work_ref.shape[1]
    riota = lax.broadcasted_iota(jnp.int32, (n_rows, V7X_LANES), 0)
    kiota = lax.broadcasted_iota(jnp.int32, (TOPK, V7X_LANES), 0)

    def run(break_ties):
        work_ref[...] = src_ref[...]
        vals_ref[...] = jnp.zeros_like(vals_ref)

        def body(p, carry):
            taken = TAKEN_BASE - jnp.asarray(p, F32) * TAKEN_STEP
            for c in range(width // V7X_LANES):
                sl = slice(c * V7X_LANES, (c + 1) * V7X_LANES)
                w = work_ref[:, sl]
                m = jnp.max(w, axis=0, keepdims=True)
                if break_ties:
                    idx = jnp.min(jnp.where(w == m, riota, n_rows), axis=0, keepdims=True)
                    hit = riota == idx
                else:
                    hit = w == m
                work_ref[:, sl] = jnp.where(hit, taken, w)
                vals_ref[:, sl] = jnp.where(kiota == p, m, vals_ref[:, sl])
            return carry

        lax.fori_loop(0, TOPK, body, 0)

    run(False)
    picked = jnp.sum(jnp.where(work_ref[...] < TAKEN_BELOW, 1.0, 0.0), axis=0, keepdims=True)
    tied = jnp.max(picked) > TOPK

    @pl.when(tied)
    def _():
        run(True)

    w = work_ref[...]
    rank = jnp.where(w < TAKEN_BELOW, w * (-1.0 / TAKEN_STEP) + (TAKEN_BASE / TAKEN_STEP), float(TOPK))
    rank_ref[...] = rank.astype(jnp.int32)


_CAND_COUNT = tuple(TOPK // (a + 1) for a in range(TOPK))
_CAND_START = tuple(sum(_CAND_COUNT[:a]) for a in range(TOPK))
_CAND_ROWS = -(-sum(_CAND_COUNT) // 8) * 8


def _peer_topk_kernel(q_ref, keys_ref, r2_ref, lim_ref, e1_ref, e2_ref,
                      s_scr, work_scr, rank_scr, vals_scr, cand_scr, cwork_scr, crank_scr, cvals_scr, *, tbk):
    nt = (((1,), (1,)), ((), ()))
    for h in range(PEER_HEADS):
        for c in range(2):
            qcol = (2 * h + c) * PEER_DH
            col = (2 * h + c) * tbk
            k_hi, k_lo = _split_bf16(keys_ref[h, c])
            q_hi, q_lo = _split_bf16(q_ref[:, qcol:qcol + PEER_DH])
            sc = (lax.dot_general(k_hi, q_hi, nt, preferred_element_type=F32)
                  + lax.dot_general(k_hi, q_lo, nt, preferred_element_type=F32)
                  + lax.dot_general(k_lo, q_hi, nt, preferred_element_type=F32))
            s_scr[:, col:col + tbk] = sc
    _extract_top(s_scr, work_scr, rank_scr, vals_scr, N_KEYS)

    crow = lax.broadcasted_iota(jnp.int32, (_CAND_ROWS, tbk), 0)
    seg = jnp.full((_CAND_ROWS, tbk), TOPK, jnp.int32)
    for a in reversed(range(TOPK)):
        seg = jnp.where(crow < _CAND_START[a] + _CAND_COUNT[a], jnp.minimum(seg, a), seg)
    pad = jnp.zeros((_CAND_ROWS - TOPK, tbk), F32)
    for h in range(PEER_HEADS):
        v1 = vals_scr[:, (2 * h) * tbk:(2 * h + 1) * tbk]
        v2 = jnp.concatenate([vals_scr[:, (2 * h + 1) * tbk:(2 * h + 2) * tbk], pad], axis=0)
        cand = jnp.full((_CAND_ROWS, tbk), CAND_PAD, F32)
        for a in range(TOPK):
            shifted = v2 if _CAND_START[a] == 0 else pltpu.roll(v2, _CAND_START[a], axis=0)
            cand = jnp.where(seg == a, v1[a:a + 1, :] + shifted, cand)
        cand_scr[:, h * tbk:(h + 1) * tbk] = cand
    _extract_top(cand_scr, cwork_scr, crank_scr, cvals_scr, _CAND_ROWS)

    for h in range(PEER_HEADS):
        hs = slice(h * tbk, (h + 1) * tbk)
        s1 = slice((2 * h) * tbk, (2 * h + 1) * tbk)
        s2 = slice((2 * h + 1) * tbk, (2 * h + 2) * tbk)
        cvals = cvals_scr[:, hs]
        z = jnp.sum(jnp.exp(cvals - cvals[0:1, :]), axis=0, keepdims=True)
        chosen = crank_scr[:, hs] < TOPK
        rank1 = rank_scr[:, s1]
        lim = jnp.zeros((N_KEYS, tbk), F32)
        for a in range(TOPK):
            count = jnp.sum(jnp.where(jnp.logical_and(chosen, seg == a), 1.0, 0.0), axis=0, keepdims=True)
            lim = jnp.where(rank1 == a, count, lim)
        lim_ref[h] = lim
        r2_ref[h] = rank_scr[:, s2].astype(F32).astype(r2_ref.dtype)
        e1_ref[h] = jnp.exp(s_scr[:, s1] - vals_scr[0:1, s1]) / z
        e2_ref[h] = jnp.exp(s_scr[:, s2] - vals_scr[0:1, s2]).astype(e2_ref.dtype)


def peer_topk(q, keys, *, tbk=128):
    m = q.shape[0]
    out = pl.BlockSpec((PEER_HEADS, N_KEYS, tbk), lambda i: (0, 0, i))
    shape = lambda dt: jax.ShapeDtypeStruct((PEER_HEADS, N_KEYS, m), dt)
    wide = 2 * PEER_HEADS * tbk
    return pl.pallas_call(
        functools.partial(_peer_topk_kernel, tbk=tbk),
        grid=(m // tbk,),
        in_specs=[pl.BlockSpec((tbk, PEER_HEADS * 2 * PEER_DH), lambda i: (i, 0)),
                  pl.BlockSpec((PEER_HEADS, 2, N_KEYS, PEER_DH), lambda i: (0, 0, 0, 0))],
        out_specs=[out] * 4,
        out_shape=[shape(BF16), shape(F32), shape(F32), shape(BF16)],
        scratch_shapes=[pltpu.VMEM((N_KEYS, wide), F32), pltpu.VMEM((N_KEYS, wide), F32),
                        pltpu.VMEM((N_KEYS, wide), jnp.int32), pltpu.VMEM((TOPK, wide), F32),
                        pltpu.VMEM((_CAND_ROWS, PEER_HEADS * tbk), F32),
                        pltpu.VMEM((_CAND_ROWS, PEER_HEADS * tbk), F32),
                        pltpu.VMEM((_CAND_ROWS, PEER_HEADS * tbk), jnp.int32),
                        pltpu.VMEM((TOPK, PEER_HEADS * tbk), F32)],
        compiler_params=_params(("parallel",)),
        name="peer_topk",
    )(q, keys)


PEER_SUB = 2 * N_KEYS
TOKEN_TILE = 768


def _peer_dense_kernel(xt_ref, r2_ref, lim_ref, e1_ref, e2_ref, u_ref, v_ref, o_ref, coef_a, coef_b, ht_scr,
                       *, eb, n_blocks):
    e = pl.program_id(1)
    blk = jnp.minimum(e, n_blocks - 1)
    n_sub = eb // PEER_SUB
    d_sub = o_ref.shape[1] // n_sub

    def hidden(s):
        ht_scr[s % 2] = jnp.dot(u_ref[s * PEER_SUB:(s + 1) * PEER_SUB, :], xt_ref[...],
                                preferred_element_type=F32)

    def step(prev_ref, next_ref):
        hidden(0)
        for s in range(n_sub):
            ht = ht_scr.at[s % 2]
            cols = slice(s * d_sub, (s + 1) * d_sub)
            if prev_ref is not None:
                o_ref[:, cols] += lax.dot_general(prev_ref[...], v_ref[:, cols], (((0,), (0,)), ((), ())),
                                                  preferred_element_type=F32)
            if s + 1 < n_sub:
                hidden(s + 1)
            for ii in range(PEER_SUB // N_KEYS):
                i = blk * (eb // N_KEYS) + s * (PEER_SUB // N_KEYS) + ii
                row = s * PEER_SUB + ii * N_KEYS
                gate = None
                for h in range(PEER_HEADS):
                    lim = lim_ref[h, pl.ds(i, 1), :].astype(BF16)
                    e1 = e1_ref[h, pl.ds(i, 1), :].astype(BF16)
                    term = jnp.where(r2_ref[h] < lim, e2_ref[h] * e1, jnp.zeros((), BF16))
                    gate = term if gate is None else gate + term
                next_ref[row:row + N_KEYS, :] = gate * _gelu_to_bf16(ht[ii * N_KEYS:(ii + 1) * N_KEYS, :])

    @pl.when(e == 0)
    def _():
        o_ref[...] = jnp.zeros_like(o_ref)
        step(None, coef_a)

    @pl.when(jnp.logical_and(e % 2 == 0, jnp.logical_and(e > 0, e < n_blocks)))
    def _():
        step(coef_b, coef_a)

    @pl.when(e % 2 == 1)
    def _():
        step(coef_a, coef_b)

    @pl.when(e == n_blocks)
    def _():
        o_ref[...] += lax.dot_general(coef_b[...], v_ref[...], (((0,), (0,)), ((), ())),
                                      preferred_element_type=F32)


def peer_dense(xt, r2, lim, e1, e2, u_bf, v_bf, *, tbl, eb=1024):
    d, m = xt.shape
    n_blocks = u_bf.shape[0] // eb
    assert n_blocks % 2 == 0
    once = pl.Buffered(1)
    head = pl.BlockSpec((PEER_HEADS, N_KEYS, tbl), lambda t, e: (0, 0, t), pipeline_mode=once)
    return pl.pallas_call(
        functools.partial(_peer_dense_kernel, eb=eb, n_blocks=n_blocks),
        grid=(m // tbl, n_blocks + 1),
        in_specs=[pl.BlockSpec((d, tbl), lambda t, e: (0, t)), head, head, head, head,
                  pl.BlockSpec((eb, d), lambda t, e: (jnp.minimum(e, n_blocks - 1), 0)),
                  pl.BlockSpec((eb, d), lambda t, e: (jnp.maximum(e - 1, 0), 0))],
        out_specs=pl.BlockSpec((tbl, d), lambda t, e: (t, 0)),
        out_shape=jax.ShapeDtypeStruct((m, d), F32),
        scratch_shapes=[pltpu.VMEM((eb, tbl), BF16), pltpu.VMEM((eb, tbl), BF16),
                        pltpu.VMEM((2, PEER_SUB, tbl), F32)],
        compiler_params=_params(("parallel", "arbitrary")),
        name="peer_dense",
    )(xt, r2, lim, e1, e2, u_bf, v_bf)


def _final_kernel(h_ref, p_ref, g_ref, o_ref):
    x = h_ref[...] + p_ref[...]
    o_ref[...] = x * lax.rsqrt(jnp.mean(x * x, axis=-1, keepdims=True) + RMS_EPS) * g_ref[...]


def final_norm(h, peer, g, row0, n_rows, *, tb):
    d = h.shape[1]
    off = row0 // tb
    rows = pl.BlockSpec((tb, d), lambda i: (i + off, 0))
    return pl.pallas_call(
        _final_kernel,
        grid=(n_rows // tb,),
        in_specs=[rows, rows, pl.BlockSpec((1, d), lambda i: (0, 0))],
        out_specs=pl.BlockSpec((tb, d), lambda i: (i, 0)),
        out_shape=jax.ShapeDtypeStruct((n_rows, d), F32),
        compiler_params=_params(("parallel",)),
        name="final_norm",
    )(h, peer, g.reshape(1, d))


def kernel(x_prompt, x_sample, mem_prompt, state_shift, state_wkv, cache_mem_k, cache_mem_v, ln1_g, w_in, gmlp_ln_g, gmlp_ln_b, gmlp_ws, gmlp_bs, rwkv_mu, rwkv_w0, rwkv_w2, rwkv_a0, rwkv_a2, rwkv_g2, rwkv_kk, rwkv_ka, rwkv_rk, rwkv_lnx_g, rwkv_lnx_b, mem_norm_g, w_mem_kv, w_up_g, w_up_r, w_up_x, w_out, ln2_g, peer_wq, peer_keys, peer_u, peer_v, final_g):
    depth = w_in.shape[0]
    assert depth == 1, "single-layer step"
    l = 0
    n_batch, seq_len, d = x_prompt.shape
    n_dec = x_sample.shape[0]
    n_prompt = n_batch * seq_len
    m = -(-(n_prompt + n_dec) // TOKEN_TILE) * TOKEN_TILE
    tb_small = TOKEN_TILE
    tb_big = _largest_divisor(m, (1408, TOKEN_TILE))

    x = jnp.concatenate([x_prompt.reshape(n_prompt, d), x_sample.reshape(n_dec, d),
                         jnp.zeros((m - n_prompt - n_dec, d), F32)], axis=0)
    xn = rmsnorm(x, ln1_g[l], tb=tb_small, out_dtype=BF16)
    proj = functools.partial(matmul, xn, w_in[l], tb=tb_big, nb=512)
    z_g = proj(col_off=0, n_cols=GMLP_COLS, name="proj_gmlp")
    z_r = proj(col_off=GMLP_COLS, n_cols=RWKV_COLS, name="proj_rwkv")
    z_q = proj(col_off=GMLP_COLS + RWKV_COLS, n_cols=XA_D, name="proj_xattn")
    z_gate = proj(col_off=GMLP_COLS + RWKV_COLS + XA_D, n_cols=GATE_COLS, out_dtype=BF16, name="proj_gate")

    memn = rmsnorm(mem_prompt.reshape(n_batch * N_MEM, d), mem_norm_g[l], tb=256, out_dtype=BF16)
    kv = matmul(memn, w_mem_kv[l], tb=N_MEM, nb=512, name="proj_mem_kv")
    p_mk = kv[:, :XA_D].reshape(1, n_batch, N_MEM, XA_HEADS, XA_HD)
    p_mv = kv[:, XA_D:].reshape(1, n_batch, N_MEM, XA_HEADS, XA_HD)

    row = lambda a: a.reshape(1, -1)
    zeros_lora = jnp.zeros((LORA_W, RWKV_D), F32)
    w2p = jnp.concatenate([rwkv_w2[l], zeros_lora], axis=0).astype(BF16)
    a2p = jnp.concatenate([zeros_lora, rwkv_a2[l]], axis=0).astype(BF16)
    rw = (row(rwkv_mu[l]), row(rwkv_w0[l]), w2p, row(rwkv_a0[l]), a2p, rwkv_g2[l].astype(BF16),
          row(rwkv_kk[l]), row(rwkv_ka[l]), row(rwkv_rk[l]))

    o_g = gmlp_prompt(z_g, jnp.zeros((m, GMLP_D), BF16), n_prompt, gmlp_ln_g[l], gmlp_ln_b[l], gmlp_ws[l],
                      gmlp_bs[l])
    prep = rwkv_prep(z_r, n_prompt, seq_len, *rw)
    o_r, st = rwkv_chunks(prep, jnp.zeros((HEAD_PAIRS, m, V7X_LANES), BF16), n_batch, seq_len,
                          rwkv_lnx_g[l], rwkv_lnx_b[l])
    o_x = xattn_prompt(z_q, kv, jnp.zeros((m, XA_D), BF16), n_batch, seq_len)

    gw = jnp.repeat(gmlp_ws[l][:, 0, 0], CHUNK).reshape(1, GMLP_D)
    gb = jnp.repeat(gmlp_bs[l][:, 0], CHUNK).reshape(1, GMLP_D)
    o_g, s_gv, wt, kkt, bt, kt, rt, vt, bonus_s, g_s = sample_mix(
        z_g, z_r, n_prompt, state_shift[l], o_g, gmlp_ln_g[l], gmlp_ln_b[l], gw, gb, *rw)
    s_wkv, y_s = sample_state(state_wkv[l], wt, kkt, bt, kt, rt, vt)
    o_r = sample_rwkv_out(y_s.reshape(RWKV_D, n_dec), bonus_s, g_s, rwkv_lnx_g[l], rwkv_lnx_b[l], o_r, n_prompt)
    o_x = xattn_sample(z_q, cache_mem_k[l].reshape(n_dec, N_MEM * XA_HEADS, XA_HD),
                       cache_mem_v[l].reshape(n_dec, N_MEM * XA_HEADS, XA_HD), o_x, n_prompt)

    merged = merge(o_g, o_r, o_x, z_gate, w_up_g[l], w_up_r[l], w_up_x[l], tb=tb_small)
    h = matmul(merged, w_out[l], tb=tb_big, nb=512, residual=x, name="proj_out")

    hn, hn_t = rmsnorm(h, ln2_g[l], tb=tb_small, out_dtype=BF16, transposed=True)
    q = matmul(hn, peer_wq[l], tb=tb_big, nb=512, name="proj_peer_q")
    r2, lim, e1, e2 = peer_topk(q, peer_keys[l])
    peer = peer_dense(hn_t, r2, lim, e1, e2, peer_u[l].astype(BF16), peer_v[l].astype(BF16), tbl=tb_small)
    y_prompt = final_norm(h, peer, final_g, 0, n_prompt, tb=512)
    y_sample = final_norm(h, peer, final_g, n_prompt, n_dec, tb=n_dec)

    st = st.reshape(n_batch, HEAD_PAIRS, 2, RWKV_HD, 2, RWKV_HD)
    p_wkv = jnp.stack([st[:, :, 0, :, 0, :], st[:, :, 1, :, 1, :]], axis=2)
    p_wkv = jnp.swapaxes(p_wkv, -1, -2).reshape(1, n_batch, RWKV_HEADS, RWKV_HD, RWKV_HD)
    p_shift = z_r[seq_len - 1:n_prompt:seq_len][None]
    s_shift = z_r[n_prompt:n_prompt + n_dec][None]
    s_wkv = s_wkv.reshape(1, n_dec, RWKV_HEADS, RWKV_HD, RWKV_HD)
    return (y_prompt.reshape(n_batch, seq_len, d), y_sample.reshape(n_dec, 1, d), p_mk, p_mv,
            p_shift, p_wkv, s_shift, s_wkv, s_gv.reshape(1, n_dec, 1, GMLP_D))
```

```python
import functools

import jax
import jax.numpy as jnp
from jax import lax
from jax.experimental import pallas as pl
from jax.experimental.pallas import tpu as pltpu

F32 = jnp.float32
BF16 = jnp.bfloat16

D_MODEL = 2048
CHUNK = 128
GMLP_GROUPS = 6
GMLP_D = 768
RWKV_HEADS = 12
RWKV_HD = 64
RWKV_D = 768
LORA_W = 64
LORA_A = 64
LORA_G = 128
RWKV_COLS = 2560
XA_HEADS = 4
XA_HD = 128
XA_D = 512
N_MEM = 256
GMLP_COLS = 2 * GMLP_D
GATE_COLS = 3 * D_MODEL
PEER_HEADS = 8
N_KEYS = 128
PEER_DH = 128
TOPK = 16
RMS_EPS = 1e-6
LN_EPS = 1e-5
GN_EPS = 64e-5

V7X_LANES = 128
V7X_VMEM_LIMIT_BYTES = 56 * 1024 * 1024

HEAD_PAIRS = RWKV_HEADS // 2
RWKV_CHUNK = 64
RWKV_CHUNKS_PER_STEP = 32
NEG_INF = float("-inf")
CAND_PAD = -(2.0 ** 100)
TAKEN_BELOW = -(2.0 ** 110)
TAKEN_BASE = -(2.0 ** 120)
TAKEN_STEP = 2.0 ** 115


def _params(semantics):
    return pltpu.CompilerParams(dimension_semantics=semantics, vmem_limit_bytes=V7X_VMEM_LIMIT_BYTES)


def _largest_divisor(m, candidates):
    return next(c for c in candidates if m % c == 0)


def _gelu(x):
    return 0.5 * x * (1.0 + jnp.tanh(0.7978845608028654 * (x + 0.044715 * (x * x * x))))


def _gelu_to_bf16(x):
    inner = (0.7978845608028654 * x) * (1.0 + 0.044715 * (x * x))
    xb = x.astype(BF16)
    return (0.5 * xb) * (1.0 + jnp.tanh(inner.astype(BF16)))


def _split_bf16(x):
    hi = x.astype(BF16)
    return hi, (x - hi.astype(F32)).astype(BF16)


def _split3_bf16(x):
    hi = x.astype(BF16)
    rest = x - hi.astype(F32)
    mid = rest.astype(BF16)
    return hi, mid, (rest - mid.astype(F32)).astype(BF16)


def _sigmoid(x):
    return 0.5 + 0.5 * jnp.tanh(0.5 * x)


def _rmsnorm_kernel(x_ref, g_ref, *o_refs, transposed):
    x = x_ref[...]
    y = x * lax.rsqrt(jnp.mean(x * x, axis=-1, keepdims=True) + RMS_EPS) * g_ref[...]
    o_refs[0][...] = y.astype(o_refs[0].dtype)
    if transposed:
        o_refs[1][...] = y.T.astype(o_refs[1].dtype)


def rmsnorm(x, g, *, tb, out_dtype, transposed=False):
    m, d = x.shape
    out_shape = [jax.ShapeDtypeStruct((m, d), out_dtype)]
    out_specs = [pl.BlockSpec((tb, d), lambda i: (i, 0))]
    if transposed:
        out_shape.append(jax.ShapeDtypeStruct((d, m), out_dtype))
        out_specs.append(pl.BlockSpec((d, tb), lambda i: (0, i)))
    res = pl.pallas_call(
        functools.partial(_rmsnorm_kernel, transposed=transposed),
        grid=(m // tb,),
        in_specs=[pl.BlockSpec((tb, d), lambda i: (i, 0)), pl.BlockSpec((1, d), lambda i: (0, 0))],
        out_specs=out_specs,
        out_shape=out_shape,
        compiler_params=_params(("parallel",)),
        name="rmsnorm_t" if transposed else "rmsnorm",
    )(x, g.reshape(1, d))
    return res if transposed else res[0]


def _matmul_kernel(a_ref, w_ref, *rest, has_residual):
    if has_residual:
        r_ref, o_ref, wbf_ref = rest
    else:
        o_ref, wbf_ref = rest

    @pl.when(pl.program_id(1) == 0)
    def _():
        wbf_ref[...] = w_ref[...].astype(BF16)

    acc = jnp.dot(a_ref[...], wbf_ref[...], preferred_element_type=F32)
    if has_residual:
        acc = acc + r_ref[...]
    o_ref[...] = acc.astype(o_ref.dtype)


def matmul(a, w, *, tb, nb, col_off=0, n_cols=None, residual=None, out_dtype=F32, name="matmul"):
    m, k = a.shape
    n = w.shape[1] if n_cols is None else n_cols
    off = col_off // nb
    in_specs = [pl.BlockSpec((tb, k), lambda j, i: (i, 0)),
                pl.BlockSpec((k, nb), lambda j, i: (0, j + off))]
    args = [a, w]
    if residual is not None:
        in_specs.append(pl.BlockSpec((tb, nb), lambda j, i: (i, j)))
        args.append(residual)
    return pl.pallas_call(
        functools.partial(_matmul_kernel, has_residual=residual is not None),
        grid=(n // nb, m // tb),
        in_specs=in_specs,
        out_specs=pl.BlockSpec((tb, nb), lambda j, i: (i, j)),
        out_shape=jax.ShapeDtypeStruct((m, n), out_dtype),
        scratch_shapes=[pltpu.VMEM((k, nb), BF16)],
        compiler_params=_params(("arbitrary", "arbitrary")),
        name=name,
    )(*args)


def _layernorm(v, g, b):
    vc = v - jnp.mean(v, axis=-1, keepdims=True)
    var = jnp.mean(vc * vc, axis=-1, keepdims=True)
    return vc * lax.rsqrt(var + LN_EPS) * g + b


def _gmlp_kernel(z_ref, lng_ref, lnb_ref, ws_ref, bst_ref, buf_any, o_ref):
    del buf_any
    ge = _gelu(z_ref[...])
    u = ge[:, :GMLP_D]
    v = _layernorm(ge[:, GMLP_D:], lng_ref[...], lnb_ref[...])
    row = lax.broadcasted_iota(jnp.int32, (CHUNK, CHUNK), 0)
    col = lax.broadcasted_iota(jnp.int32, (CHUNK, CHUNK), 1)
    causal = col <= row
    for g in range(GMLP_GROUPS):
        sl = slice(g * CHUNK, (g + 1) * CHUNK)
        wm = jnp.where(causal, ws_ref[g], 0.0).astype(BF16)
        mixed = jnp.dot(wm, v[:, sl].astype(BF16), preferred_element_type=F32) + bst_ref[:, g:g + 1]
        o_ref[:, sl] = (u[:, sl] * mixed).astype(o_ref.dtype)


def gmlp_prompt(z_g, buf, n_tokens, ln_g, ln_b, ws, bs):
    return pl.pallas_call(
        _gmlp_kernel,
        grid=(n_tokens // CHUNK,),
        in_specs=[pl.BlockSpec((CHUNK, GMLP_COLS), lambda i: (i, 0)),
                  pl.BlockSpec((1, GMLP_D), lambda i: (0, 0)),
                  pl.BlockSpec((1, GMLP_D), lambda i: (0, 0)),
                  pl.BlockSpec((GMLP_GROUPS, CHUNK, CHUNK), lambda i: (0, 0, 0)),
                  pl.BlockSpec((CHUNK, GMLP_GROUPS), lambda i: (0, 0)),
                  pl.BlockSpec(memory_space=pl.ANY)],
        out_specs=pl.BlockSpec((CHUNK, GMLP_D), lambda i: (i, 0)),
        out_shape=jax.ShapeDtypeStruct(buf.shape, buf.dtype),
        input_output_aliases={5: 0},
        compiler_params=_params(("parallel",)),
        name="gmlp_prompt",
    )(z_g, ln_g.reshape(1, GMLP_D), ln_b.reshape(1, GMLP_D), ws, bs.T, buf)


def _head_block_ones():
    r = lax.broadcasted_iota(jnp.int32, (V7X_LANES, V7X_LANES), 0) // RWKV_HD
    c = lax.broadcasted_iota(jnp.int32, (V7X_LANES, V7X_LANES), 1) // RWKV_HD
    return (r == c).astype(F32)


def _head_sum(x, ones_bd):
    ones = ones_bd.astype(BF16)
    return sum(jnp.dot(t, ones, preferred_element_type=F32) for t in _split3_bf16(x))


def _rwkv_mix(zs, w0, w2p, a0, a2p, g2, kkp, ka, rk, ones_bd):
    r = zs[:, 0:RWKV_D]
    k = zs[:, RWKV_D:2 * RWKV_D]
    v = zs[:, 2 * RWKV_D:3 * RWKV_D]
    lwa = zs[:, 3 * RWKV_D:3 * RWKV_D + LORA_W + LORA_A]
    lg = zs[:, 3 * RWKV_D + LORA_W + LORA_A:]
    lora_w = jnp.dot(jnp.tanh(lwa).astype(BF16), w2p, preferred_element_type=F32)
    lora_a = jnp.dot(lwa.astype(BF16), a2p, preferred_element_type=F32)
    x = -(w0 + lora_w)
    softplus = jnp.maximum(x, 0.0) + jnp.log(1.0 + jnp.exp(-jnp.abs(x)))
    logw = -jnp.exp(-softplus - 0.5)
    a = _sigmoid(a0 + lora_a)
    g = jnp.dot(_sigmoid(lg).astype(BF16), g2, preferred_element_type=F32)
    kk = k * kkp
    k2 = k * (1.0 + (a - 1.0) * ka)
    rkk = r * k2 * rk
    out = []
    for p in range(HEAD_PAIRS):
        sl = slice(p * V7X_LANES, (p + 1) * V7X_LANES)
        kk_p = kk[:, sl]
        norm = jnp.sqrt(_head_sum(kk_p * kk_p, ones_bd))
        kkn = kk_p / jnp.maximum(norm, 1e-12)
        bonus = _head_sum(rkk[:, sl], ones_bd) * v[:, sl]
        out.append((r[:, sl], k2[:, sl], v[:, sl], logw[:, sl], kkn, a[:, sl], g[:, sl], bonus))
    return out


def _rwkv_prep_kernel(z_ref, zp_ref, mu_ref, w0_ref, w2_ref, a0_ref, a2_ref, g2_ref, kkp_ref, ka_ref, rk_ref,
                      at_ref, btp_ref, ktp_ref, rt_ref, v_ref, g_ref, bonus_ref, w_ref, pc_ref, cs_scr,
                      *, tb, blocks_per_seq):
    i = pl.program_id(0)
    z = z_ref[...]
    first = (i % blocks_per_seq) == 0
    prev_row = jnp.where(first, 0.0, zp_ref[7:8, :])
    row = lax.broadcasted_iota(jnp.int32, (tb, 1), 0)
    zprev = jnp.where(row == 0, prev_row, pltpu.roll(z, 1, axis=0))
    zs = z + (zprev - z) * mu_ref[...]
    ones_bd = _head_block_ones()
    mixed = _rwkv_mix(zs, w0_ref[...], w2_ref[...], a0_ref[...], a2_ref[...], g2_ref[...], kkp_ref[...],
                      ka_ref[...], rk_ref[...], ones_bd)
    n_chunks = tb // RWKV_CHUNK
    shape3 = (n_chunks, RWKV_CHUNK, V7X_LANES)
    tr = lax.broadcasted_iota(jnp.int32, (n_chunks, RWKV_CHUNK, RWKV_CHUNK), 1)
    tc = lax.broadcasted_iota(jnp.int32, (n_chunks, RWKV_CHUNK, RWKV_CHUNK), 2)
    tri = (tc <= tr).astype(BF16)
    bdot = lambda x: lax.dot_general(tri, x, (((2,), (1,)), ((0,), (0,))), preferred_element_type=F32)
    for p, (r, k2, v, logw, kkn, a, g, bonus) in enumerate(mixed):
        hi, mid, lo = _split3_bf16(logw.reshape(shape3))
        cs3 = bdot(hi) + bdot(mid) + bdot(lo)
        total = cs3[:, RWKV_CHUNK - 1:RWKV_CHUNK, :]
        cs = cs3.reshape(tb, V7X_LANES)
        tail = jnp.exp(total - cs3).reshape(tb, V7X_LANES)
        cs_scr[...] = cs
        cs_end = cs_scr[pl.ds(RWKV_CHUNK - 1, n_chunks, stride=RWKV_CHUNK), :]
        inv_p = jnp.exp(-cs)
        at_ref[p] = (jnp.exp(cs - logw) * kkn).astype(BF16)
        btp_ref[p] = (kkn * a * tail).astype(BF16)
        ktp_ref[p] = (k2 * tail).astype(BF16)
        rt_ref[p] = (jnp.exp(cs) * r).astype(BF16)
        v_ref[p] = v.astype(BF16)
        g_ref[p] = g
        bonus_ref[p] = bonus
        w_ref[p] = jnp.concatenate([kkn * a * inv_p, k2 * inv_p], axis=-1).astype(BF16)
        pc_ref[p] = jnp.exp(cs_end)


def rwkv_prep(z_r, n_tokens, seq_len, mu, w0, w2p, a0, a2p, g2, kkp, ka, rk, *, tb=512):
    n_chunks = n_tokens // RWKV_CHUNK
    row_spec = lambda w: pl.BlockSpec((1, w), lambda i: (0, 0))
    pair_out = lambda w: pl.BlockSpec((HEAD_PAIRS, tb, w), lambda i: (0, i, 0))
    pair_shape = lambda w, dt=BF16: jax.ShapeDtypeStruct((HEAD_PAIRS, n_tokens, w), dt)
    return pl.pallas_call(
        functools.partial(_rwkv_prep_kernel, tb=tb, blocks_per_seq=seq_len // tb),
        grid=(n_tokens // tb,),
        in_specs=[pl.BlockSpec((tb, RWKV_COLS), lambda i: (i, 0)),
                  pl.BlockSpec((8, RWKV_COLS), lambda i: (jnp.maximum(i * (tb // 8) - 1, 0), 0)),
                  row_spec(RWKV_COLS), row_spec(RWKV_D),
                  pl.BlockSpec((LORA_W + LORA_A, RWKV_D), lambda i: (0, 0)),
                  row_spec(RWKV_D),
                  pl.BlockSpec((LORA_W + LORA_A, RWKV_D), lambda i: (0, 0)),
                  pl.BlockSpec((LORA_G, RWKV_D), lambda i: (0, 0)),
                  row_spec(RWKV_D), row_spec(RWKV_D), row_spec(RWKV_D)],
        out_specs=[pair_out(V7X_LANES)] * 7 + [pair_out(2 * V7X_LANES),
                   pl.BlockSpec((HEAD_PAIRS, tb // RWKV_CHUNK, V7X_LANES), lambda i: (0, i, 0))],
        out_shape=[pair_shape(V7X_LANES)] * 5 + [pair_shape(V7X_LANES, F32)] * 2 + [pair_shape(2 * V7X_LANES),
                   jax.ShapeDtypeStruct((HEAD_PAIRS, n_chunks, V7X_LANES), F32)],
        scratch_shapes=[pltpu.VMEM((tb, V7X_LANES), F32)],
        compiler_params=_params(("parallel",)),
        name="rwkv_prep",
    )(z_r, z_r, mu, w0, w2p, a0, a2p, g2, kkp, ka, rk)


def _bmm(x, y):
    return lax.dot_general(x.astype(BF16), y.astype(BF16), (((2,), (1,)), ((0,), (0,))),
                           preferred_element_type=F32)


def _bmm_nt(x, y):
    return lax.dot_general(x.astype(BF16), y.astype(BF16), (((2,), (2,)), ((0,), (0,))),
                           preferred_element_type=F32)


def _mm_tn(x, y, precision=None):
    return lax.dot_general(x, y, (((0,), (0,)), ((), ())), preferred_element_type=F32, precision=precision)


def _group_norm_gate(y, bonus, g, lng, lnb, ones_bd):
    mean = _head_sum(y, ones_bd) * (1.0 / RWKV_HD)
    yc = y - mean
    var = _head_sum(yc * yc, ones_bd) * (1.0 / RWKV_HD)
    return (yc * lax.rsqrt(var + GN_EPS) * lng + lnb + bonus) * g


def _rwkv_chunk_kernel(at_ref, btp_ref, ktp_ref, rt_ref, v_ref, g_ref, bonus_ref, w_ref, pc_ref, lng_ref,
                       lnb_ref, buf_any, o_ref, st_ref, st_scr):
    del buf_any
    nc, c, ln = RWKV_CHUNKS_PER_STEP, RWKV_CHUNK, V7X_LANES
    t = pl.program_id(1)

    @pl.when(t == 0)
    def _():
        st_scr[...] = jnp.zeros_like(st_scr)

    shape3 = (nc, c, ln)
    at = at_ref[0].reshape(shape3)
    rt = rt_ref[0].reshape(shape3)
    v = v_ref[0].reshape(shape3)
    btk = w_ref[0].reshape(nc, c, 2 * ln)
    bt, kt = btk[..., :ln], btk[..., ln:]
    lane = lax.broadcasted_iota(jnp.int32, (1, 1, ln), 2)
    head0 = lane < RWKV_HD
    head0_2 = jnp.concatenate([head0, head0], axis=-1)
    zero = jnp.zeros(shape3, F32)

    lhs = jnp.concatenate([jnp.where(head0, at, 0.0), jnp.where(head0, rt, 0.0),
                           jnp.where(head0, 0.0, at), jnp.where(head0, 0.0, rt)], axis=1)
    gram = _bmm_nt(lhs, jnp.concatenate([bt, kt], axis=1))
    gr = lax.broadcasted_iota(jnp.int32, (1, 4 * c, 2 * c), 1)
    gc = lax.broadcasted_iota(jnp.int32, (1, 4 * c, 2 * c), 2) % c
    causal = gr % c + (gr // c) % 2 > gc
    gram = jnp.where(causal, gram, 0.0)
    zv = jnp.concatenate([jnp.zeros(shape3, BF16), v], axis=1)
    first = lax.broadcasted_iota(jnp.int32, (1, 1, 2 * c), 2) < c

    xs, mvs, mwus = [], [], []
    for h in range(2):
        g_h = gram[:, 2 * h * c:(2 * h + 2) * c, :]
        lmv = _bmm(g_h, zv)
        x = jnp.concatenate([at.astype(F32), lmv[:, :c, :]], axis=-1)
        lp = jnp.where(first, -g_h[:, :c, :], 0.0)
        pad2 = jnp.zeros((nc, c, 2 * ln), F32)
        x = x + _bmm(lp, jnp.concatenate([x, pad2], axis=1))
        n = 2
        while n < c:
            lp = _bmm(lp, jnp.concatenate([lp, zero], axis=1))
            x = x + _bmm(lp, jnp.concatenate([x, pad2], axis=1))
            n *= 2
        xs.append(x)
        mvs.append(lmv[:, c:, :])
        mrb = jnp.where(first, g_h[:, c:, :], 0.0)
        mwus.append(_bmm(mrb, jnp.concatenate([x, pad2], axis=1)))
    x = jnp.where(head0_2, xs[0], xs[1])
    mv = jnp.where(head0, mvs[0], mvs[1])
    mwu = jnp.where(head0_2, mwus[0], mwus[1])
    qe = (rt.astype(F32) - mwu[..., :ln]).astype(BF16)
    yl = mv - mwu[..., ln:]

    rr = lax.broadcasted_iota(jnp.int32, (ln, ln), 0)
    cc = lax.broadcasted_iota(jnp.int32, (ln, ln), 1)
    same_head = (rr // RWKV_HD) == (cc // RWKV_HD)
    eye = rr == cc
    btp = btp_ref[0].reshape(shape3).astype(BF16)
    ktp = ktp_ref[0].reshape(shape3).astype(BF16)
    x_bf = x.astype(BF16)
    v_bf = v.astype(BF16)
    pc = pc_ref[0]
    st = st_scr[...]
    ys = []
    for j in range(nc):
        bwu = _mm_tn(btp[j], x_bf[j])
        kv = _mm_tn(ktp[j], v_bf[j])
        tr = jnp.where(same_head, jnp.where(eye, pc[j:j + 1, :], 0.0) - bwu[:, :ln], 0.0)
        ad = jnp.where(same_head, kv - bwu[:, ln:], 0.0)
        st_hi = st.astype(BF16)
        st_lo = (st - st_hi.astype(F32)).astype(BF16)
        ys.append(jnp.dot(qe[j], st_hi, preferred_element_type=F32) + yl[j])
        tr_bf = tr.astype(BF16)
        st = (jnp.dot(tr_bf, st_hi, preferred_element_type=F32)
              + jnp.dot(tr_bf, st_lo, preferred_element_type=F32) + ad)
    st_scr[...] = st
    st_ref[0] = st
    y = jnp.concatenate(ys, axis=0)
    o = _group_norm_gate(y, bonus_ref[0], g_ref[0], lng_ref[0], lnb_ref[0], _head_block_ones())
    o_ref[0] = o.astype(o_ref.dtype)


def rwkv_chunks(prep, buf, n_batch, seq_len, lnx_g, lnx_b):
    at, btp, ktp, rt, v, g, bonus, wbk, pc = prep
    rows = RWKV_CHUNK * RWKV_CHUNKS_PER_STEP
    steps = seq_len // rows
    tok = lambda w: pl.BlockSpec((1, rows, w), lambda bp, t: (bp % HEAD_PAIRS, (bp // HEAD_PAIRS) * steps + t, 0))
    pair_row = pl.BlockSpec((1, 1, V7X_LANES), lambda bp, t: (bp % HEAD_PAIRS, 0, 0))
    return pl.pallas_call(
        _rwkv_chunk_kernel,
        grid=(n_batch * HEAD_PAIRS, steps),
        in_specs=[tok(V7X_LANES)] * 7 + [tok(2 * V7X_LANES),
                  pl.BlockSpec((1, RWKV_CHUNKS_PER_STEP, V7X_LANES),
                               lambda bp, t: (bp % HEAD_PAIRS, (bp // HEAD_PAIRS) * steps + t, 0)),
                  pair_row, pair_row, pl.BlockSpec(memory_space=pl.ANY)],
        out_specs=[tok(V7X_LANES),
                   pl.BlockSpec((1, V7X_LANES, V7X_LANES), lambda bp, t: (bp, 0, 0))],
        out_shape=[jax.ShapeDtypeStruct(buf.shape, buf.dtype),
                   jax.ShapeDtypeStruct((n_batch * HEAD_PAIRS, V7X_LANES, V7X_LANES), F32)],
        input_output_aliases={11: 0},
        scratch_shapes=[pltpu.VMEM((V7X_LANES, V7X_LANES), F32)],
        compiler_params=_params(("parallel", "arbitrary")),
        name="rwkv_chunks",
    )(at, btp, ktp, rt, v, g, bonus, wbk, pc,
      lnx_g.reshape(HEAD_PAIRS, 1, V7X_LANES), lnx_b.reshape(HEAD_PAIRS, 1, V7X_LANES), buf)


def _softmax_rows(s):
    e = jnp.exp(s - jnp.max(s, axis=-1, keepdims=True))
    return e / jnp.sum(e, axis=-1, keepdims=True)


def _xattn_kernel(q_ref, k_ref, v_ref, buf_any, o_ref):
    del buf_any
    for h in range(XA_HEADS):
        sl = slice(h * XA_HD, (h + 1) * XA_HD)
        s = lax.dot_general(q_ref[:, sl].astype(BF16), k_ref[:, sl].astype(BF16), (((1,), (1,)), ((), ())),
                            preferred_element_type=F32) * (XA_HD ** -0.5)
        pr = _softmax_rows(s)
        o_ref[:, sl] = jnp.dot(pr.astype(BF16), v_ref[:, sl].astype(BF16),
                               preferred_element_type=F32).astype(o_ref.dtype)


def xattn_prompt(z_q, kv, buf, n_batch, seq_len, *, tq=512):
    steps = seq_len // tq
    return pl.pallas_call(
        _xattn_kernel,
        grid=(n_batch, steps),
        in_specs=[pl.BlockSpec((tq, XA_D), lambda b, t: (b * steps + t, 0)),
                  pl.BlockSpec((N_MEM, XA_D), lambda b, t: (b, 0)),
                  pl.BlockSpec((N_MEM, XA_D), lambda b, t: (b, 1)),
                  pl.BlockSpec(memory_space=pl.ANY)],
        out_specs=pl.BlockSpec((tq, XA_D), lambda b, t: (b * steps + t, 0)),
        out_shape=jax.ShapeDtypeStruct(buf.shape, buf.dtype),
        input_output_aliases={3: 0},
        compiler_params=_params(("parallel", "parallel")),
        name="xattn_prompt",
    )(z_q, kv, kv, buf)


def _sample_mix_kernel(zg_ref, zr_ref, shift_ref, lng_ref, lnb_ref, gw_ref, gb_ref,
                       mu_ref, w0_ref, w2_ref, a0_ref, a2_ref, g2_ref, kkp_ref, ka_ref, rk_ref, og_any,
                       og_ref, gv_ref, wt_ref, kkt_ref, bt_ref, kt_ref, rt_ref, vt_ref, bonus_ref, g_ref):
    del og_any
    ge = _gelu(zg_ref[...])
    v = _layernorm(ge[:, GMLP_D:], lng_ref[...], lnb_ref[...])
    gv_ref[...] = v
    og_ref[...] = (ge[:, :GMLP_D] * (v * gw_ref[...] + gb_ref[...])).astype(og_ref.dtype)

    z = zr_ref[...]
    zs = z + (shift_ref[...] - z) * mu_ref[...]
    mixed = _rwkv_mix(zs, w0_ref[...], w2_ref[...], a0_ref[...], a2_ref[...], g2_ref[...], kkp_ref[...],
                      ka_ref[...], rk_ref[...], _head_block_ones())
    for p, (r, k2, vv, logw, kkn, a, g, bonus) in enumerate(mixed):
        sl = slice(p * V7X_LANES, (p + 1) * V7X_LANES)
        wt_ref[sl, :] = jnp.exp(logw).T
        kkt_ref[sl, :] = kkn.T
        bt_ref[sl, :] = (kkn * a).T
        kt_ref[sl, :] = k2.T
        rt_ref[sl, :] = r.T
        vt_ref[sl, :] = vv.T
        bonus_ref[:, sl] = bonus
        g_ref[:, sl] = g


def sample_mix(z_g, z_r, row0, shift, o_g, ln_g, ln_b, gw, gb, mu, w0, w2p, a0, a2p, g2, kkp, ka, rk):
    n = shift.shape[0]
    blk = row0 // n
    tok = lambda w: pl.BlockSpec((n, w), lambda i: (blk, 0))
    loc = lambda w: pl.BlockSpec((n, w), lambda i: (0, 0))
    row = lambda w: pl.BlockSpec((1, w), lambda i: (0, 0))
    full2 = lambda a, b: pl.BlockSpec((a, b), lambda i: (0, 0))
    chan = jax.ShapeDtypeStruct((RWKV_D, n), F32)
    return pl.pallas_call(
        _sample_mix_kernel,
        grid=(1,),
        in_specs=[tok(GMLP_COLS), tok(RWKV_COLS), loc(RWKV_COLS),
                  row(GMLP_D), row(GMLP_D), row(GMLP_D), row(GMLP_D),
                  row(RWKV_COLS), row(RWKV_D), full2(LORA_W + LORA_A, RWKV_D), row(RWKV_D),
                  full2(LORA_W + LORA_A, RWKV_D), full2(LORA_G, RWKV_D), row(RWKV_D), row(RWKV_D), row(RWKV_D),
                  pl.BlockSpec(memory_space=pl.ANY)],
        out_specs=[tok(GMLP_D), loc(GMLP_D)] + [full2(RWKV_D, n)] * 6 + [loc(RWKV_D), loc(RWKV_D)],
        out_shape=[jax.ShapeDtypeStruct(o_g.shape, o_g.dtype), jax.ShapeDtypeStruct((n, GMLP_D), F32)]
                  + [chan] * 6 + [jax.ShapeDtypeStruct((n, RWKV_D), F32)] * 2,
        input_output_aliases={16: 0},
        compiler_params=_params(("arbitrary",)),
        name="sample_mix",
    )(z_g, z_r, shift, ln_g.reshape(1, -1), ln_b.reshape(1, -1), gw, gb, mu, w0, w2p, a0, a2p, g2, kkp, ka, rk, o_g)


def _sample_state_kernel(s_ref, w_ref, kk_ref, b_ref, k_ref, r_ref, v_ref, snew_ref, y_ref):
    n = s_ref.shape[0]
    st = s_ref[...].T.reshape(RWKV_HD, RWKV_HD, n)
    sa = jnp.sum(st * (-kk_ref[...])[None], axis=1, keepdims=True)
    st_new = st * w_ref[...][None] + sa * b_ref[...][None] + v_ref[...] * k_ref[...][None]
    y_ref[...] = jnp.sum(st_new * r_ref[...][None], axis=1, keepdims=True)
    snew_ref[...] = st_new.reshape(RWKV_HD * RWKV_HD, n).T


def sample_state(s0, wt, kkt, bt, kt, rt, vt):
    n = s0.shape[0]
    hd2 = RWKV_HD * RWKV_HD
    state = pl.BlockSpec((n, hd2), lambda h: (0, h))
    keyvec = pl.BlockSpec((RWKV_HD, n), lambda h: (h, 0))
    valvec = pl.BlockSpec((RWKV_HD, 1, n), lambda h: (h, 0, 0))
    return pl.pallas_call(
        _sample_state_kernel,
        grid=(RWKV_HEADS,),
        in_specs=[state, keyvec, keyvec, keyvec, keyvec, keyvec, valvec],
        out_specs=[state, valvec],
        out_shape=[jax.ShapeDtypeStruct((n, RWKV_HEADS * hd2), F32),
                   jax.ShapeDtypeStruct((RWKV_D, 1, n), F32)],
        compiler_params=_params(("parallel",)),
        name="sample_state",
    )(s0.reshape(n, RWKV_HEADS * hd2), wt, kkt, bt, kt, rt, vt.reshape(RWKV_D, 1, n))


def _sample_rwkv_out_kernel(yt_ref, bonus_ref, g_ref, lng_ref, lnb_ref, or_any, or_ref):
    del or_any
    y = yt_ref[...].T
    ones_bd = _head_block_ones()
    for p in range(HEAD_PAIRS):
        sl = slice(p * V7X_LANES, (p + 1) * V7X_LANES)
        o = _group_norm_gate(y[:, sl], bonus_ref[:, sl], g_ref[:, sl], lng_ref[p], lnb_ref[p], ones_bd)
        or_ref[p] = o.astype(or_ref.dtype)


def sample_rwkv_out(yt, bonus, g, lnx_g, lnx_b, o_r, row0):
    n = bonus.shape[0]
    full2 = lambda a, b: pl.BlockSpec((a, b), lambda i: (0, 0))
    pair_row = pl.BlockSpec((HEAD_PAIRS, 1, V7X_LANES), lambda i: (0, 0, 0))
    return pl.pallas_call(
        _sample_rwkv_out_kernel,
        grid=(1,),
        in_specs=[full2(RWKV_D, n), full2(n, RWKV_D), full2(n, RWKV_D), pair_row, pair_row,
                  pl.BlockSpec(memory_space=pl.ANY)],
        out_specs=pl.BlockSpec((HEAD_PAIRS, n, V7X_LANES), lambda i: (0, row0 // n, 0)),
        out_shape=jax.ShapeDtypeStruct(o_r.shape, o_r.dtype),
        input_output_aliases={5: 0},
        compiler_params=_params(("arbitrary",)),
        name="sample_rwkv_out",
    )(yt, bonus, g, lnx_g.reshape(HEAD_PAIRS, 1, V7X_LANES), lnx_b.reshape(HEAD_PAIRS, 1, V7X_LANES), o_r)


def _xattn_sample_kernel(q_ref, ck_ref, cv_ref, ox_any, ox_ref, *, sb):
    del ox_any
    q = q_ref[...]
    rows = N_MEM * XA_HEADS
    hrow = lax.broadcasted_iota(jnp.int32, (8, rows), 0) % XA_HEADS
    hcol = lax.broadcasted_iota(jnp.int32, (8, rows), 1) % XA_HEADS
    own = hrow == hcol
    h8 = lax.broadcasted_iota(jnp.int32, (8, XA_HD), 0) % XA_HEADS
    nt = (((1,), (1,)), ((), ()))
    outs = []
    for s in range(sb):
        qm = jnp.zeros((8, XA_HD), F32)
        for h in range(XA_HEADS):
            qm = jnp.where(h8 == h, q[s:s + 1, h * XA_HD:(h + 1) * XA_HD], qm)
        sc = lax.dot_general(qm.astype(BF16), ck_ref[s].astype(BF16), nt,
                             preferred_element_type=F32) * (XA_HD ** -0.5)
        pr = _softmax_rows(jnp.where(own, sc, NEG_INF))
        res = jnp.dot(pr.astype(BF16), cv_ref[s].astype(BF16), preferred_element_type=F32)
        outs.append(jnp.concatenate([res[h:h + 1, :] for h in range(XA_HEADS)], axis=1))
    ox_ref[...] = jnp.concatenate(outs, axis=0).astype(ox_ref.dtype)


def xattn_sample(z_q, ck, cv, o_x, row0, *, sb=8):
    n = ck.shape[0]
    off = row0 // sb
    tok = pl.BlockSpec((sb, XA_D), lambda i: (i + off, 0))
    cache = pl.BlockSpec((sb, N_MEM * XA_HEADS, XA_HD), lambda i: (i, 0, 0))
    return pl.pallas_call(
        functools.partial(_xattn_sample_kernel, sb=sb),
        grid=(n // sb,),
        in_specs=[tok, cache, cache, pl.BlockSpec(memory_space=pl.ANY)],
        out_specs=tok,
        out_shape=jax.ShapeDtypeStruct(o_x.shape, o_x.dtype),
        input_output_aliases={3: 0},
        compiler_params=_params(("arbitrary",)),
        name="xattn_sample",
    )(z_q, ck, cv, o_x)


def _merge_kernel(og_ref, or_ref, ox_ref, g0_ref, g1_ref, g2_ref, wg_ref, wr_ref, wx_ref, o_ref,
                  wg_bf, wr_bf, wx_bf):
    @pl.when(pl.program_id(1) == 0)
    def _():
        wg_bf[...] = wg_ref[...].astype(BF16)
        wr_bf[...] = wr_ref[...].astype(BF16)
        wx_bf[...] = wx_ref[...].astype(BF16)

    up_g = jnp.dot(og_ref[...], wg_bf[...], preferred_element_type=F32)
    up_r = jnp.dot(or_ref[0], wr_bf[0:V7X_LANES, :], preferred_element_type=F32)
    for p in range(1, HEAD_PAIRS):
        up_r = up_r + jnp.dot(or_ref[p], wr_bf[p * V7X_LANES:(p + 1) * V7X_LANES, :], preferred_element_type=F32)
    up_x = jnp.dot(ox_ref[...], wx_bf[...], preferred_element_type=F32)
    gate = lambda ref: _sigmoid(ref[...].astype(F32))
    merged = gate(g0_ref) * up_g + gate(g1_ref) * up_r + gate(g2_ref) * up_x
    o_ref[...] = merged.astype(o_ref.dtype)


def merge(o_g, o_r, o_x, z_gate, w_up_g, w_up_r, w_up_x, *, tb, nb=512):
    m = o_g.shape[0]
    nblk = D_MODEL // nb
    gate = lambda b: pl.BlockSpec((tb, nb), lambda j, i: (i, b * nblk + j))
    wspec = lambda k: pl.BlockSpec((k, nb), lambda j, i: (0, j))
    return pl.pallas_call(
        _merge_kernel,
        grid=(nblk, m // tb),
        in_specs=[pl.BlockSpec((tb, GMLP_D), lambda j, i: (i, 0)),
                  pl.BlockSpec((HEAD_PAIRS, tb, V7X_LANES), lambda j, i: (0, i, 0)),
                  pl.BlockSpec((tb, XA_D), lambda j, i: (i, 0)),
                  gate(0), gate(1), gate(2), wspec(GMLP_D), wspec(RWKV_D), wspec(XA_D)],
        out_specs=pl.BlockSpec((tb, nb), lambda j, i: (i, j)),
        out_shape=jax.ShapeDtypeStruct((m, D_MODEL), BF16),
        scratch_shapes=[pltpu.VMEM((GMLP_D, nb), BF16), pltpu.VMEM((RWKV_D, nb), BF16),
                        pltpu.VMEM((XA_D, nb), BF16)],
        compiler_params=_params(("arbitrary", "arbitrary")),
        name="merge",
    )(o_g, o_r, o_x, z_gate, z_gate, z_gate, w_up_g, w_up_r, w_up_x)


def _extract_top(src_ref, work_ref, rank_ref, vals_ref, n_rows):
    width = work_ref.shape[1]
    riota = lax.broadcasted_iota(jnp.int32, (n_rows, V7X_LANES), 0)
    kiota = lax.broadcasted_iota(jnp.int32, (TOPK, V7X_LANES), 0)

    def run(break_ties):
        work_ref[...] = src_ref[...]
        vals_ref[...] = jnp.zeros_like(vals_ref)

        def body(p, carry):
            taken = TAKEN_BASE - jnp.asarray(p, F32) * TAKEN_STEP
            for c in range(width // V7X_LANES):
                sl = slice(c * V7X_LANES, (c + 1) * V7X_LANES)
                w = work_ref[:, sl]
                m = jnp.max(w, axis=0, keepdims=True)
                if break_ties:
                    idx = jnp.min(jnp.where(w == m, riota, n_rows), axis=0, keepdims=True)
                    hit = riota == idx
                else:
                    hit = w == m
                work_ref[:, sl] = jnp.where(hit, taken, w)
                vals_ref[:, sl] = jnp.where(kiota == p, m, vals_ref[:, sl])
            return carry

        lax.fori_loop(0, TOPK, body, 0)

    run(False)
    picked = jnp.sum(jnp.where(work_ref[...] < TAKEN_BELOW, 1.0, 0.0), axis=0, keepdims=True)
    tied = jnp.max(picked) > TOPK

    @pl.when(tied)
    def _():
        run(True)

    w = work_ref[...]
    rank = jnp.where(w < TAKEN_BELOW, w * (-1.0 / TAKEN_STEP) + (TAKEN_BASE / TAKEN_STEP), float(TOPK))
    rank_ref[...] = rank.astype(jnp.int32)


_CAND_COUNT = tuple(TOPK // (a + 1) for a in range(TOPK))
_CAND_START = tuple(sum(_CAND_COUNT[:a]) for a in range(TOPK))
_CAND_ROWS = -(-sum(_CAND_COUNT) // 8) * 8


def _peer_topk_kernel(q_ref, keys_ref, r2_ref, lim_ref, e1_ref, e2_ref,
                      s_scr, work_scr, rank_scr, vals_scr, cand_scr, cwork_scr, crank_scr, cvals_scr, *, tbk):
    nt = (((1,), (1,)), ((), ()))
    for h in range(PEER_HEADS):
        for c in range(2):
            qcol = (2 * h + c) * PEER_DH
            col = (2 * h + c) * tbk
            k_hi, k_lo = _split_bf16(keys_ref[h, c])
            q_hi, q_lo = _split_bf16(q_ref[:, qcol:qcol + PEER_DH])
            sc = (lax.dot_general(k_hi, q_hi, nt, preferred_element_type=F32)
                  + lax.dot_general(k_hi, q_lo, nt, preferred_element_type=F32)
                  + lax.dot_general(k_lo, q_hi, nt, preferred_element_type=F32))
            s_scr[:, col:col + tbk] = sc
    _extract_top(s_scr, work_scr, rank_scr, vals_scr, N_KEYS)

    crow = lax.broadcasted_iota(jnp.int32, (_CAND_ROWS, tbk), 0)
    seg = jnp.full((_CAND_ROWS, tbk), TOPK, jnp.int32)
    for a in reversed(range(TOPK)):
        seg = jnp.where(crow < _CAND_START[a] + _CAND_COUNT[a], jnp.minimum(seg, a), seg)
    pad = jnp.zeros((_CAND_ROWS - TOPK, tbk), F32)
    for h in range(PEER_HEADS):
        v1 = vals_scr[:, (2 * h) * tbk:(2 * h + 1) * tbk]
        v2 = jnp.concatenate([vals_scr[:, (2 * h + 1) * tbk:(2 * h + 2) * tbk], pad], axis=0)
        cand = jnp.full((_CAND_ROWS, tbk), CAND_PAD, F32)
        for a in range(TOPK):
            shifted = v2 if _CAND_START[a] == 0 else pltpu.roll(v2, _CAND_START[a], axis=0)
            cand = jnp.where(seg == a, v1[a:a + 1, :] + shifted, cand)
        cand_scr[:, h * tbk:(h + 1) * tbk] = cand
    _extract_top(cand_scr, cwork_scr, crank_scr, cvals_scr, _CAND_ROWS)

    for h in range(PEER_HEADS):
        hs = slice(h * tbk, (h + 1) * tbk)
        s1 = slice((2 * h) * tbk, (2 * h + 1) * tbk)
        s2 = slice((2 * h + 1) * tbk, (2 * h + 2) * tbk)
        cvals = cvals_scr[:, hs]
        z = jnp.sum(jnp.exp(cvals - cvals[0:1, :]), axis=0, keepdims=True)
        chosen = crank_scr[:, hs] < TOPK
        rank1 = rank_scr[:, s1]
        lim = jnp.zeros((N_KEYS, tbk), F32)
        for a in range(TOPK):
            count = jnp.sum(jnp.where(jnp.logical_and(chosen, seg == a), 1.0, 0.0), axis=0, keepdims=True)
            lim = jnp.where(rank1 == a, count, lim)
        lim_ref[h] = lim
        r2_ref[h] = rank_scr[:, s2].astype(F32).astype(r2_ref.dtype)
        e1_ref[h] = jnp.exp(s_scr[:, s1] - vals_scr[0:1, s1]) / z
        e2_ref[h] = jnp.exp(s_scr[:, s2] - vals_scr[0:1, s2]).astype(e2_ref.dtype)


def peer_topk(q, keys, *, tbk=128):
    m = q.shape[0]
    out = pl.BlockSpec((PEER_HEADS, N_KEYS, tbk), lambda i: (0, 0, i))
    shape = lambda dt: jax.ShapeDtypeStruct((PEER_HEADS, N_KEYS, m), dt)
    wide = 2 * PEER_HEADS * tbk
    return pl.pallas_call(
        functools.partial(_peer_topk_kernel, tbk=tbk),
        grid=(m // tbk,),
        in_specs=[pl.BlockSpec((tbk, PEER_HEADS * 2 * PEER_DH), lambda i: (i, 0)),
                  pl.BlockSpec((PEER_HEADS, 2, N_KEYS, PEER_DH), lambda i: (0, 0, 0, 0))],
        out_specs=[out] * 4,
        out_shape=[shape(BF16), shape(F32), shape(F32), shape(BF16)],
        scratch_shapes=[pltpu.VMEM((N_KEYS, wide), F32), pltpu.VMEM((N_KEYS, wide), F32),
                        pltpu.VMEM((N_KEYS, wide), jnp.int32), pltpu.VMEM((TOPK, wide), F32),
                        pltpu.VMEM((_CAND_ROWS, PEER_HEADS * tbk), F32),
                        pltpu.VMEM((_CAND_ROWS, PEER_HEADS * tbk), F32),
                        pltpu.VMEM((_CAND_ROWS, PEER_HEADS * tbk), jnp.int32),
                        pltpu.VMEM((TOPK, PEER_HEADS * tbk), F32)],
        compiler_params=_params(("parallel",)),
        name="peer_topk",
    )(q, keys)


PEER_SUB = 2 * N_KEYS
TOKEN_TILE = 768


def _peer_dense_kernel(xt_ref, r2_ref, lim_ref, e1_ref, e2_ref, u_ref, v_ref, o_ref, coef_a, coef_b, ht_scr,
                       *, eb, n_blocks):
    e = pl.program_id(1)
    blk = jnp.minimum(e, n_blocks - 1)
    n_sub = eb // PEER_SUB
    d_sub = o_ref.shape[1] // n_sub

    def hidden(s):
        ht_scr[s % 2] = jnp.dot(u_ref[s * PEER_SUB:(s + 1) * PEER_SUB, :], xt_ref[...],
                                preferred_element_type=F32)

    def step(prev_ref, next_ref):
        hidden(0)
        for s in range(n_sub):
            ht = ht_scr.at[s % 2]
            cols = slice(s * d_sub, (s + 1) * d_sub)
            if prev_ref is not None:
                o_ref[:, cols] += lax.dot_general(prev_ref[...], v_ref[:, cols], (((0,), (0,)), ((), ())),
                                                  preferred_element_type=F32)
            if s + 1 < n_sub:
                hidden(s + 1)
            for ii in range(PEER_SUB // N_KEYS):
                i = blk * (eb // N_KEYS) + s * (PEER_SUB // N_KEYS) + ii
                row = s * PEER_SUB + ii * N_KEYS
                gate = None
                for h in range(PEER_HEADS):
                    lim = lim_ref[h, pl.ds(i, 1), :].astype(BF16)
                    e1 = e1_ref[h, pl.ds(i, 1), :].astype(BF16)
                    term = jnp.where(r2_ref[h] < lim, e2_ref[h] * e1, jnp.zeros((), BF16))
                    gate = term if gate is None else gate + term
                next_ref[row:row + N_KEYS, :] = gate * _gelu_to_bf16(ht[ii * N_KEYS:(ii + 1) * N_KEYS, :])

    @pl.when(e == 0)
    def _():
        o_ref[...] = jnp.zeros_like(o_ref)
        step(None, coef_a)

    @pl.when(jnp.logical_and(e % 2 == 0, jnp.logical_and(e > 0, e < n_blocks)))
    def _():
        step(coef_b, coef_a)

    @pl.when(e % 2 == 1)
    def _():
        step(coef_a, coef_b)

    @pl.when(e == n_blocks)
    def _():
        o_ref[...] += lax.dot_general(coef_b[...], v_ref[...], (((0,), (0,)), ((), ())),
                                      preferred_element_type=F32)


def peer_dense(xt, r2, lim, e1, e2, u_bf, v_bf, *, tbl, eb=1024):
    d, m = xt.shape
    n_blocks = u_bf.shape[0] // eb
    assert n_blocks % 2 == 0
    once = pl.Buffered(1)
    head = pl.BlockSpec((PEER_HEADS, N_KEYS, tbl), lambda t, e: (0, 0, t), pipeline_mode=once)
    return pl.pallas_call(
        functools.partial(_peer_dense_kernel, eb=eb, n_blocks=n_blocks),
        grid=(m // tbl, n_blocks + 1),
        in_specs=[pl.BlockSpec((d, tbl), lambda t, e: (0, t)), head, head, head, head,
                  pl.BlockSpec((eb, d), lambda t, e: (jnp.minimum(e, n_blocks - 1), 0)),
                  pl.BlockSpec((eb, d), lambda t, e: (jnp.maximum(e - 1, 0), 0))],
        out_specs=pl.BlockSpec((tbl, d), lambda t, e: (t, 0)),
        out_shape=jax.ShapeDtypeStruct((m, d), F32),
        scratch_shapes=[pltpu.VMEM((eb, tbl), BF16), pltpu.VMEM((eb, tbl), BF16),
                        pltpu.VMEM((2, PEER_SUB, tbl), F32)],
        compiler_params=_params(("parallel", "arbitrary")),
        name="peer_dense",
    )(xt, r2, lim, e1, e2, u_bf, v_bf)


def _final_kernel(h_ref, p_ref, g_ref, o_ref):
    x = h_ref[...] + p_ref[...]
    o_ref[...] = x * lax.rsqrt(jnp.mean(x * x, axis=-1, keepdims=True) + RMS_EPS) * g_ref[...]


def final_norm(h, peer, g, row0, n_rows, *, tb):
    d = h.shape[1]
    off = row0 // tb
    rows = pl.BlockSpec((tb, d), lambda i: (i + off, 0))
    return pl.pallas_call(
        _final_kernel,
        grid=(n_rows // tb,),
        in_specs=[rows, rows, pl.BlockSpec((1, d), lambda i: (0, 0))],
        out_specs=pl.BlockSpec((tb, d), lambda i: (i, 0)),
        out_shape=jax.ShapeDtypeStruct((n_rows, d), F32),
        compiler_params=_params(("parallel",)),
        name="final_norm",
    )(h, peer, g.reshape(1, d))


def kernel(x_prompt, x_sample, mem_prompt, state_shift, state_wkv, cache_mem_k, cache_mem_v, ln1_g, w_in, gmlp_ln_g, gmlp_ln_b, gmlp_ws, gmlp_bs, rwkv_mu, rwkv_w0, rwkv_w2, rwkv_a0, rwkv_a2, rwkv_g2, rwkv_kk, rwkv_ka, rwkv_rk, rwkv_lnx_g, rwkv_lnx_b, mem_norm_g, w_mem_kv, w_up_g, w_up_r, w_up_x, w_out, ln2_g, peer_wq, peer_keys, peer_u, peer_v, final_g):
    depth = w_in.shape[0]
    assert depth == 1, "single-layer step"
    l = 0
    n_batch, seq_len, d = x_prompt.shape
    n_dec = x_sample.shape[0]
    n_prompt = n_batch * seq_len
    m = -(-(n_prompt + n_dec) // TOKEN_TILE) * TOKEN_TILE
    n_pad = m - n_prompt - n_dec
    assert n_pad <= n_prompt
    tb_small = TOKEN_TILE
    tb_big = _largest_divisor(m, (1408, TOKEN_TILE))

    xp_rows = x_prompt.reshape(n_prompt, d)
    x = jnp.concatenate([xp_rows, x_sample.reshape(n_dec, d), xp_rows[:n_pad]], axis=0)
    xn = rmsnorm(x, ln1_g[l], tb=tb_small, out_dtype=BF16)
    proj = functools.partial(matmul, xn, w_in[l], tb=tb_big, nb=512)
    z_g = proj(col_off=0, n_cols=GMLP_COLS, name="proj_gmlp")
    z_r = proj(col_off=GMLP_COLS, n_cols=RWKV_COLS, name="proj_rwkv")
    z_q = proj(col_off=GMLP_COLS + RWKV_COLS, n_cols=XA_D, name="proj_xattn")
    z_gate = proj(col_off=GMLP_COLS + RWKV_COLS + XA_D, n_cols=GATE_COLS, out_dtype=BF16, name="proj_gate")

    memn = rmsnorm(mem_prompt.reshape(n_batch * N_MEM, d), mem_norm_g[l], tb=256, out_dtype=BF16)
    kv = matmul(memn, w_mem_kv[l], tb=N_MEM, nb=512, name="proj_mem_kv")
    p_mk = kv[:, :XA_D].reshape(1, n_batch, N_MEM, XA_HEADS, XA_HD)
    p_mv = kv[:, XA_D:].reshape(1, n_batch, N_MEM, XA_HEADS, XA_HD)

    row = lambda a: a.reshape(1, -1)
    zeros_lora = jnp.zeros((LORA_W, RWKV_D), F32)
    w2p = jnp.concatenate([rwkv_w2[l], zeros_lora], axis=0).astype(BF16)
    a2p = jnp.concatenate([zeros_lora, rwkv_a2[l]], axis=0).astype(BF16)
    rw = (row(rwkv_mu[l]), row(rwkv_w0[l]), w2p, row(rwkv_a0[l]), a2p, rwkv_g2[l].astype(BF16),
          row(rwkv_kk[l]), row(rwkv_ka[l]), row(rwkv_rk[l]))

    o_g = gmlp_prompt(z_g, jnp.zeros((m, GMLP_D), BF16), n_prompt, gmlp_ln_g[l], gmlp_ln_b[l], gmlp_ws[l],
                      gmlp_bs[l])
    prep = rwkv_prep(z_r, n_prompt, seq_len, *rw)
    o_r, st = rwkv_chunks(prep, jnp.zeros((HEAD_PAIRS, m, V7X_LANES), BF16), n_batch, seq_len,
                          rwkv_lnx_g[l], rwkv_lnx_b[l])
    o_x = xattn_prompt(z_q, kv, jnp.zeros((m, XA_D), BF16), n_batch, seq_len)

    gw = jnp.repeat(gmlp_ws[l][:, 0, 0], CHUNK).reshape(1, GMLP_D)
    gb = jnp.repeat(gmlp_bs[l][:, 0], CHUNK).reshape(1, GMLP_D)
    o_g, s_gv, wt, kkt, bt, kt, rt, vt, bonus_s, g_s = sample_mix(
        z_g, z_r, n_prompt, state_shift[l], o_g, gmlp_ln_g[l], gmlp_ln_b[l], gw, gb, *rw)
    s_wkv, y_s = sample_state(state_wkv[l], wt, kkt, bt, kt, rt, vt)
    o_r = sample_rwkv_out(y_s.reshape(RWKV_D, n_dec), bonus_s, g_s, rwkv_lnx_g[l], rwkv_lnx_b[l], o_r, n_prompt)
    o_x = xattn_sample(z_q, cache_mem_k[l].reshape(n_dec, N_MEM * XA_HEADS, XA_HD),
                       cache_mem_v[l].reshape(n_dec, N_MEM * XA_HEADS, XA_HD), o_x, n_prompt)

    merged = merge(o_g, o_r, o_x, z_gate, w_up_g[l], w_up_r[l], w_up_x[l], tb=tb_small)
    h = matmul(merged, w_out[l], tb=tb_big, nb=512, residual=x, name="proj_out")

    hn, hn_t = rmsnorm(h, ln2_g[l], tb=tb_small, out_dtype=BF16, transposed=True)
    q = matmul(hn, peer_wq[l], tb=tb_big, nb=512, name="proj_peer_q")
    r2, lim, e1, e2 = peer_topk(q, peer_keys[l])
    peer = peer_dense(hn_t, r2, lim, e1, e2, peer_u[l].astype(BF16), peer_v[l].astype(BF16), tbl=tb_small)
    y_prompt = final_norm(h, peer, final_g, 0, n_prompt, tb=512)
    y_sample = final_norm(h, peer, final_g, n_prompt, n_dec, tb=n_dec)

    st = st.reshape(n_batch, HEAD_PAIRS, 2, RWKV_HD, 2, RWKV_HD)
    p_wkv = jnp.stack([st[:, :, 0, :, 0, :], st[:, :, 1, :, 1, :]], axis=2)
    p_wkv = jnp.swapaxes(p_wkv, -1, -2).reshape(1, n_batch, RWKV_HEADS, RWKV_HD, RWKV_HD)
    p_shift = jnp.concatenate([z_r[(b + 1) * seq_len - 1:(b + 1) * seq_len] for b in range(n_batch)], axis=0)[None]
    s_shift = z_r[n_prompt:n_prompt + n_dec][None]
    s_wkv = s_wkv.reshape(1, n_dec, RWKV_HEADS, RWKV_HD, RWKV_HD)
    return (y_prompt.reshape(n_batch, seq_len, d), y_sample.reshape(n_dec, 1, d), p_mk, p_mv,
            p_shift, p_wkv, s_shift, s_wkv, s_gv.reshape(1, n_dec, 1, GMLP_D))
```

```python
import functools

import jax
import jax.numpy as jnp
from jax import lax
from jax.experimental import pallas as pl
from jax.experimental.pallas import tpu as pltpu

F32 = jnp.float32
BF16 = jnp.bfloat16

D_MODEL = 2048
CHUNK = 128
GMLP_GROUPS = 6
GMLP_D = 768
RWKV_HEADS = 12
RWKV_HD = 64
RWKV_D = 768
LORA_W = 64
LORA_A = 64
LORA_G = 128
RWKV_COLS = 2560
XA_HEADS = 4
XA_HD = 128
XA_D = 512
N_MEM = 256
GMLP_COLS = 2 * GMLP_D
GATE_COLS = 3 * D_MODEL
PEER_HEADS = 8
N_KEYS = 128
PEER_DH = 128
TOPK = 16
RMS_EPS = 1e-6
LN_EPS = 1e-5
GN_EPS = 64e-5

V7X_LANES = 128
V7X_VMEM_LIMIT_BYTES = 56 * 1024 * 1024

HEAD_PAIRS = RWKV_HEADS // 2
RWKV_CHUNK = 64
RWKV_CHUNKS_PER_STEP = 32
NEG_INF = float("-inf")
CAND_PAD = -(2.0 ** 100)
TAKEN_BELOW = -(2.0 ** 110)
TAKEN_BASE = -(2.0 ** 120)
TAKEN_STEP = 2.0 ** 115


def _params(semantics):
    return pltpu.CompilerParams(dimension_semantics=semantics, vmem_limit_bytes=V7X_VMEM_LIMIT_BYTES)


def _largest_divisor(m, candidates):
    return next(c for c in candidates if m % c == 0)


def _gelu(x):
    return 0.5 * x * (1.0 + jnp.tanh(0.7978845608028654 * (x + 0.044715 * (x * x * x))))


def _gelu_to_bf16(x):
    inner = (0.7978845608028654 * x) * (1.0 + 0.044715 * (x * x))
    xb = x.astype(BF16)
    return (0.5 * xb) * (1.0 + jnp.tanh(inner.astype(BF16)))


def _split_bf16(x):
    hi = x.astype(BF16)
    return hi, (x - hi.astype(F32)).astype(BF16)


def _split3_bf16(x):
    hi = x.astype(BF16)
    rest = x - hi.astype(F32)
    mid = rest.astype(BF16)
    return hi, mid, (rest - mid.astype(F32)).astype(BF16)


def _sigmoid(x):
    return 0.5 + 0.5 * jnp.tanh(0.5 * x)


def _rmsnorm_kernel(x_ref, g_ref, *o_refs, transposed):
    x = x_ref[...]
    y = x * lax.rsqrt(jnp.mean(x * x, axis=-1, keepdims=True) + RMS_EPS) * g_ref[...]
    o_refs[0][...] = y.astype(o_refs[0].dtype)
    if transposed:
        o_refs[1][...] = y.T.astype(o_refs[1].dtype)


def rmsnorm(x, g, *, tb, out_dtype, transposed=False):
    m, d = x.shape
    out_shape = [jax.ShapeDtypeStruct((m, d), out_dtype)]
    out_specs = [pl.BlockSpec((tb, d), lambda i: (i, 0))]
    if transposed:
        out_shape.append(jax.ShapeDtypeStruct((d, m), out_dtype))
        out_specs.append(pl.BlockSpec((d, tb), lambda i: (0, i)))
    res = pl.pallas_call(
        functools.partial(_rmsnorm_kernel, transposed=transposed),
        grid=(m // tb,),
        in_specs=[pl.BlockSpec((tb, d), lambda i: (i, 0)), pl.BlockSpec((1, d), lambda i: (0, 0))],
        out_specs=out_specs,
        out_shape=out_shape,
        compiler_params=_params(("parallel",)),
        name="rmsnorm_t" if transposed else "rmsnorm",
    )(x, g.reshape(1, d))
    return res if transposed else res[0]


def _matmul_kernel(a_ref, w_ref, *rest, has_residual):
    if has_residual:
        r_ref, o_ref, wbf_ref = rest
    else:
        o_ref, wbf_ref = rest

    @pl.when(pl.program_id(1) == 0)
    def _():
        wbf_ref[...] = w_ref[...].astype(BF16)

    acc = jnp.dot(a_ref[...], wbf_ref[...], preferred_element_type=F32)
    if has_residual:
        acc = acc + r_ref[...]
    o_ref[...] = acc.astype(o_ref.dtype)


def matmul(a, w, *, tb, nb, col_off=0, n_cols=None, residual=None, out_dtype=F32, name="matmul"):
    m, k = a.shape
    n = w.shape[1] if n_cols is None else n_cols
    off = col_off // nb
    in_specs = [pl.BlockSpec((tb, k), lambda j, i: (i, 0)),
                pl.BlockSpec((k, nb), lambda j, i: (0, j + off))]
    args = [a, w]
    if residual is not None:
        in_specs.append(pl.BlockSpec((tb, nb), lambda j, i: (i, j)))
        args.append(residual)
    return pl.pallas_call(
        functools.partial(_matmul_kernel, has_residual=residual is not None),
        grid=(n // nb, m // tb),
        in_specs=in_specs,
        out_specs=pl.BlockSpec((tb, nb), lambda j, i: (i, j)),
        out_shape=jax.ShapeDtypeStruct((m, n), out_dtype),
        scratch_shapes=[pltpu.VMEM((k, nb), BF16)],
        compiler_params=_params(("arbitrary", "arbitrary")),
        name=name,
    )(*args)


def _layernorm(v, g, b):
    vc = v - jnp.mean(v, axis=-1, keepdims=True)
    var = jnp.mean(vc * vc, axis=-1, keepdims=True)
    return vc * lax.rsqrt(var + LN_EPS) * g + b


def _gmlp_kernel(z_ref, lng_ref, lnb_ref, ws_ref, bst_ref, buf_any, o_ref):
    del buf_any
    ge = _gelu(z_ref[...])
    u = ge[:, :GMLP_D]
    v = _layernorm(ge[:, GMLP_D:], lng_ref[...], lnb_ref[...])
    row = lax.broadcasted_iota(jnp.int32, (CHUNK, CHUNK), 0)
    col = lax.broadcasted_iota(jnp.int32, (CHUNK, CHUNK), 1)
    causal = col <= row
    for g in range(GMLP_GROUPS):
        sl = slice(g * CHUNK, (g + 1) * CHUNK)
        wm = jnp.where(causal, ws_ref[g], 0.0).astype(BF16)
        mixed = jnp.dot(wm, v[:, sl].astype(BF16), preferred_element_type=F32) + bst_ref[:, g:g + 1]
        o_ref[:, sl] = (u[:, sl] * mixed).astype(o_ref.dtype)


def gmlp_prompt(z_g, buf, n_tokens, ln_g, ln_b, ws, bs):
    return pl.pallas_call(
        _gmlp_kernel,
        grid=(n_tokens // CHUNK,),
        in_specs=[pl.BlockSpec((CHUNK, GMLP_COLS), lambda i: (i, 0)),
                  pl.BlockSpec((1, GMLP_D), lambda i: (0, 0)),
                  pl.BlockSpec((1, GMLP_D), lambda i: (0, 0)),
                  pl.BlockSpec((GMLP_GROUPS, CHUNK, CHUNK), lambda i: (0, 0, 0)),
                  pl.BlockSpec((CHUNK, GMLP_GROUPS), lambda i: (0, 0)),
                  pl.BlockSpec(memory_space=pl.ANY)],
        out_specs=pl.BlockSpec((CHUNK, GMLP_D), lambda i: (i, 0)),
        out_shape=jax.ShapeDtypeStruct(buf.shape, buf.dtype),
        input_output_aliases={5: 0},
        compiler_params=_params(("parallel",)),
        name="gmlp_prompt",
    )(z_g, ln_g.reshape(1, GMLP_D), ln_b.reshape(1, GMLP_D), ws, bs.T, buf)


def _head_block_ones():
    r = lax.broadcasted_iota(jnp.int32, (V7X_LANES, V7X_LANES), 0) // RWKV_HD
    c = lax.broadcasted_iota(jnp.int32, (V7X_LANES, V7X_LANES), 1) // RWKV_HD
    return (r == c).astype(F32)


def _head_sum(x, ones_bd):
    ones = ones_bd.astype(BF16)
    return sum(jnp.dot(t, ones, preferred_element_type=F32) for t in _split3_bf16(x))


def _rwkv_mix(zs, w0, w2p, a0, a2p, g2, kkp, ka, rk, ones_bd):
    r = zs[:, 0:RWKV_D]
    k = zs[:, RWKV_D:2 * RWKV_D]
    v = zs[:, 2 * RWKV_D:3 * RWKV_D]
    lwa = zs[:, 3 * RWKV_D:3 * RWKV_D + LORA_W + LORA_A]
    lg = zs[:, 3 * RWKV_D + LORA_W + LORA_A:]
    lora_w = jnp.dot(jnp.tanh(lwa).astype(BF16), w2p, preferred_element_type=F32)
    lora_a = jnp.dot(lwa.astype(BF16), a2p, preferred_element_type=F32)
    x = -(w0 + lora_w)
    softplus = jnp.maximum(x, 0.0) + jnp.log(1.0 + jnp.exp(-jnp.abs(x)))
    logw = -jnp.exp(-softplus - 0.5)
    a = _sigmoid(a0 + lora_a)
    g = jnp.dot(_sigmoid(lg).astype(BF16), g2, preferred_element_type=F32)
    kk = k * kkp
    k2 = k * (1.0 + (a - 1.0) * ka)
    rkk = r * k2 * rk
    out = []
    for p in range(HEAD_PAIRS):
        sl = slice(p * V7X_LANES, (p + 1) * V7X_LANES)
        kk_p = kk[:, sl]
        norm = jnp.sqrt(_head_sum(kk_p * kk_p, ones_bd))
        kkn = kk_p / jnp.maximum(norm, 1e-12)
        bonus = _head_sum(rkk[:, sl], ones_bd) * v[:, sl]
        out.append((r[:, sl], k2[:, sl], v[:, sl], logw[:, sl], kkn, a[:, sl], g[:, sl], bonus))
    return out


def _rwkv_prep_kernel(z_ref, zp_ref, mu_ref, w0_ref, w2_ref, a0_ref, a2_ref, g2_ref, kkp_ref, ka_ref, rk_ref,
                      at_ref, btp_ref, ktp_ref, rt_ref, v_ref, g_ref, bonus_ref, w_ref, pc_ref, cs_scr,
                      *, tb, blocks_per_seq):
    i = pl.program_id(0)
    z = z_ref[...]
    first = (i % blocks_per_seq) == 0
    prev_row = jnp.where(first, 0.0, zp_ref[7:8, :])
    row = lax.broadcasted_iota(jnp.int32, (tb, 1), 0)
    zprev = jnp.where(row == 0, prev_row, pltpu.roll(z, 1, axis=0))
    zs = z + (zprev - z) * mu_ref[...]
    ones_bd = _head_block_ones()
    mixed = _rwkv_mix(zs, w0_ref[...], w2_ref[...], a0_ref[...], a2_ref[...], g2_ref[...], kkp_ref[...],
                      ka_ref[...], rk_ref[...], ones_bd)
    n_chunks = tb // RWKV_CHUNK
    shape3 = (n_chunks, RWKV_CHUNK, V7X_LANES)
    tr = lax.broadcasted_iota(jnp.int32, (n_chunks, RWKV_CHUNK, RWKV_CHUNK), 1)
    tc = lax.broadcasted_iota(jnp.int32, (n_chunks, RWKV_CHUNK, RWKV_CHUNK), 2)
    tri = (tc <= tr).astype(BF16)
    bdot = lambda x: lax.dot_general(tri, x, (((2,), (1,)), ((0,), (0,))), preferred_element_type=F32)
    for p, (r, k2, v, logw, kkn, a, g, bonus) in enumerate(mixed):
        hi, mid, lo = _split3_bf16(logw.reshape(shape3))
        cs3 = bdot(hi) + bdot(mid) + bdot(lo)
        total = cs3[:, RWKV_CHUNK - 1:RWKV_CHUNK, :]
        cs = cs3.reshape(tb, V7X_LANES)
        tail = jnp.exp(total - cs3).reshape(tb, V7X_LANES)
        cs_scr[...] = cs
        cs_end = cs_scr[pl.ds(RWKV_CHUNK - 1, n_chunks, stride=RWKV_CHUNK), :]
        inv_p = jnp.exp(-cs)
        at_ref[p] = (jnp.exp(cs - logw) * kkn).astype(BF16)
        btp_ref[p] = (kkn * a * tail).astype(BF16)
        ktp_ref[p] = (k2 * tail).astype(BF16)
        rt_ref[p] = (jnp.exp(cs) * r).astype(BF16)
        v_ref[p] = v.astype(BF16)
        g_ref[p] = g
        bonus_ref[p] = bonus
        w_ref[p] = jnp.concatenate([kkn * a * inv_p, k2 * inv_p], axis=-1).astype(BF16)
        pc_ref[p] = jnp.exp(cs_end)


def rwkv_prep(z_r, n_tokens, seq_len, mu, w0, w2p, a0, a2p, g2, kkp, ka, rk, *, tb=512):
    n_chunks = n_tokens // RWKV_CHUNK
    row_spec = lambda w: pl.BlockSpec((1, w), lambda i: (0, 0))
    pair_out = lambda w: pl.BlockSpec((HEAD_PAIRS, tb, w), lambda i: (0, i, 0))
    pair_shape = lambda w, dt=BF16: jax.ShapeDtypeStruct((HEAD_PAIRS, n_tokens, w), dt)
    return pl.pallas_call(
        functools.partial(_rwkv_prep_kernel, tb=tb, blocks_per_seq=seq_len // tb),
        grid=(n_tokens // tb,),
        in_specs=[pl.BlockSpec((tb, RWKV_COLS), lambda i: (i, 0)),
                  pl.BlockSpec((8, RWKV_COLS), lambda i: (jnp.maximum(i * (tb // 8) - 1, 0), 0)),
                  row_spec(RWKV_COLS), row_spec(RWKV_D),
                  pl.BlockSpec((LORA_W + LORA_A, RWKV_D), lambda i: (0, 0)),
                  row_spec(RWKV_D),
                  pl.BlockSpec((LORA_W + LORA_A, RWKV_D), lambda i: (0, 0)),
                  pl.BlockSpec((LORA_G, RWKV_D), lambda i: (0, 0)),
                  row_spec(RWKV_D), row_spec(RWKV_D), row_spec(RWKV_D)],
        out_specs=[pair_out(V7X_LANES)] * 7 + [pair_out(2 * V7X_LANES),
                   pl.BlockSpec((HEAD_PAIRS, tb // RWKV_CHUNK, V7X_LANES), lambda i: (0, i, 0))],
        out_shape=[pair_shape(V7X_LANES)] * 5 + [pair_shape(V7X_LANES, F32)] * 2 + [pair_shape(2 * V7X_LANES),
                   jax.ShapeDtypeStruct((HEAD_PAIRS, n_chunks, V7X_LANES), F32)],
        scratch_shapes=[pltpu.VMEM((tb, V7X_LANES), F32)],
        compiler_params=_params(("parallel",)),
        name="rwkv_prep",
    )(z_r, z_r, mu, w0, w2p, a0, a2p, g2, kkp, ka, rk)


def _bmm(x, y):
    return lax.dot_general(x.astype(BF16), y.astype(BF16), (((2,), (1,)), ((0,), (0,))),
                           preferred_element_type=F32)


def _bmm_nt(x, y):
    return lax.dot_general(x.astype(BF16), y.astype(BF16), (((2,), (2,)), ((0,), (0,))),
                           preferred_element_type=F32)


def _mm_tn(x, y, precision=None):
    return lax.dot_general(x, y, (((0,), (0,)), ((), ())), preferred_element_type=F32, precision=precision)


def _group_norm_gate(y, bonus, g, lng, lnb, ones_bd):
    mean = _head_sum(y, ones_bd) * (1.0 / RWKV_HD)
    yc = y - mean
    var = _head_sum(yc * yc, ones_bd) * (1.0 / RWKV_HD)
    return (yc * lax.rsqrt(var + GN_EPS) * lng + lnb + bonus) * g


def _rwkv_chunk_kernel(at_ref, btp_ref, ktp_ref, rt_ref, v_ref, g_ref, bonus_ref, w_ref, pc_ref, lng_ref,
                       lnb_ref, buf_any, o_ref, st_ref, st_scr):
    del buf_any
    nc, c, ln = RWKV_CHUNKS_PER_STEP, RWKV_CHUNK, V7X_LANES
    t = pl.program_id(1)

    @pl.when(t == 0)
    def _():
        st_scr[...] = jnp.zeros_like(st_scr)

    shape3 = (nc, c, ln)
    at = at_ref[0].reshape(shape3)
    rt = rt_ref[0].reshape(shape3)
    v = v_ref[0].reshape(shape3)
    btk = w_ref[0].reshape(nc, c, 2 * ln)
    bt, kt = btk[..., :ln], btk[..., ln:]
    lane = lax.broadcasted_iota(jnp.int32, (1, 1, ln), 2)
    head0 = lane < RWKV_HD
    head0_2 = jnp.concatenate([head0, head0], axis=-1)
    zero = jnp.zeros(shape3, F32)

    lhs = jnp.concatenate([jnp.where(head0, at, 0.0), jnp.where(head0, rt, 0.0),
                           jnp.where(head0, 0.0, at), jnp.where(head0, 0.0, rt)], axis=1)
    gram = _bmm_nt(lhs, jnp.concatenate([bt, kt], axis=1))
    gr = lax.broadcasted_iota(jnp.int32, (1, 4 * c, 2 * c), 1)
    gc = lax.broadcasted_iota(jnp.int32, (1, 4 * c, 2 * c), 2) % c
    causal = gr % c + (gr // c) % 2 > gc
    gram = jnp.where(causal, gram, 0.0)
    zv = jnp.concatenate([jnp.zeros(shape3, BF16), v], axis=1)
    first = lax.broadcasted_iota(jnp.int32, (1, 1, 2 * c), 2) < c

    xs, mvs, mwus = [], [], []
    for h in range(2):
        g_h = gram[:, 2 * h * c:(2 * h + 2) * c, :]
        lmv = _bmm(g_h, zv)
        x = jnp.concatenate([at.astype(F32), lmv[:, :c, :]], axis=-1)
        lp = jnp.where(first, -g_h[:, :c, :], 0.0)
        pad2 = jnp.zeros((nc, c, 2 * ln), F32)
        x = x + _bmm(lp, jnp.concatenate([x, pad2], axis=1))
        n = 2
        while n < c:
            lp = _bmm(lp, jnp.concatenate([lp, zero], axis=1))
            x = x + _bmm(lp, jnp.concatenate([x, pad2], axis=1))
            n *= 2
        xs.append(x)
        mvs.append(lmv[:, c:, :])
        mrb = jnp.where(first, g_h[:, c:, :], 0.0)
        mwus.append(_bmm(mrb, jnp.concatenate([x, pad2], axis=1)))
    x = jnp.where(head0_2, xs[0], xs[1])
    mv = jnp.where(head0, mvs[0], mvs[1])
    mwu = jnp.where(head0_2, mwus[0], mwus[1])
    qe = (rt.astype(F32) - mwu[..., :ln]).astype(BF16)
    yl = mv - mwu[..., ln:]

    rr = lax.broadcasted_iota(jnp.int32, (ln, ln), 0)
    cc = lax.broadcasted_iota(jnp.int32, (ln, ln), 1)
    same_head = (rr // RWKV_HD) == (cc // RWKV_HD)
    eye = rr == cc
    btp = btp_ref[0].reshape(shape3).astype(BF16)
    ktp = ktp_ref[0].reshape(shape3).astype(BF16)
    x_bf = x.astype(BF16)
    v_bf = v.astype(BF16)
    pc = pc_ref[0]
    st = st_scr[...]
    ys = []
    for j in range(nc):
        bwu = _mm_tn(btp[j], x_bf[j])
        kv = _mm_tn(ktp[j], v_bf[j])
        tr = jnp.where(same_head, jnp.where(eye, pc[j:j + 1, :], 0.0) - bwu[:, :ln], 0.0)
        ad = jnp.where(same_head, kv - bwu[:, ln:], 0.0)
        st_hi = st.astype(BF16)
        st_lo = (st - st_hi.astype(F32)).astype(BF16)
        ys.append(jnp.dot(qe[j], st_hi, preferred_element_type=F32) + yl[j])
        tr_bf = tr.astype(BF16)
        st = (jnp.dot(tr_bf, st_hi, preferred_element_type=F32)
              + jnp.dot(tr_bf, st_lo, preferred_element_type=F32) + ad)
    st_scr[...] = st
    st_ref[0] = st
    y = jnp.concatenate(ys, axis=0)
    o = _group_norm_gate(y, bonus_ref[0], g_ref[0], lng_ref[0], lnb_ref[0], _head_block_ones())
    o_ref[...] = o.astype(o_ref.dtype)


def rwkv_chunks(prep, buf, n_batch, seq_len, lnx_g, lnx_b):
    at, btp, ktp, rt, v, g, bonus, wbk, pc = prep
    rows = RWKV_CHUNK * RWKV_CHUNKS_PER_STEP
    steps = seq_len // rows
    tok = lambda w: pl.BlockSpec((1, rows, w), lambda bp, t: (bp % HEAD_PAIRS, (bp // HEAD_PAIRS) * steps + t, 0))
    pair_row = pl.BlockSpec((1, 1, V7X_LANES), lambda bp, t: (bp % HEAD_PAIRS, 0, 0))
    return pl.pallas_call(
        _rwkv_chunk_kernel,
        grid=(n_batch * HEAD_PAIRS, steps),
        in_specs=[tok(V7X_LANES)] * 7 + [tok(2 * V7X_LANES),
                  pl.BlockSpec((1, RWKV_CHUNKS_PER_STEP, V7X_LANES),
                               lambda bp, t: (bp % HEAD_PAIRS, (bp // HEAD_PAIRS) * steps + t, 0)),
                  pair_row, pair_row, pl.BlockSpec(memory_space=pl.ANY)],
        out_specs=[pl.BlockSpec((rows, V7X_LANES),
                                lambda bp, t: ((bp // HEAD_PAIRS) * steps + t, bp % HEAD_PAIRS)),
                   pl.BlockSpec((1, V7X_LANES, V7X_LANES), lambda bp, t: (bp, 0, 0))],
        out_shape=[jax.ShapeDtypeStruct(buf.shape, buf.dtype),
                   jax.ShapeDtypeStruct((n_batch * HEAD_PAIRS, V7X_LANES, V7X_LANES), F32)],
        input_output_aliases={11: 0},
        scratch_shapes=[pltpu.VMEM((V7X_LANES, V7X_LANES), F32)],
        compiler_params=_params(("parallel", "arbitrary")),
        name="rwkv_chunks",
    )(at, btp, ktp, rt, v, g, bonus, wbk, pc,
      lnx_g.reshape(HEAD_PAIRS, 1, V7X_LANES), lnx_b.reshape(HEAD_PAIRS, 1, V7X_LANES), buf)


def _softmax_rows(s):
    e = jnp.exp(s - jnp.max(s, axis=-1, keepdims=True))
    return e / jnp.sum(e, axis=-1, keepdims=True)


def _xattn_kernel(q_ref, k_ref, v_ref, buf_any, o_ref):
    del buf_any
    for h in range(XA_HEADS):
        sl = slice(h * XA_HD, (h + 1) * XA_HD)
        s = lax.dot_general(q_ref[:, sl].astype(BF16), k_ref[:, sl].astype(BF16), (((1,), (1,)), ((), ())),
                            preferred_element_type=F32) * (XA_HD ** -0.5)
        pr = _softmax_rows(s)
        o_ref[:, sl] = jnp.dot(pr.astype(BF16), v_ref[:, sl].astype(BF16),
                               preferred_element_type=F32).astype(o_ref.dtype)


def xattn_prompt(z_q, kv, buf, n_batch, seq_len, *, tq=512):
    steps = seq_len // tq
    return pl.pallas_call(
        _xattn_kernel,
        grid=(n_batch, steps),
        in_specs=[pl.BlockSpec((tq, XA_D), lambda b, t: (b * steps + t, 0)),
                  pl.BlockSpec((N_MEM, XA_D), lambda b, t: (b, 0)),
                  pl.BlockSpec((N_MEM, XA_D), lambda b, t: (b, 1)),
                  pl.BlockSpec(memory_space=pl.ANY)],
        out_specs=pl.BlockSpec((tq, XA_D), lambda b, t: (b * steps + t, 0)),
        out_shape=jax.ShapeDtypeStruct(buf.shape, buf.dtype),
        input_output_aliases={3: 0},
        compiler_params=_params(("parallel", "parallel")),
        name="xattn_prompt",
    )(z_q, kv, kv, buf)


def _sample_mix_kernel(zg_ref, zr_ref, shift_ref, lng_ref, lnb_ref, gw_ref, gb_ref,
                       mu_ref, w0_ref, w2_ref, a0_ref, a2_ref, g2_ref, kkp_ref, ka_ref, rk_ref, og_any,
                       og_ref, gv_ref, wt_ref, kkt_ref, bt_ref, kt_ref, rt_ref, vt_ref, bonus_ref, g_ref):
    del og_any
    ge = _gelu(zg_ref[...])
    v = _layernorm(ge[:, GMLP_D:], lng_ref[...], lnb_ref[...])
    gv_ref[...] = v
    og_ref[...] = (ge[:, :GMLP_D] * (v * gw_ref[...] + gb_ref[...])).astype(og_ref.dtype)

    z = zr_ref[...]
    zs = z + (shift_ref[...] - z) * mu_ref[...]
    mixed = _rwkv_mix(zs, w0_ref[...], w2_ref[...], a0_ref[...], a2_ref[...], g2_ref[...], kkp_ref[...],
                      ka_ref[...], rk_ref[...], _head_block_ones())
    for p, (r, k2, vv, logw, kkn, a, g, bonus) in enumerate(mixed):
        sl = slice(p * V7X_LANES, (p + 1) * V7X_LANES)
        wt_ref[sl, :] = jnp.exp(logw).T
        kkt_ref[sl, :] = kkn.T
        bt_ref[sl, :] = (kkn * a).T
        kt_ref[sl, :] = k2.T
        rt_ref[sl, :] = r.T
        vt_ref[sl, :] = vv.T
        bonus_ref[:, sl] = bonus
        g_ref[:, sl] = g


def sample_mix(z_g, z_r, row0, shift, o_g, ln_g, ln_b, gw, gb, mu, w0, w2p, a0, a2p, g2, kkp, ka, rk):
    n = shift.shape[0]
    blk = row0 // n
    tok = lambda w: pl.BlockSpec((n, w), lambda i: (blk, 0))
    loc = lambda w: pl.BlockSpec((n, w), lambda i: (0, 0))
    row = lambda w: pl.BlockSpec((1, w), lambda i: (0, 0))
    full2 = lambda a, b: pl.BlockSpec((a, b), lambda i: (0, 0))
    chan = jax.ShapeDtypeStruct((RWKV_D, n), F32)
    return pl.pallas_call(
        _sample_mix_kernel,
        grid=(1,),
        in_specs=[tok(GMLP_COLS), tok(RWKV_COLS), loc(RWKV_COLS),
                  row(GMLP_D), row(GMLP_D), row(GMLP_D), row(GMLP_D),
                  row(RWKV_COLS), row(RWKV_D), full2(LORA_W + LORA_A, RWKV_D), row(RWKV_D),
                  full2(LORA_W + LORA_A, RWKV_D), full2(LORA_G, RWKV_D), row(RWKV_D), row(RWKV_D), row(RWKV_D),
                  pl.BlockSpec(memory_space=pl.ANY)],
        out_specs=[tok(GMLP_D), loc(GMLP_D)] + [full2(RWKV_D, n)] * 6 + [loc(RWKV_D), loc(RWKV_D)],
        out_shape=[jax.ShapeDtypeStruct(o_g.shape, o_g.dtype), jax.ShapeDtypeStruct((n, GMLP_D), F32)]
                  + [chan] * 6 + [jax.ShapeDtypeStruct((n, RWKV_D), F32)] * 2,
        input_output_aliases={16: 0},
        compiler_params=_params(("arbitrary",)),
        name="sample_mix",
    )(z_g, z_r, shift, ln_g.reshape(1, -1), ln_b.reshape(1, -1), gw, gb, mu, w0, w2p, a0, a2p, g2, kkp, ka, rk, o_g)


def _sample_state_kernel(s_ref, w_ref, kk_ref, b_ref, k_ref, r_ref, v_ref, snew_ref, y_ref):
    n = s_ref.shape[0]
    st = s_ref[...].T.reshape(RWKV_HD, RWKV_HD, n)
    sa = jnp.sum(st * (-kk_ref[...])[None], axis=1, keepdims=True)
    st_new = st * w_ref[...][None] + sa * b_ref[...][None] + v_ref[...] * k_ref[...][None]
    y_ref[...] = jnp.sum(st_new * r_ref[...][None], axis=1, keepdims=True)
    snew_ref[...] = st_new.reshape(RWKV_HD * RWKV_HD, n).T


def sample_state(s0, wt, kkt, bt, kt, rt, vt):
    n = s0.shape[0]
    hd2 = RWKV_HD * RWKV_HD
    state = pl.BlockSpec((n, hd2), lambda h: (0, h))
    keyvec = pl.BlockSpec((RWKV_HD, n), lambda h: (h, 0))
    valvec = pl.BlockSpec((RWKV_HD, 1, n), lambda h: (h, 0, 0))
    return pl.pallas_call(
        _sample_state_kernel,
        grid=(RWKV_HEADS,),
        in_specs=[state, keyvec, keyvec, keyvec, keyvec, keyvec, valvec],
        out_specs=[state, valvec],
        out_shape=[jax.ShapeDtypeStruct((n, RWKV_HEADS * hd2), F32),
                   jax.ShapeDtypeStruct((RWKV_D, 1, n), F32)],
        compiler_params=_params(("parallel",)),
        name="sample_state",
    )(s0.reshape(n, RWKV_HEADS * hd2), wt, kkt, bt, kt, rt, vt.reshape(RWKV_D, 1, n))


def _sample_rwkv_out_kernel(yt_ref, bonus_ref, g_ref, lng_ref, lnb_ref, or_any, or_ref):
    del or_any
    y = yt_ref[...].T
    ones_bd = _head_block_ones()
    for p in range(HEAD_PAIRS):
        sl = slice(p * V7X_LANES, (p + 1) * V7X_LANES)
        o = _group_norm_gate(y[:, sl], bonus_ref[:, sl], g_ref[:, sl], lng_ref[p], lnb_ref[p], ones_bd)
        or_ref[:, sl] = o.astype(or_ref.dtype)


def sample_rwkv_out(yt, bonus, g, lnx_g, lnx_b, o_r, row0):
    n = bonus.shape[0]
    full2 = lambda a, b: pl.BlockSpec((a, b), lambda i: (0, 0))
    pair_row = pl.BlockSpec((HEAD_PAIRS, 1, V7X_LANES), lambda i: (0, 0, 0))
    return pl.pallas_call(
        _sample_rwkv_out_kernel,
        grid=(1,),
        in_specs=[full2(RWKV_D, n), full2(n, RWKV_D), full2(n, RWKV_D), pair_row, pair_row,
                  pl.BlockSpec(memory_space=pl.ANY)],
        out_specs=pl.BlockSpec((n, RWKV_D), lambda i: (row0 // n, 0)),
        out_shape=jax.ShapeDtypeStruct(o_r.shape, o_r.dtype),
        input_output_aliases={5: 0},
        compiler_params=_params(("arbitrary",)),
        name="sample_rwkv_out",
    )(yt, bonus, g, lnx_g.reshape(HEAD_PAIRS, 1, V7X_LANES), lnx_b.reshape(HEAD_PAIRS, 1, V7X_LANES), o_r)


def _xattn_sample_kernel(q_ref, ck_ref, cv_ref, ox_any, ox_ref, *, sb):
    del ox_any
    q = q_ref[...]
    rows = N_MEM * XA_HEADS
    hrow = lax.broadcasted_iota(jnp.int32, (8, rows), 0) % XA_HEADS
    hcol = lax.broadcasted_iota(jnp.int32, (8, rows), 1) % XA_HEADS
    own = hrow == hcol
    h8 = lax.broadcasted_iota(jnp.int32, (8, XA_HD), 0) % XA_HEADS
    nt = (((1,), (1,)), ((), ()))
    outs = []
    for s in range(sb):
        qm = jnp.zeros((8, XA_HD), F32)
        for h in range(XA_HEADS):
            qm = jnp.where(h8 == h, q[s:s + 1, h * XA_HD:(h + 1) * XA_HD], qm)
        sc = lax.dot_general(qm.astype(BF16), ck_ref[s].astype(BF16), nt,
                             preferred_element_type=F32) * (XA_HD ** -0.5)
        pr = _softmax_rows(jnp.where(own, sc, NEG_INF))
        res = jnp.dot(pr.astype(BF16), cv_ref[s].astype(BF16), preferred_element_type=F32)
        outs.append(jnp.concatenate([res[h:h + 1, :] for h in range(XA_HEADS)], axis=1))
    ox_ref[...] = jnp.concatenate(outs, axis=0).astype(ox_ref.dtype)


def xattn_sample(z_q, ck, cv, o_x, row0, *, sb=8):
    n = ck.shape[0]
    off = row0 // sb
    tok = pl.BlockSpec((sb, XA_D), lambda i: (i + off, 0))
    cache = pl.BlockSpec((sb, N_MEM * XA_HEADS, XA_HD), lambda i: (i, 0, 0))
    return pl.pallas_call(
        functools.partial(_xattn_sample_kernel, sb=sb),
        grid=(n // sb,),
        in_specs=[tok, cache, cache, pl.BlockSpec(memory_space=pl.ANY)],
        out_specs=tok,
        out_shape=jax.ShapeDtypeStruct(o_x.shape, o_x.dtype),
        input_output_aliases={3: 0},
        compiler_params=_params(("arbitrary",)),
        name="xattn_sample",
    )(z_q, ck, cv, o_x)


def _merge_kernel(og_ref, or_ref, ox_ref, g0_ref, g1_ref, g2_ref, wg_ref, wr_ref, wx_ref, o_ref,
                  wg_bf, wr_bf, wx_bf):
    @pl.when(pl.program_id(1) == 0)
    def _():
        wg_bf[...] = wg_ref[...].astype(BF16)
        wr_bf[...] = wr_ref[...].astype(BF16)
        wx_bf[...] = wx_ref[...].astype(BF16)

    up_g = jnp.dot(og_ref[...], wg_bf[...], preferred_element_type=F32)
    up_r = jnp.dot(or_ref[...], wr_bf[...], preferred_element_type=F32)
    up_x = jnp.dot(ox_ref[...], wx_bf[...], preferred_element_type=F32)
    gate = lambda ref: _sigmoid(ref[...].astype(F32))
    merged = gate(g0_ref) * up_g + gate(g1_ref) * up_r + gate(g2_ref) * up_x
    o_ref[...] = merged.astype(o_ref.dtype)


def merge(o_g, o_r, o_x, z_gate, w_up_g, w_up_r, w_up_x, *, tb, nb=512):
    m = o_g.shape[0]
    nblk = D_MODEL // nb
    gate = lambda b: pl.BlockSpec((tb, nb), lambda j, i: (i, b * nblk + j))
    wspec = lambda k: pl.BlockSpec((k, nb), lambda j, i: (0, j))
    return pl.pallas_call(
        _merge_kernel,
        grid=(nblk, m // tb),
        in_specs=[pl.BlockSpec((tb, GMLP_D), lambda j, i: (i, 0)),
                  pl.BlockSpec((tb, RWKV_D), lambda j, i: (i, 0)),
                  pl.BlockSpec((tb, XA_D), lambda j, i: (i, 0)),
                  gate(0), gate(1), gate(2), wspec(GMLP_D), wspec(RWKV_D), wspec(XA_D)],
        out_specs=pl.BlockSpec((tb, nb), lambda j, i: (i, j)),
        out_shape=jax.ShapeDtypeStruct((m, D_MODEL), BF16),
        scratch_shapes=[pltpu.VMEM((GMLP_D, nb), BF16), pltpu.VMEM((RWKV_D, nb), BF16),
                        pltpu.VMEM((XA_D, nb), BF16)],
        compiler_params=_params(("arbitrary", "arbitrary")),
        name="merge",
    )(o_g, o_r, o_x, z_gate, z_gate, z_gate, w_up_g, w_up_r, w_up_x)


def _extract_top(src_ref, work_ref, rank_ref, vals_ref, n_rows):
    width = work_ref.shape[1]
    riota = lax.broadcasted_iota(jnp.int32, (n_rows, V7X_LANES), 0)
    kiota = lax.broadcasted_iota(jnp.int32, (TOPK, V7X_LANES), 0)

    def run(break_ties):
        work_ref[...] = src_ref[...]
        vals_ref[...] = jnp.zeros_like(vals_ref)

        def body(p, carry):
            taken = TAKEN_BASE - jnp.asarray(p, F32) * TAKEN_STEP
            for c in range(width // V7X_LANES):
                sl = slice(c * V7X_LANES, (c + 1) * V7X_LANES)
                w = work_ref[:, sl]
                m = jnp.max(w, axis=0, keepdims=True)
                if break_ties:
                    idx = jnp.min(jnp.where(w == m, riota, n_rows), axis=0, keepdims=True)
                    hit = riota == idx
                else:
                    hit = w == m
                work_ref[:, sl] = jnp.where(hit, taken, w)
                vals_ref[:, sl] = jnp.where(kiota == p, m, vals_ref[:, sl])
            return carry

        lax.fori_loop(0, TOPK, body, 0)

    run(False)
    picked = jnp.sum(jnp.where(work_ref[...] < TAKEN_BELOW, 1.0, 0.0), axis=0, keepdims=True)
    tied = jnp.max(picked) > TOPK

    @pl.when(tied)
    def _():
        run(True)

    w = work_ref[...]
    rank = jnp.where(w < TAKEN_BELOW, w * (-1.0 / TAKEN_STEP) + (TAKEN_BASE / TAKEN_STEP), float(TOPK))
    rank_ref[...] = rank.astype(jnp.int32)


_CAND_COUNT = tuple(TOPK // (a + 1) for a in range(TOPK))
_CAND_START = tuple(sum(_CAND_COUNT[:a]) for a in range(TOPK))
_CAND_ROWS = -(-sum(_CAND_COUNT) // 8) * 8


def _peer_topk_kernel(q_ref, keys_ref, r2_ref, lim_ref, e1_ref, e2_ref,
                      s_scr, work_scr, rank_scr, vals_scr, cand_scr, cwork_scr, crank_scr, cvals_scr, *, tbk):
    nt = (((1,), (1,)), ((), ()))
    for h in range(PEER_HEADS):
        for c in range(2):
            qcol = (2 * h + c) * PEER_DH
            col = (2 * h + c) * tbk
            k_hi, k_lo = _split_bf16(keys_ref[h, c])
            q_hi, q_lo = _split_bf16(q_ref[:, qcol:qcol + PEER_DH])
            sc = (lax.dot_general(k_hi, q_hi, nt, preferred_element_type=F32)
                  + lax.dot_general(k_hi, q_lo, nt, preferred_element_type=F32)
                  + lax.dot_general(k_lo, q_hi, nt, preferred_element_type=F32))
            s_scr[:, col:col + tbk] = sc
    _extract_top(s_scr, work_scr, rank_scr, vals_scr, N_KEYS)

    crow = lax.broadcasted_iota(jnp.int32, (_CAND_ROWS, tbk), 0)
    seg = jnp.full((_CAND_ROWS, tbk), TOPK, jnp.int32)
    for a in reversed(range(TOPK)):
        seg = jnp.where(crow < _CAND_START[a] + _CAND_COUNT[a], jnp.minimum(seg, a), seg)
    pad = jnp.zeros((_CAND_ROWS - TOPK, tbk), F32)
    for h in range(PEER_HEADS):
        v1 = vals_scr[:, (2 * h) * tbk:(2 * h + 1) * tbk]
        v2 = jnp.concatenate([vals_scr[:, (2 * h + 1) * tbk:(2 * h + 2) * tbk], pad], axis=0)
        cand = jnp.full((_CAND_ROWS, tbk), CAND_PAD, F32)
        for a in range(TOPK):
            shifted = v2 if _CAND_START[a] == 0 else pltpu.roll(v2, _CAND_START[a], axis=0)
            cand = jnp.where(seg == a, v1[a:a + 1, :] + shifted, cand)
        cand_scr[:, h * tbk:(h + 1) * tbk] = cand
    _extract_top(cand_scr, cwork_scr, crank_scr, cvals_scr, _CAND_ROWS)

    for h in range(PEER_HEADS):
        hs = slice(h * tbk, (h + 1) * tbk)
        s1 = slice((2 * h) * tbk, (2 * h + 1) * tbk)
        s2 = slice((2 * h + 1) * tbk, (2 * h + 2) * tbk)
        cvals = cvals_scr[:, hs]
        z = jnp.sum(jnp.exp(cvals - cvals[0:1, :]), axis=0, keepdims=True)
        chosen = crank_scr[:, hs] < TOPK
        rank1 = rank_scr[:, s1]
        lim = jnp.zeros((N_KEYS, tbk), F32)
        for a in range(TOPK):
            count = jnp.sum(jnp.where(jnp.logical_and(chosen, seg == a), 1.0, 0.0), axis=0, keepdims=True)
            lim = jnp.where(rank1 == a, count, lim)
        lim_ref[h] = lim
        r2_ref[h] = rank_scr[:, s2].astype(F32).astype(r2_ref.dtype)
        e1_ref[h] = jnp.exp(s_scr[:, s1] - vals_scr[0:1, s1]) / z
        e2_ref[h] = jnp.exp(s_scr[:, s2] - vals_scr[0:1, s2]).astype(e2_ref.dtype)


def peer_topk(q, keys, *, tbk=128):
    m = q.shape[0]
    out = pl.BlockSpec((PEER_HEADS, N_KEYS, tbk), lambda i: (0, 0, i))
    shape = lambda dt: jax.ShapeDtypeStruct((PEER_HEADS, N_KEYS, m), dt)
    wide = 2 * PEER_HEADS * tbk
    return pl.pallas_call(
        functools.partial(_peer_topk_kernel, tbk=tbk),
        grid=(m // tbk,),
        in_specs=[pl.BlockSpec((tbk, PEER_HEADS * 2 * PEER_DH), lambda i: (i, 0)),
                  pl.BlockSpec((PEER_HEADS, 2, N_KEYS, PEER_DH), lambda i: (0, 0, 0, 0))],
        out_specs=[out] * 4,
        out_shape=[shape(BF16), shape(F32), shape(F32), shape(BF16)],
        scratch_shapes=[pltpu.VMEM((N_KEYS, wide), F32), pltpu.VMEM((N_KEYS, wide), F32),
                        pltpu.VMEM((N_KEYS, wide), jnp.int32), pltpu.VMEM((TOPK, wide), F32),
                        pltpu.VMEM((_CAND_ROWS, PEER_HEADS * tbk), F32),
                        pltpu.VMEM((_CAND_ROWS, PEER_HEADS * tbk), F32),
                        pltpu.VMEM((_CAND_ROWS, PEER_HEADS * tbk), jnp.int32),
                        pltpu.VMEM((TOPK, PEER_HEADS * tbk), F32)],
        compiler_params=_params(("parallel",)),
        name="peer_topk",
    )(q, keys)


PEER_SUB = 2 * N_KEYS
TOKEN_TILE = 768


def _peer_dense_kernel(xt_ref, r2_ref, lim_ref, e1_ref, e2_ref, u_ref, v_ref, o_ref, coef_a, coef_b, ht_scr,
                       *, eb, n_blocks):
    e = pl.program_id(1)
    blk = jnp.minimum(e, n_blocks - 1)
    n_sub = eb // PEER_SUB
    d_sub = o_ref.shape[1] // n_sub

    def hidden(s):
        ht_scr[s % 2] = jnp.dot(u_ref[s * PEER_SUB:(s + 1) * PEER_SUB, :], xt_ref[...],
                                preferred_element_type=F32)

    def step(prev_ref, next_ref):
        hidden(0)
        for s in range(n_sub):
            ht = ht_scr.at[s % 2]
            cols = slice(s * d_sub, (s + 1) * d_sub)
            if prev_ref is not None:
                o_ref[:, cols] += lax.dot_general(prev_ref[...], v_ref[:, cols], (((0,), (0,)), ((), ())),
                                                  preferred_element_type=F32)
            if s + 1 < n_sub:
                hidden(s + 1)
            for ii in range(PEER_SUB // N_KEYS):
                i = blk * (eb // N_KEYS) + s * (PEER_SUB // N_KEYS) + ii
                row = s * PEER_SUB + ii * N_KEYS
                gate = None
                for h in range(PEER_HEADS):
                    lim = lim_ref[h, pl.ds(i, 1), :].astype(BF16)
                    e1 = e1_ref[h, pl.ds(i, 1), :].astype(BF16)
                    term = jnp.where(r2_ref[h] < lim, e2_ref[h] * e1, jnp.zeros((), BF16))
                    gate = term if gate is None else gate + term
                next_ref[row:row + N_KEYS, :] = gate * _gelu_to_bf16(ht[ii * N_KEYS:(ii + 1) * N_KEYS, :])

    @pl.when(e == 0)
    def _():
        o_ref[...] = jnp.zeros_like(o_ref)
        step(None, coef_a)

    @pl.when(jnp.logical_and(e % 2 == 0, jnp.logical_and(e > 0, e < n_blocks)))
    def _():
        step(coef_b, coef_a)

    @pl.when(e % 2 == 1)
    def _():
        step(coef_a, coef_b)

    @pl.when(e == n_blocks)
    def _():
        o_ref[...] += lax.dot_general(coef_b[...], v_ref[...], (((0,), (0,)), ((), ())),
                                      preferred_element_type=F32)


def peer_dense(xt, r2, lim, e1, e2, u_bf, v_bf, *, tbl, eb=1024):
    d, m = xt.shape
    n_blocks = u_bf.shape[0] // eb
    assert n_blocks % 2 == 0
    once = pl.Buffered(1)
    head = pl.BlockSpec((PEER_HEADS, N_KEYS, tbl), lambda t, e: (0, 0, t), pipeline_mode=once)
    return pl.pallas_call(
        functools.partial(_peer_dense_kernel, eb=eb, n_blocks=n_blocks),
        grid=(m // tbl, n_blocks + 1),
        in_specs=[pl.BlockSpec((d, tbl), lambda t, e: (0, t)), head, head, head, head,
                  pl.BlockSpec((eb, d), lambda t, e: (jnp.minimum(e, n_blocks - 1), 0)),
                  pl.BlockSpec((eb, d), lambda t, e: (jnp.maximum(e - 1, 0), 0))],
        out_specs=pl.BlockSpec((tbl, d), lambda t, e: (t, 0)),
        out_shape=jax.ShapeDtypeStruct((m, d), F32),
        scratch_shapes=[pltpu.VMEM((eb, tbl), BF16), pltpu.VMEM((eb, tbl), BF16),
                        pltpu.VMEM((2, PEER_SUB, tbl), F32)],
        compiler_params=_params(("parallel", "arbitrary")),
        name="peer_dense",
    )(xt, r2, lim, e1, e2, u_bf, v_bf)


def _final_kernel(h_ref, p_ref, g_ref, o_ref):
    x = h_ref[...] + p_ref[...]
    o_ref[...] = x * lax.rsqrt(jnp.mean(x * x, axis=-1, keepdims=True) + RMS_EPS) * g_ref[...]


def final_norm(h, peer, g, row0, n_rows, *, tb):
    d = h.shape[1]
    off = row0 // tb
    rows = pl.BlockSpec((tb, d), lambda i: (i + off, 0))
    return pl.pallas_call(
        _final_kernel,
        grid=(n_rows // tb,),
        in_specs=[rows, rows, pl.BlockSpec((1, d), lambda i: (0, 0))],
        out_specs=pl.BlockSpec((tb, d), lambda i: (i, 0)),
        out_shape=jax.ShapeDtypeStruct((n_rows, d), F32),
        compiler_params=_params(("parallel",)),
        name="final_norm",
    )(h, peer, g.reshape(1, d))


def kernel(x_prompt, x_sample, mem_prompt, state_shift, state_wkv, cache_mem_k, cache_mem_v, ln1_g, w_in, gmlp_ln_g, gmlp_ln_b, gmlp_ws, gmlp_bs, rwkv_mu, rwkv_w0, rwkv_w2, rwkv_a0, rwkv_a2, rwkv_g2, rwkv_kk, rwkv_ka, rwkv_rk, rwkv_lnx_g, rwkv_lnx_b, mem_norm_g, w_mem_kv, w_up_g, w_up_r, w_up_x, w_out, ln2_g, peer_wq, peer_keys, peer_u, peer_v, final_g):
    depth = w_in.shape[0]
    assert depth == 1, "single-layer step"
    l = 0
    n_batch, seq_len, d = x_prompt.shape
    n_dec = x_sample.shape[0]
    n_prompt = n_batch * seq_len
    m = -(-(n_prompt + n_dec) // TOKEN_TILE) * TOKEN_TILE
    n_pad = m - n_prompt - n_dec
    assert n_pad <= n_prompt
    tb_small = TOKEN_TILE
    tb_big = _largest_divisor(m, (1408, TOKEN_TILE))

    xp_rows = x_prompt.reshape(n_prompt, d)
    x = jnp.concatenate([xp_rows, x_sample.reshape(n_dec, d), xp_rows[:n_pad]], axis=0)
    xn = rmsnorm(x, ln1_g[l], tb=tb_small, out_dtype=BF16)
    proj = functools.partial(matmul, xn, w_in[l], tb=tb_big, nb=512)
    z_g = proj(col_off=0, n_cols=GMLP_COLS, name="proj_gmlp")
    z_r = proj(col_off=GMLP_COLS, n_cols=RWKV_COLS, name="proj_rwkv")
    z_q = proj(col_off=GMLP_COLS + RWKV_COLS, n_cols=XA_D, name="proj_xattn")
    z_gate = proj(col_off=GMLP_COLS + RWKV_COLS + XA_D, n_cols=GATE_COLS, out_dtype=BF16, name="proj_gate")

    memn = rmsnorm(mem_prompt.reshape(n_batch * N_MEM, d), mem_norm_g[l], tb=256, out_dtype=BF16)
    kv = matmul(memn, w_mem_kv[l], tb=N_MEM, nb=512, name="proj_mem_kv")
    p_mk = kv[:, :XA_D].reshape(1, n_batch, N_MEM, XA_HEADS, XA_HD)
    p_mv = kv[:, XA_D:].reshape(1, n_batch, N_MEM, XA_HEADS, XA_HD)

    row = lambda a: a.reshape(1, -1)
    zeros_lora = jnp.zeros((LORA_W, RWKV_D), F32)
    w2p = jnp.concatenate([rwkv_w2[l], zeros_lora], axis=0).astype(BF16)
    a2p = jnp.concatenate([zeros_lora, rwkv_a2[l]], axis=0).astype(BF16)
    rw = (row(rwkv_mu[l]), row(rwkv_w0[l]), w2p, row(rwkv_a0[l]), a2p, rwkv_g2[l].astype(BF16),
          row(rwkv_kk[l]), row(rwkv_ka[l]), row(rwkv_rk[l]))

    o_g = gmlp_prompt(z_g, jnp.zeros((m, GMLP_D), BF16), n_prompt, gmlp_ln_g[l], gmlp_ln_b[l], gmlp_ws[l],
                      gmlp_bs[l])
    prep = rwkv_prep(z_r, n_prompt, seq_len, *rw)
    o_r, st = rwkv_chunks(prep, jnp.zeros((m, RWKV_D), BF16), n_batch, seq_len,
                          rwkv_lnx_g[l], rwkv_lnx_b[l])
    o_x = xattn_prompt(z_q, kv, jnp.zeros((m, XA_D), BF16), n_batch, seq_len)

    gw = jnp.repeat(gmlp_ws[l][:, 0, 0], CHUNK).reshape(1, GMLP_D)
    gb = jnp.repeat(gmlp_bs[l][:, 0], CHUNK).reshape(1, GMLP_D)
    o_g, s_gv, wt, kkt, bt, kt, rt, vt, bonus_s, g_s = sample_mix(
        z_g, z_r, n_prompt, state_shift[l], o_g, gmlp_ln_g[l], gmlp_ln_b[l], gw, gb, *rw)
    s_wkv, y_s = sample_state(state_wkv[l], wt, kkt, bt, kt, rt, vt)
    o_r = sample_rwkv_out(y_s.reshape(RWKV_D, n_dec), bonus_s, g_s, rwkv_lnx_g[l], rwkv_lnx_b[l], o_r, n_prompt)
    o_x = xattn_sample(z_q, cache_mem_k[l].reshape(n_dec, N_MEM * XA_HEADS, XA_HD),
                       cache_mem_v[l].reshape(n_dec, N_MEM * XA_HEADS, XA_HD), o_x, n_prompt)

    merged = merge(o_g, o_r, o_x, z_gate, w_up_g[l], w_up_r[l], w_up_x[l], tb=tb_small)
    h = matmul(merged, w_out[l], tb=tb_big, nb=512, residual=x, name="proj_out")

    hn, hn_t = rmsnorm(h, ln2_g[l], tb=tb_small, out_dtype=BF16, transposed=True)
    q = matmul(hn, peer_wq[l], tb=tb_big, nb=512, name="proj_peer_q")
    r2, lim, e1, e2 = peer_topk(q, peer_keys[l])
    peer = peer_dense(hn_t, r2, lim, e1, e2, peer_u[l].astype(BF16), peer_v[l].astype(BF16), tbl=tb_small)
    y_prompt = final_norm(h, peer, final_g, 0, n_prompt, tb=512)
    y_sample = final_norm(h, peer, final_g, n_prompt, n_dec, tb=n_dec)

    st = st.reshape(n_batch, HEAD_PAIRS, 2, RWKV_HD, 2, RWKV_HD)
    p_wkv = jnp.stack([st[:, :, 0, :, 0, :], st[:, :, 1, :, 1, :]], axis=2)
    p_wkv = jnp.swapaxes(p_wkv, -1, -2).reshape(1, n_batch, RWKV_HEADS, RWKV_HD, RWKV_HD)
    p_shift = jnp.concatenate([z_r[(b + 1) * seq_len - 1:(b + 1) * seq_len] for b in range(n_batch)], axis=0)[None]
    s_shift = z_r[n_prompt:n_prompt + n_dec][None]
    s_wkv = s_wkv.reshape(1, n_dec, RWKV_HEADS, RWKV_HD, RWKV_HD)
    return (y_prompt.reshape(n_batch, seq_len, d), y_sample.reshape(n_dec, 1, d), p_mk, p_mv,
            p_shift, p_wkv, s_shift, s_wkv, s_gv.reshape(1, n_dec, 1, GMLP_D))
```
